```python
import jax, jax.numpy as jnp
from jax import lax
import numpy as np

D_MODEL = 1024
BATCH = 4
SEQ = 8192
DEPTH = 2

CHUNK = 64
LEFT_CHUNKS = 8
BAND = (LEFT_CHUNKS + 1) * CHUNK
MAX_REL = 128
Q_BLOCK = 128

HEAD_DIM = 64
N_HEADS_A = D_MODEL // (2 * HEAD_DIM)
N_HEADS_B = D_MODEL // (2 * HEAD_DIM)
AB_COLS = 3 * (N_HEADS_A + N_HEADS_B) * HEAD_DIM
AB_OUT = (N_HEADS_A + N_HEADS_B) * HEAD_DIM
N_HEADS_C = D_MODEL // (2 * HEAD_DIM)
QK_NOPE = HEAD_DIM
QK_ROPE = HEAD_DIM // 2
V_DIM_C = HEAD_DIM
Q_RANK = 3 * D_MODEL // 8
KV_RANK = D_MODEL // 4
N_HEADS_D = D_MODEL // (2 * HEAD_DIM)
CD_SPLITS = [Q_RANK, KV_RANK, QK_ROPE, N_HEADS_D * HEAD_DIM, N_HEADS_D * HEAD_DIM, N_HEADS_D * HEAD_DIM, N_HEADS_D]
CD_COLS = sum(CD_SPLITS)
CD_OUT = N_HEADS_C * V_DIM_C + N_HEADS_D * HEAD_DIM
ROPE_THETA = 10000.0
D_FF = ((8 * D_MODEL // 3) + 127) // 128 * 128
CONV_WIDTH = 3
RMS_EPS = 1e-6
N_EVEN = (DEPTH + 1) // 2
N_ODD = DEPTH // 2

kernel_name = 'hybrid_chunk_stick_mla_fox_convffn'


def rmsnorm(x, g):
    x32 = x.astype(jnp.float32)
    y = x32 * lax.rsqrt(jnp.mean(x32 * x32, axis=-1, keepdims=True) + RMS_EPS)
    return (y * g.astype(jnp.float32)).astype(x.dtype)


def rope(x, positions):
    half = x.shape[-1] // 2
    inv = ROPE_THETA ** (-jnp.arange(half, dtype=jnp.float32) / half)
    ang = positions.astype(jnp.float32)[:, None] * inv[None, :]
    ang = ang.reshape((ang.shape[0],) + (1,) * (x.ndim - 3) + (half,))
    cos, sin = jnp.cos(ang), jnp.sin(ang)
    x32 = x.astype(jnp.float32)
    x1, x2 = x32[..., :half], x32[..., half:]
    return jnp.concatenate([x1 * cos - x2 * sin, x2 * cos + x1 * sin], axis=-1).astype(x.dtype)


def sweep_query_blocks(block_fn, seq):
    out = lax.map(block_fn, jnp.arange(seq // Q_BLOCK))
    nb, b, qb, h, dv = out.shape
    return jnp.transpose(out, (1, 0, 2, 3, 4)).reshape(b, nb * qb, h, dv)


def chunked_relpos_attention(q, k, v, rel_bias):
    b, s, h, d = q.shape
    pad = LEFT_CHUNKS * CHUNK
    kp = jnp.pad(k, ((0, 0), (pad, 0), (0, 0), (0, 0)))
    vp = jnp.pad(v, ((0, 0), (pad, 0), (0, 0), (0, 0)))
    qi = np.arange(CHUNK)[:, None]
    kj = np.arange(BAND)[None, :]
    rel_idx = np.clip(qi + pad - kj, -MAX_REL, MAX_REL) + MAX_REL
    bias = rel_bias[:, rel_idx].astype(jnp.float32)
    scale = d ** -0.5
    band_pos = jnp.arange(BAND)

    def chunk_fn(c):
        start = c * CHUNK
        qc = lax.dynamic_slice_in_dim(q, start, CHUNK, axis=1)
        kb = lax.dynamic_slice_in_dim(kp, start, BAND, axis=1)
        vb = lax.dynamic_slice_in_dim(vp, start, BAND, axis=1)
        logits = jnp.einsum('bqhd,bkhd->bhqk', qc, kb).astype(jnp.float32) * scale + bias
        valid = (start - pad + band_pos) >= 0
        logits = jnp.where(valid, logits, -jnp.inf)
        p = jax.nn.softmax(logits, axis=-1)
        return jnp.einsum('bhqk,bkhd->bqhd', p.astype(vb.dtype), vb)

    out = lax.map(chunk_fn, jnp.arange(s // CHUNK))
    return jnp.transpose(out, (1, 0, 2, 3, 4)).reshape(b, s, h, d)


def stick_breaking_attention(q, k, v):
    b, s, h, d = q.shape
    scale = d ** -0.5
    key_pos = jnp.arange(s)

    def block_fn(blk):
        start = blk * Q_BLOCK
        qb = lax.dynamic_slice_in_dim(q, start, Q_BLOCK, axis=1)
        z = jnp.einsum('bqhd,bkhd->bhqk', qb, k).astype(jnp.float32) * scale
        q_pos = start + jnp.arange(Q_BLOCK)
        mask = key_pos[None, :] < q_pos[:, None]
        log_keep = jnp.where(mask, jax.nn.log_sigmoid(-z), 0.0)
        later = lax.cumsum(log_keep, axis=3, reverse=True) - log_keep
        w = jnp.where(mask, jnp.exp(jax.nn.log_sigmoid(z) + later), 0.0)
        return jnp.einsum('bhqk,bkhd->bqhd', w.astype(v.dtype), v)

    return sweep_query_blocks(block_fn, s)


def mla_attention(q_nope, q_rope, k_nope, k_rope, v):
    s = q_nope.shape[1]
    scale = (QK_NOPE + QK_ROPE) ** -0.5
    key_chunk = jnp.arange(s) // CHUNK

    def block_fn(blk):
        start = blk * Q_BLOCK
        qn = lax.dynamic_slice_in_dim(q_nope, start, Q_BLOCK, axis=1)
        qr = lax.dynamic_slice_in_dim(q_rope, start, Q_BLOCK, axis=1)
        logits = (jnp.einsum('bqhd,bkhd->bhqk', qn, k_nope)
                  + jnp.einsum('bqhr,bkr->bhqk', qr, k_rope)).astype(jnp.float32) * scale
        q_chunk = (start + jnp.arange(Q_BLOCK)) // CHUNK
        mask = key_chunk[None, :] <= q_chunk[:, None]
        p = jax.nn.softmax(jnp.where(mask, logits, -jnp.inf), axis=-1)
        return jnp.einsum('bhqk,bkhd->bqhd', p.astype(v.dtype), v)

    return sweep_query_blocks(block_fn, s)


def forgetting_attention(q, k, v, log_f):
    b, s, h, d = q.shape
    scale = d ** -0.5
    cum = jnp.transpose(jnp.cumsum(log_f, axis=1), (0, 2, 1))
    key_pos = jnp.arange(s)

    def block_fn(blk):
        start = blk * Q_BLOCK
        qb = lax.dynamic_slice_in_dim(q, start, Q_BLOCK, axis=1)
        cum_q = lax.dynamic_slice_in_dim(cum, start, Q_BLOCK, axis=2)
        logits = (jnp.einsum('bqhd,bkhd->bhqk', qb, k).astype(jnp.float32) * scale
                  + cum_q[..., None] - cum[:, :, None, :])
        q_pos = start + jnp.arange(Q_BLOCK)
        mask = key_pos[None, :] <= q_pos[:, None]
        p = jax.nn.softmax(jnp.where(mask, logits, -jnp.inf), axis=-1)
        return jnp.einsum('bhqk,bkhd->bqhd', p.astype(v.dtype), v)

    return sweep_query_blocks(block_fn, s)


def chunk_stick_layer(x, norm_g, w_in, rel_bias, w_o):
    b, s, _ = x.shape
    h = rmsnorm(x, norm_g)
    proj = h @ w_in
    qa, ka, va, qb, kb, vb = jnp.split(proj, 6, axis=-1)
    shp = (b, s, N_HEADS_A, HEAD_DIM)
    oa = chunked_relpos_attention(qa.reshape(shp), ka.reshape(shp), va.reshape(shp), rel_bias)
    shp_b = (b, s, N_HEADS_B, HEAD_DIM)
    ob = stick_breaking_attention(qb.reshape(shp_b), kb.reshape(shp_b), vb.reshape(shp_b))
    o = jnp.concatenate([oa, ob], axis=2).reshape(b, s, AB_OUT)
    return x + o @ w_o


def mla_fox_layer(x, norm_g, w_in, q_norm, w_uq, kv_norm, w_ukv, b_f, w_o):
    b, s, _ = x.shape
    h = rmsnorm(x, norm_g)
    proj = h @ w_in
    c_q, c_kv, k_rope, q_d, k_d, v_d, f_logit = jnp.split(proj, list(np.cumsum(CD_SPLITS)[:-1]), axis=-1)
    positions = jnp.arange(s)
    q_c = (rmsnorm(c_q, q_norm) @ w_uq).reshape(b, s, N_HEADS_C, QK_NOPE + QK_ROPE)
    kv_c = (rmsnorm(c_kv, kv_norm) @ w_ukv).reshape(b, s, N_HEADS_C, QK_NOPE + V_DIM_C)
    q_nope, q_rope = q_c[..., :QK_NOPE], rope(q_c[..., QK_NOPE:], positions)
    k_nope, v_c = kv_c[..., :QK_NOPE], kv_c[..., QK_NOPE:]
    oc = mla_attention(q_nope, q_rope, k_nope, rope(k_rope, positions), v_c)
    log_f = jax.nn.log_sigmoid((f_logit + b_f).astype(jnp.float32))
    shp = (b, s, N_HEADS_D, HEAD_DIM)
    od = forgetting_attention(q_d.reshape(shp), k_d.reshape(shp), v_d.reshape(shp), log_f)
    o = jnp.concatenate([oc, od], axis=2).reshape(b, s, CD_OUT)
    return x + o @ w_o


def conv_gated_mlp(x, norm_g, w_gate, w_up, conv_w, conv_b, w_down):
    h = rmsnorm(x, norm_g)
    g = h @ w_gate
    g = lax.conv_general_dilated(g, conv_w[:, None, :], window_strides=(1,),
                                 padding=[(CONV_WIDTH - 1, 0)],
                                 dimension_numbers=('NWC', 'WIO', 'NWC'),
                                 feature_group_count=g.shape[-1]) + conv_b
    y = jax.nn.silu(g) * (h @ w_up)
    return x + y @ w_down


def setup_inputs(seed: int = 0) -> dict:
    key = jax.random.key(seed)
    ks = iter(jax.random.split(key, 32))

    def nrm(shape, scale):
        return jax.random.normal(next(ks), shape, jnp.float32) * scale

    def gain(shape):
        return 1.0 + nrm(shape, 0.05)

    return {
        'x': nrm((BATCH, SEQ, D_MODEL), 1.0),
        'ab_norm': gain((N_EVEN, D_MODEL)),
        'ab_w_in': nrm((N_EVEN, D_MODEL, AB_COLS), D_MODEL ** -0.5),
        'ab_rel_bias': nrm((N_EVEN, N_HEADS_A, 2 * MAX_REL + 1), 0.5),
        'ab_w_o': nrm((N_EVEN, AB_OUT, D_MODEL), AB_OUT ** -0.5),
        'cd_norm': gain((N_ODD, D_MODEL)),
        'cd_w_in': nrm((N_ODD, D_MODEL, CD_COLS), D_MODEL ** -0.5),
        'cd_q_norm': gain((N_ODD, Q_RANK)),
        'cd_w_uq': nrm((N_ODD, Q_RANK, N_HEADS_C * (QK_NOPE + QK_ROPE)), Q_RANK ** -0.5),
        'cd_kv_norm': gain((N_ODD, KV_RANK)),
        'cd_w_ukv': nrm((N_ODD, KV_RANK, N_HEADS_C * (QK_NOPE + V_DIM_C)), KV_RANK ** -0.5),
        'cd_b_f': 3.0 + nrm((N_ODD, N_HEADS_D), 0.5),
        'cd_w_o': nrm((N_ODD, CD_OUT, D_MODEL), CD_OUT ** -0.5),
        'ffn_norm': gain((DEPTH, D_MODEL)),
        'ffn_w_gate': nrm((DEPTH, D_MODEL, D_FF), D_MODEL ** -0.5),
        'ffn_w_up': nrm((DEPTH, D_MODEL, D_FF), D_MODEL ** -0.5),
        'ffn_conv_w': nrm((DEPTH, CONV_WIDTH, D_FF), CONV_WIDTH ** -0.5),
        'ffn_conv_b': nrm((DEPTH, D_FF), 0.02),
        'ffn_w_down': nrm((DEPTH, D_FF, D_MODEL), D_FF ** -0.5),
        'final_norm': gain((D_MODEL,)),
    }


def reference(x, ab_norm, ab_w_in, ab_rel_bias, ab_w_o,
              cd_norm, cd_w_in, cd_q_norm, cd_w_uq, cd_kv_norm, cd_w_ukv, cd_b_f, cd_w_o,
              ffn_norm, ffn_w_gate, ffn_w_up, ffn_conv_w, ffn_conv_b, ffn_w_down,
              final_norm):
    for layer in range(DEPTH):
        i = layer // 2
        if layer % 2 == 0:
            x = chunk_stick_layer(x, ab_norm[i], ab_w_in[i], ab_rel_bias[i], ab_w_o[i])
        else:
            x = mla_fox_layer(x, cd_norm[i], cd_w_in[i], cd_q_norm[i], cd_w_uq[i], cd_kv_norm[i],
                              cd_w_ukv[i], cd_b_f[i], cd_w_o[i])
        x = conv_gated_mlp(x, ffn_norm[layer], ffn_w_gate[layer], ffn_w_up[layer],
                           ffn_conv_w[layer], ffn_conv_b[layer], ffn_w_down[layer])
    return rmsnorm(x, final_norm)
```

```python
import functools
import math

import numpy as np
import jax
import jax.numpy as jnp
from jax import lax
from jax.experimental import pallas as pl
from jax.experimental.pallas import tpu as pltpu

F32 = jnp.float32
BF16 = jnp.bfloat16

D_MODEL = 1024
HEAD_DIM = 64
CHUNK = 64
LEFT_CHUNKS = 8
BAND = (LEFT_CHUNKS + 1) * CHUNK
MAX_REL = 128
N_HEADS = 8
QK_NOPE = 64
QK_ROPE = 32
Q_RANK = 384
KV_RANK = 256
ROPE_THETA = 10000.0
D_FF = 2816
RMS_EPS = 1e-6

LANES = 128
LOG2E = math.log2(math.e)
NEG_BIG = -1e30
VMEM_LIMIT = 52 * 1024 * 1024

TM_PROJ = 256
TM_FFN = 512
TF_FFN = 256
TQ_A = 128
TQ = 256
N_WIN_A = LEFT_CHUNKS * CHUNK // TQ_A + 1

NT_DIMS = (((1,), (1,)), ((), ()))


def _rms(x, g):
    ms = jnp.mean(x * x, axis=-1, keepdims=True)
    return x * lax.rsqrt(ms + RMS_EPS) * g


def _stack_heads(q):
    lane = lax.broadcasted_iota(jnp.int32, q.shape, 1)
    zero = jnp.zeros_like(q)
    return jnp.concatenate([jnp.where(lane < HEAD_DIM, q, zero), jnp.where(lane >= HEAD_DIM, q, zero)], axis=0)


def _unstack_heads(acc, t):
    lane = lax.broadcasted_iota(jnp.int32, (t, LANES), 1)
    return jnp.where(lane < HEAD_DIM, acc[:t], acc[t:])


def _norm_proj_kernel(x_ref, g_ref, w_ref, cs_ref, o_ref):
    h = _rms(x_ref[...], g_ref[...]).astype(BF16)
    p = jnp.dot(h, w_ref[...], preferred_element_type=F32)
    o_ref[...] = (p * cs_ref[...]).astype(o_ref.dtype)


def _norm_proj(x2d, g, w, colscale):
    n, d = x2d.shape
    nc = w.shape[1]
    return pl.pallas_call(
        _norm_proj_kernel,
        grid=(n // TM_PROJ,),
        in_specs=[
            pl.BlockSpec((TM_PROJ, d), lambda i: (i, 0)),
            pl.BlockSpec((1, d), lambda i: (0, 0)),
            pl.BlockSpec((d, nc), lambda i: (0, 0)),
            pl.BlockSpec((1, nc), lambda i: (0, 0)),
        ],
        out_specs=pl.BlockSpec((TM_PROJ, nc), lambda i: (i, 0)),
        out_shape=jax.ShapeDtypeStruct((n, nc), BF16),
        compiler_params=pltpu.CompilerParams(
            dimension_semantics=("arbitrary",), vmem_limit_bytes=VMEM_LIMIT),
        name="ab_norm_proj",
    )(x2d, g, w, colscale)


_C_Q0, _C_KV0, _C_KR0, _C_D0, _C_F0, _C_END = 0, 384, 640, 768, 2304, 2432


def _cd_proj_kernel(x_ref, g_ref, w1_ref, qn_ref, wuq_ref, kvn_ref, wk_ref, wv_ref, bf_ref, cos_ref, sin_ref,
                    qc_ref, kc_ref, vc_ref, qkvd_ref, f_ref, carry_ref):
    t = pl.program_id(1)
    tm = x_ref.shape[1]
    h = _rms(x_ref[0], g_ref[...]).astype(BF16)
    p = jnp.dot(h, w1_ref[...], preferred_element_type=F32)

    cq = _rms(p[:, _C_Q0:_C_KV0], qn_ref[...]).astype(BF16)
    ckv = _rms(p[:, _C_KV0:_C_KR0], kvn_ref[...]).astype(BF16)

    cosb = cos_ref[...]
    sinb = sin_ref[...]
    lane = lax.broadcasted_iota(jnp.int32, (tm, LANES), 1)

    def rope(xb):
        partner = jnp.where(lane < QK_NOPE + QK_ROPE // 2, pltpu.roll(xb, LANES - QK_ROPE // 2, 1),
                            pltpu.roll(xb, QK_ROPE // 2, 1))
        return xb * cosb + partner * sinb

    qc = jnp.dot(cq, wuq_ref[...], preferred_element_type=F32) * ((QK_NOPE + QK_ROPE) ** -0.5 * LOG2E)
    kc = jnp.dot(ckv, wk_ref[...], preferred_element_type=F32)
    kr = rope(p[:, _C_KR0:_C_D0])
    for hh in range(N_HEADS):
        sl = slice(hh * LANES, (hh + 1) * LANES)
        qc_ref[0, :, sl] = rope(qc[:, sl]).astype(BF16)
        kc_ref[0, :, sl] = (kc[:, sl] + kr).astype(BF16)
    vc_ref[0] = jnp.dot(ckv, wv_ref[...], preferred_element_type=F32).astype(BF16)

    nqd = N_HEADS * HEAD_DIM
    qkvd_ref[0, :, 0:nqd] = (p[:, _C_D0:_C_D0 + nqd] * (HEAD_DIM ** -0.5 * LOG2E)).astype(BF16)
    qkvd_ref[0, :, nqd:3 * nqd] = p[:, _C_D0 + nqd:_C_F0].astype(BF16)

    fl = p[:, _C_F0:_C_END] + bf_ref[...]
    y = jnp.minimum(fl, 0.0) - jnp.log(1.0 + jnp.exp(-jnp.abs(fl)))
    row = lax.broadcasted_iota(jnp.int32, (tm, LANES), 0)
    sh = 1
    while sh < tm:
        y = y + jnp.where(row >= sh, pltpu.roll(y, sh, 0), 0.0)
        sh *= 2

    @pl.when(t == 0)
    def _():
        carry_ref[...] = jnp.zeros_like(carry_ref)

    y = y + carry_ref[0:1, :]
    carry_ref[...] = jnp.broadcast_to(y[tm - 1:tm, :], carry_ref.shape)
    f_ref[0] = y * LOG2E


def _cd_proj(x, g, w1, qn, wuq, kvn, wk, wv, bf, cos_t, sin_t):
    b, s, d = x.shape
    tm = TM_PROJ
    const = lambda shape: pl.BlockSpec(shape, lambda bi, ti: (0,) * len(shape))
    tok = lambda nc: pl.BlockSpec((1, tm, nc), lambda bi, ti: (bi, ti, 0))
    return pl.pallas_call(
        _cd_proj_kernel,
        grid=(b, s // tm),
        in_specs=[
            tok(d), const((1, d)), const(w1.shape), const(qn.shape), const(wuq.shape), const(kvn.shape),
            const(wk.shape), const(wv.shape), const(bf.shape),
            pl.BlockSpec((tm, LANES), lambda bi, ti: (ti, 0)),
            pl.BlockSpec((tm, LANES), lambda bi, ti: (ti, 0)),
        ],
        out_specs=[tok(N_HEADS * LANES), tok(N_HEADS * LANES), tok(N_HEADS * HEAD_DIM),
                   tok(3 * N_HEADS * HEAD_DIM), tok(LANES)],
        out_shape=[
            jax.ShapeDtypeStruct((b, s, N_HEADS * LANES), BF16),
            jax.ShapeDtypeStruct((b, s, N_HEADS * LANES), BF16),
            jax.ShapeDtypeStruct((b, s, N_HEADS * HEAD_DIM), BF16),
            jax.ShapeDtypeStruct((b, s, 3 * N_HEADS * HEAD_DIM), BF16),
            jax.ShapeDtypeStruct((b, s, LANES), F32),
        ],
        scratch_shapes=[pltpu.VMEM((8, LANES), F32)],
        compiler_params=pltpu.CompilerParams(
            dimension_semantics=("arbitrary", "arbitrary"), vmem_limit_bytes=VMEM_LIMIT),
        name="cd_norm_proj",
    )(x, g, w1, qn, wuq, kvn, wk, wv, bf, cos_t, sin_t)


def _chunk_attn_kernel(q_ref, k_ref, v_ref, bias_ref, o_ref):
    qi = pl.program_id(2)
    tq = q_ref.shape[1]
    q2 = _stack_heads(q_ref[0])

    def body(j, carry):
        m, l, acc = carry
        kstart = pl.multiple_of((qi - (N_WIN_A - 1) + j) * tq, tq)
        k = k_ref[0, pl.ds(kstart, tq), :]
        v = v_ref[0, pl.ds(kstart, tq), :]
        s = lax.dot_general(q2, k, NT_DIMS, preferred_element_type=F32)
        s = s + jnp.concatenate([bias_ref[0, j], bias_ref[1, j]], axis=0)
        m_new = jnp.maximum(m, jnp.max(s, axis=-1, keepdims=True))
        alpha = jnp.exp2(m - m_new)
        p = jnp.exp2(s - m_new)
        l = alpha * l + jnp.sum(p, axis=-1, keepdims=True)
        acc = alpha * acc + jnp.dot(p.astype(BF16), v, preferred_element_type=F32)
        return m_new, l, acc

    init = (jnp.full((2 * tq, 1), NEG_BIG, F32), jnp.zeros((2 * tq, 1), F32), jnp.zeros((2 * tq, LANES), F32))
    _, l, acc = lax.fori_loop(jnp.maximum(N_WIN_A - 1 - qi, 0), N_WIN_A, body, init)
    o_ref[0] = _unstack_heads(acc / l, tq).astype(o_ref.dtype)


def _chunk_attn(proj, bias):
    b, s, _ = proj.shape
    npair = N_HEADS // 2
    return pl.pallas_call(
        _chunk_attn_kernel,
        grid=(b, npair, s // TQ_A),
        in_specs=[
            pl.BlockSpec((1, TQ_A, LANES), lambda bi, hp, qi: (bi, qi, hp)),
            pl.BlockSpec((1, s, LANES), lambda bi, hp, qi: (bi, 0, npair + hp)),
            pl.BlockSpec((1, s, LANES), lambda bi, hp, qi: (bi, 0, 2 * npair + hp)),
            pl.BlockSpec((2, N_WIN_A, TQ_A, TQ_A), lambda bi, hp, qi: (hp, 0, 0, 0)),
        ],
        out_specs=pl.BlockSpec((1, TQ_A, LANES), lambda bi, hp, qi: (bi, qi, hp)),
        out_shape=jax.ShapeDtypeStruct((b, s, N_HEADS * HEAD_DIM), BF16),
        compiler_params=pltpu.CompilerParams(
            dimension_semantics=("arbitrary", "arbitrary", "arbitrary"), vmem_limit_bytes=VMEM_LIMIT),
        name="chunk_attn",
    )(proj, proj, proj, bias)


def _chunk_bias_tiles(rel_bias):
    r = np.arange(TQ_A)[:, None]
    kj = np.arange(N_WIN_A * TQ_A)[None, :]
    rel = np.clip(r + LEFT_CHUNKS * CHUNK - kj, -MAX_REL, MAX_REL) + MAX_REL
    off = kj - CHUNK * (r // CHUNK)
    in_band = (off >= 0) & (off < BAND)
    bias = jnp.where(in_band[None], rel_bias[:, rel].astype(F32) * LOG2E, NEG_BIG)
    h = rel_bias.shape[0]
    return bias.reshape(h, TQ_A, N_WIN_A, TQ_A).transpose(0, 2, 1, 3)


def _softmax_step(s, v, m, l, acc):
    m_new = jnp.maximum(m, jnp.max(s, axis=-1, keepdims=True))
    alpha = jnp.exp2(m - m_new)
    p = jnp.exp2(s - m_new)
    l = alpha * l + jnp.sum(p, axis=-1, keepdims=True)
    acc = alpha * acc + jnp.dot(p.astype(BF16), v, preferred_element_type=F32)
    return m_new, l, acc


def _softmax_init(rows):
    return (jnp.full((rows, 1), NEG_BIG, F32), jnp.zeros((rows, 1), F32), jnp.zeros((rows, LANES), F32))


def _stick_kernel(q_ref, k_ref, v_ref, o_ref):
    qi = pl.program_id(2)
    tq = q_ref.shape[1]
    q2 = _stack_heads(q_ref[0])
    jj = lax.broadcasted_iota(jnp.int32, (tq, tq), 0)
    ss = lax.broadcasted_iota(jnp.int32, (tq, tq), 1)
    suffix = jnp.where(jj >= ss, 1.0, 0.0).astype(BF16)
    row = lax.broadcasted_iota(jnp.int32, (2 * tq, tq), 0)
    col = lax.broadcasted_iota(jnp.int32, (2 * tq, tq), 1)
    strictly_before = col < jnp.where(row >= tq, row - tq, row)

    def tile(kstart, c, acc, mask):
        k = k_ref[0, pl.ds(kstart, tq), :]
        v = v_ref[0, pl.ds(kstart, tq), :]
        z = lax.dot_general(q2, k, NT_DIMS, preferred_element_type=F32)
        nlk = jnp.maximum(z, 0.0) + jnp.log(1.0 + jnp.exp2(-jnp.abs(z))) * LOG2E
        if mask is not None:
            nlk = jnp.where(mask, nlk, 0.0)
        r = jnp.dot(nlk.astype(BF16), suffix, preferred_element_type=F32)
        w = jnp.exp2(z - r - c)
        if mask is not None:
            w = jnp.where(mask, w, 0.0)
        acc = acc + jnp.dot(w.astype(BF16), v, preferred_element_type=F32)
        return c + jnp.sum(nlk, axis=-1, keepdims=True), acc

    c0 = jnp.zeros((2 * tq, 1), F32)
    acc0 = jnp.zeros((2 * tq, LANES), F32)
    c, acc = tile(pl.multiple_of(qi * tq, tq), c0, acc0, strictly_before)

    def body(it, carry):
        return tile(pl.multiple_of((qi - 1 - it) * tq, tq), carry[0], carry[1], None)

    _, acc = lax.fori_loop(0, qi, body, (c, acc))
    o_ref[0] = _unstack_heads(acc, tq).astype(o_ref.dtype)


def _stick_attn(proj):
    b, s, _ = proj.shape
    npair = N_HEADS // 2
    base = 3 * npair
    return pl.pallas_call(
        _stick_kernel,
        grid=(b, npair, s // TQ),
        in_specs=[
            pl.BlockSpec((1, TQ, LANES), lambda bi, hp, qi: (bi, qi, base + hp)),
            pl.BlockSpec((1, s, LANES), lambda bi, hp, qi: (bi, 0, base + npair + hp)),
            pl.BlockSpec((1, s, LANES), lambda bi, hp, qi: (bi, 0, base + 2 * npair + hp)),
        ],
        out_specs=pl.BlockSpec((1, TQ, LANES), lambda bi, hp, qi: (bi, qi, hp)),
        out_shape=jax.ShapeDtypeStruct((b, s, N_HEADS * HEAD_DIM), BF16),
        compiler_params=pltpu.CompilerParams(
            dimension_semantics=("arbitrary", "arbitrary", "arbitrary"), vmem_limit_bytes=VMEM_LIMIT),
        name="stick_attn",
    )(proj, proj, proj)


def _mla_kernel(q_ref, k_ref, v_ref, o_ref):
    qi = pl.program_id(2)
    tq = q_ref.shape[1]
    q = q_ref[0]
    qa, qb = q[:, :LANES], q[:, LANES:]
    row = lax.broadcasted_iota(jnp.int32, (2 * tq, tq), 0)
    col = lax.broadcasted_iota(jnp.int32, (2 * tq, tq), 1)
    visible = (col // CHUNK) <= (jnp.where(row >= tq, row - tq, row) // CHUNK)

    def tile(kstart, carry, mask):
        k = k_ref[0, pl.ds(kstart, tq), :]
        v = v_ref[0, pl.ds(kstart, tq), :]
        s = jnp.concatenate([lax.dot_general(qa, k[:, :LANES], NT_DIMS, preferred_element_type=F32),
                             lax.dot_general(qb, k[:, LANES:], NT_DIMS, preferred_element_type=F32)], axis=0)
        if mask is not None:
            s = jnp.where(mask, s, NEG_BIG)
        return _softmax_step(s, v, *carry)

    def body(j, carry):
        return tile(pl.multiple_of(j * tq, tq), carry, None)

    carry = lax.fori_loop(0, qi, body, _softmax_init(2 * tq))
    _, l, acc = tile(pl.multiple_of(qi * tq, tq), carry, visible)
    o_ref[0] = _unstack_heads(acc / l, tq).astype(o_ref.dtype)


def _mla_attn(qc, kc, vc):
    b, s, _ = qc.shape
    npair = N_HEADS // 2
    return pl.pallas_call(
        _mla_kernel,
        grid=(b, npair, s // TQ),
        in_specs=[
            pl.BlockSpec((1, TQ, 2 * LANES), lambda bi, hp, qi: (bi, qi, hp)),
            pl.BlockSpec((1, s, 2 * LANES), lambda bi, hp, qi: (bi, 0, hp)),
            pl.BlockSpec((1, s, LANES), lambda bi, hp, qi: (bi, 0, hp)),
        ],
        out_specs=pl.BlockSpec((1, TQ, LANES), lambda bi, hp, qi: (bi, qi, hp)),
        out_shape=jax.ShapeDtypeStruct((b, s, N_HEADS * HEAD_DIM), BF16),
        compiler_params=pltpu.CompilerParams(
            dimension_semantics=("arbitrary", "arbitrary", "arbitrary"), vmem_limit_bytes=VMEM_LIMIT),
        name="mla_attn",
    )(qc, kc, vc)


def _fox_kernel(q_ref, k_ref, v_ref, fq_ref, fk_ref, o_ref):
    qi = pl.program_id(2)
    tq = q_ref.shape[1]
    q2 = _stack_heads(q_ref[0])
    fq = fq_ref[0, 0]
    fq2 = jnp.concatenate([fq[:, 0:1], fq[:, 1:2]], axis=0)
    row = lax.broadcasted_iota(jnp.int32, (2 * tq, tq), 0)
    col = lax.broadcasted_iota(jnp.int32, (2 * tq, tq), 1)
    causal = col <= jnp.where(row >= tq, row - tq, row)

    def tile(kstart, carry, mask):
        k = k_ref[0, pl.ds(kstart, tq), :]
        v = v_ref[0, pl.ds(kstart, tq), :]
        fk = fk_ref[0, 0, :, pl.ds(kstart, tq)]
        fk2 = jnp.concatenate([jnp.broadcast_to(fk[0:1], (tq, tq)), jnp.broadcast_to(fk[1:2], (tq, tq))], axis=0)
        s = lax.dot_general(q2, k, NT_DIMS, preferred_element_type=F32) + (fq2 - fk2)
        if mask is not None:
            s = jnp.where(mask, s, NEG_BIG)
        return _softmax_step(s, v, *carry)

    def body(j, carry):
        return tile(pl.multiple_of(j * tq, tq), carry, None)

    carry = lax.fori_loop(0, qi, body, _softmax_init(2 * tq))
    _, l, acc = tile(pl.multiple_of(qi * tq, tq), carry, causal)
    o_ref[0] = _unstack_heads(acc / l, tq).astype(o_ref.dtype)


def _fox_attn(qkvd, fq, fk):
    b, s, _ = qkvd.shape
    npair = N_HEADS // 2
    return pl.pallas_call(
        _fox_kernel,
        grid=(b, npair, s // TQ),
        in_specs=[
            pl.BlockSpec((1, TQ, LANES), lambda bi, hp, qi: (bi, qi, hp)),
            pl.BlockSpec((1, s, LANES), lambda bi, hp, qi: (bi, 0, npair + hp)),
            pl.BlockSpec((1, s, LANES), lambda bi, hp, qi: (bi, 0, 2 * npair + hp)),
            pl.BlockSpec((1, 1, TQ, 2), lambda bi, hp, qi: (bi, hp, qi, 0)),
            pl.BlockSpec((1, 1, 2, s), lambda bi, hp, qi: (bi, hp, 0, 0)),
        ],
        out_specs=pl.BlockSpec((1, TQ, LANES), lambda bi, hp, qi: (bi, qi, hp)),
        out_shape=jax.ShapeDtypeStruct((b, s, N_HEADS * HEAD_DIM), BF16),
        compiler_params=pltpu.CompilerParams(
            dimension_semantics=("arbitrary", "arbitrary", "arbitrary"), vmem_limit_bytes=VMEM_LIMIT),
        name="fox_attn",
    )(qkvd, qkvd, qkvd, fq, fk)


def _out_ffn_kernel(*refs, tiles_per_seq, final_norm):
    if final_norm:
        (x_ref, o1_ref, o2_ref, wo_ref, g_ref, wg_ref, wu_ref, cw_ref, cb_ref, wd_ref, fg_ref,
         out_ref, x1_ref, h_ref, acc_ref, gbuf_ref, tail_ref) = refs
    else:
        (x_ref, o1_ref, o2_ref, wo_ref, g_ref, wg_ref, wu_ref, cw_ref, cb_ref, wd_ref,
         out_ref, x1_ref, h_ref, acc_ref, gbuf_ref, tail_ref) = refs
    i = pl.program_id(0)
    f = pl.program_id(1)
    nf = pl.num_programs(1)
    tm = x_ref.shape[0]
    half = o1_ref.shape[1]

    @pl.when(f == 0)
    def _():
        x1 = (x_ref[...]
              + jnp.dot(o1_ref[...], wo_ref[0:half, :], preferred_element_type=F32)
              + jnp.dot(o2_ref[...], wo_ref[half:2 * half, :], preferred_element_type=F32))
        x1_ref[...] = x1
        h_ref[...] = _rms(x1, g_ref[...]).astype(BF16)
        acc_ref[...] = jnp.zeros_like(acc_ref)

    h = h_ref[...]
    g = jnp.dot(h, wg_ref[...], preferred_element_type=F32)
    u = jnp.dot(h, wu_ref[...], preferred_element_type=F32)

    prev = jnp.where(i % tiles_per_seq == 0, 0.0, tail_ref[f])
    gbuf_ref[0:8, :] = prev
    gbuf_ref[8:8 + tm, :] = g
    tail_ref[f] = g[tm - 8:tm, :]
    gm1 = gbuf_ref[7:7 + tm, :]
    gm2 = gbuf_ref[6:6 + tm, :]
    cw = cw_ref[...]
    gc = cw[0:1, :] * gm2 + cw[1:2, :] * gm1 + cw[2:3, :] * g + cb_ref[...]
    y = (gc / (1.0 + jnp.exp(-gc)) * u).astype(BF16)
    acc_ref[...] += jnp.dot(y, wd_ref[...], preferred_element_type=F32)

    @pl.when(f == nf - 1)
    def _():
        res = x1_ref[...] + acc_ref[...]
        if final_norm:
            res = _rms(res, fg_ref[...])
        out_ref[...] = res


def _out_ffn(x2d, o1, o2, wo, g, wg, wu, cw, cb, wd, final_g, seq_len):
    n, d = x2d.shape
    half = o1.shape[1]
    dff = wg.shape[1]
    tm, tf = TM_FFN, TF_FFN
    nf = dff // tf
    final_norm = final_g is not None
    in_specs = [
        pl.BlockSpec((tm, d), lambda i, f: (i, 0)),
        pl.BlockSpec((tm, half), lambda i, f: (i, 0)),
        pl.BlockSpec((tm, half), lambda i, f: (i, 0)),
        pl.BlockSpec((d, d), lambda i, f: (0, 0)),
        pl.BlockSpec((1, d), lambda i, f: (0, 0)),
        pl.BlockSpec((d, tf), lambda i, f: (0, f)),
        pl.BlockSpec((d, tf), lambda i, f: (0, f)),
        pl.BlockSpec((3, tf), lambda i, f: (0, f)),
        pl.BlockSpec((1, tf), lambda i, f: (0, f)),
        pl.BlockSpec((tf, d), lambda i, f: (f, 0)),
    ]
    args = [x2d, o1, o2, wo, g, wg, wu, cw, cb, wd]
    if final_norm:
        in_specs.append(pl.BlockSpec((1, d), lambda i, f: (0, 0)))
        args.append(final_g)
    return pl.pallas_call(
        functools.partial(_out_ffn_kernel, tiles_per_seq=seq_len // tm, final_norm=final_norm),
        grid=(n // tm, nf),
        in_specs=in_specs,
        out_specs=pl.BlockSpec((tm, d), lambda i, f: (i, 0)),
        out_shape=jax.ShapeDtypeStruct((n, d), F32),
        scratch_shapes=[
            pltpu.VMEM((tm, d), F32),
            pltpu.VMEM((tm, d), BF16),
            pltpu.VMEM((tm, d), F32),
            pltpu.VMEM((tm + 8, tf), F32),
            pltpu.VMEM((nf, 8, tf), F32),
        ],
        compiler_params=pltpu.CompilerParams(
            dimension_semantics=("arbitrary", "arbitrary"), vmem_limit_bytes=VMEM_LIMIT),
        name="out_ffn_final" if final_norm else "out_ffn",
    )(*args)


def _rope_tables(seq_len):
    half = QK_ROPE // 2
    inv = ROPE_THETA ** (-jnp.arange(half, dtype=F32) / half)
    ang = jnp.arange(seq_len, dtype=F32)[:, None] * inv[None, :]
    cos, sin = jnp.cos(ang), jnp.sin(ang)
    ones = jnp.ones((seq_len, QK_NOPE), F32)
    zeros = jnp.zeros((seq_len, QK_NOPE), F32)
    pad1 = jnp.ones((seq_len, LANES - QK_NOPE - QK_ROPE), F32)
    pad0 = jnp.zeros((seq_len, LANES - QK_NOPE - QK_ROPE), F32)
    return (jnp.concatenate([ones, cos, cos, pad1], axis=1),
            jnp.concatenate([zeros, -sin, sin, pad0], axis=1))


def _cd_weights(w_in, w_uq, w_ukv, b_f):
    d = w_in.shape[0]
    c_q, c_kv, k_rope, qkv_d, f_logit = jnp.split(
        w_in, [Q_RANK, Q_RANK + KV_RANK, Q_RANK + KV_RANK + QK_ROPE, Q_RANK + KV_RANK + QK_ROPE + 3 * 512], axis=1)
    kr_blk = jnp.zeros((d, LANES), w_in.dtype).at[:, QK_NOPE:QK_NOPE + QK_ROPE].set(k_rope)
    f_blk = jnp.zeros((d, LANES), w_in.dtype).at[:, :N_HEADS].set(f_logit)
    w1 = jnp.concatenate([c_q, c_kv, kr_blk, qkv_d, f_blk], axis=1).astype(BF16)
    wuq = jnp.zeros((Q_RANK, N_HEADS, LANES), w_uq.dtype).at[:, :, :QK_NOPE + QK_ROPE].set(
        w_uq.reshape(Q_RANK, N_HEADS, QK_NOPE + QK_ROPE)).reshape(Q_RANK, N_HEADS * LANES).astype(BF16)
    ukv = w_ukv.reshape(KV_RANK, N_HEADS, QK_NOPE + HEAD_DIM)
    wk = jnp.zeros((KV_RANK, N_HEADS, LANES), w_ukv.dtype).at[:, :, :QK_NOPE].set(
        ukv[:, :, :QK_NOPE]).reshape(KV_RANK, N_HEADS * LANES).astype(BF16)
    wv = ukv[:, :, QK_NOPE:].reshape(KV_RANK, N_HEADS * HEAD_DIM).astype(BF16)
    bf = jnp.zeros((1, LANES), F32).at[0, :N_HEADS].set(b_f.astype(F32))
    return w1, wuq, wk, wv, bf


def kernel(x, ab_norm, ab_w_in, ab_rel_bias, ab_w_o, cd_norm, cd_w_in, cd_q_norm, cd_w_uq, cd_kv_norm, cd_w_ukv,
           cd_b_f, cd_w_o, ffn_norm, ffn_w_gate, ffn_w_up, ffn_conv_w, ffn_conv_b, ffn_w_down, final_norm):
    b, s, d = x.shape
    n = b * s
    x2d = x.reshape(n, d)
    npair = N_HEADS // 2

    qscale = jnp.full((N_HEADS * HEAD_DIM,), HEAD_DIM ** -0.5 * LOG2E, F32)
    one = jnp.ones((2 * N_HEADS * HEAD_DIM,), F32)
    colscale = jnp.concatenate([qscale, one, qscale, one])[None, :]
    proj = _norm_proj(x2d, ab_norm[0][None, :], ab_w_in[0].astype(BF16), colscale).reshape(b, s, -1)
    oa = _chunk_attn(proj, _chunk_bias_tiles(ab_rel_bias[0]))
    ob = _stick_attn(proj)
    x2d = _out_ffn(x2d, oa.reshape(n, -1), ob.reshape(n, -1), ab_w_o[0].astype(BF16), ffn_norm[0][None, :],
                   ffn_w_gate[0].astype(BF16), ffn_w_up[0].astype(BF16), ffn_conv_w[0], ffn_conv_b[0][None, :],
                   ffn_w_down[0].astype(BF16), None, s)

    w1, wuq, wk, wv, bf = _cd_weights(cd_w_in[0], cd_w_uq[0], cd_w_ukv[0], cd_b_f[0])
    cos_t, sin_t = _rope_tables(s)
    qc, kc, vc, qkvd, fcum = _cd_proj(x2d.reshape(b, s, d), cd_norm[0][None, :], w1, cd_q_norm[0][None, :], wuq,
                                      cd_kv_norm[0][None, :], wk, wv, bf, cos_t, sin_t)
    oc = _mla_attn(qc, kc, vc)
    fh = fcum[:, :, :N_HEADS].reshape(b, s, npair, 2)
    od = _fox_attn(qkvd, jnp.transpose(fh, (0, 2, 1, 3)), jnp.transpose(fh, (0, 2, 3, 1)))
    out = _out_ffn(x2d, oc.reshape(n, -1), od.reshape(n, -1), cd_w_o[0].astype(BF16), ffn_norm[1][None, :],
                   ffn_w_gate[1].astype(BF16), ffn_w_up[1].astype(BF16), ffn_conv_w[1], ffn_conv_b[1][None, :],
                   ffn_w_down[1].astype(BF16), final_norm[None, :], s)
    return out.reshape(b, s, d)
```

```python
import functools
import math

import numpy as np
import jax
import jax.numpy as jnp
from jax import lax
from jax.experimental import pallas as pl
from jax.experimental.pallas import tpu as pltpu

F32 = jnp.float32
BF16 = jnp.bfloat16

D_MODEL = 1024
HEAD_DIM = 64
CHUNK = 64
LEFT_CHUNKS = 8
BAND = (LEFT_CHUNKS + 1) * CHUNK
MAX_REL = 128
N_HEADS = 8
N_PAIRS = N_HEADS // 2
QK_NOPE = 64
QK_ROPE = 32
Q_RANK = 384
KV_RANK = 256
ROPE_THETA = 10000.0
D_FF = 2816
RMS_EPS = 1e-6

LANES = 128
LOG2E = math.log2(math.e)
NEG_BIG = -1e30
VMEM_LIMIT = 52 * 1024 * 1024

TM_PROJ = 256
TM_FFN = 512
TF_FFN = 256
TQ_A = 128
TQ = 256
N_WIN_A = LEFT_CHUNKS * CHUNK // TQ_A + 1
HEADS_PER_STEP = 8

N_PIECES = 3
F_LANE0 = HEAD_DIM
ONE_LANE0 = HEAD_DIM + N_PIECES

NT_DIMS = (((1,), (1,)), ((), ()))


def _rms(x, g):
    ms = jnp.mean(x * x, axis=-1, keepdims=True)
    return x * lax.rsqrt(ms + RMS_EPS) * g


def _store_pairs_transposed(vt_ref, v):
    for p in range(N_PAIRS):
        vt_ref[0, p] = v[:, p * LANES:(p + 1) * LANES].T.astype(vt_ref.dtype)


def _ab_proj_kernel(x_ref, g_ref, w_ref, cs_ref, o_ref, vt_ref):
    h = _rms(x_ref[0], g_ref[...]).astype(BF16)
    p = jnp.dot(h, w_ref[...], preferred_element_type=F32) * cs_ref[...]
    o_ref[0] = p.astype(o_ref.dtype)
    nv = N_HEADS * HEAD_DIM
    _store_pairs_transposed(vt_ref, p[:, p.shape[1] - nv:])


def _ab_proj(x, g, w, colscale):
    b, s, d = x.shape
    nc = w.shape[1]
    tm = TM_PROJ
    return pl.pallas_call(
        _ab_proj_kernel,
        grid=(b, s // tm),
        in_specs=[
            pl.BlockSpec((1, tm, d), lambda bi, ti: (bi, ti, 0)),
            pl.BlockSpec((1, d), lambda bi, ti: (0, 0)),
            pl.BlockSpec((d, nc), lambda bi, ti: (0, 0)),
            pl.BlockSpec((1, nc), lambda bi, ti: (0, 0)),
        ],
        out_specs=[pl.BlockSpec((1, tm, nc), lambda bi, ti: (bi, ti, 0)),
                   pl.BlockSpec((1, N_PAIRS, LANES, tm), lambda bi, ti: (bi, 0, 0, ti))],
        out_shape=[jax.ShapeDtypeStruct((b, s, nc), BF16),
                   jax.ShapeDtypeStruct((b, N_PAIRS, LANES, s), BF16)],
        compiler_params=pltpu.CompilerParams(
            dimension_semantics=("arbitrary", "arbitrary"), vmem_limit_bytes=VMEM_LIMIT),
        name="ab_norm_proj",
    )(x, g, w, colscale)


_NPAD = N_HEADS * LANES
_C_Q0 = 0
_C_KV0 = _C_Q0 + Q_RANK
_C_KR0 = _C_KV0 + KV_RANK
_C_QD0 = _C_KR0 + LANES
_C_KD0 = _C_QD0 + _NPAD
_C_VD0 = _C_KD0 + _NPAD
_C_F0 = _C_VD0 + N_HEADS * HEAD_DIM
_C_END = _C_F0 + LANES


def _cd_proj_kernel(x_ref, g_ref, w1_ref, qn_ref, wuq_ref, kvn_ref, wk_ref, wv_ref, bf_ref, cos_ref, sin_ref,
                    selq_ref, selk_ref, oneq_ref, onek_ref,
                    qc_ref, kc_ref, vct_ref, qd_ref, kd_ref, vdt_ref, carry_ref):
    t = pl.program_id(1)
    tm = x_ref.shape[1]
    h = _rms(x_ref[0], g_ref[...]).astype(BF16)
    p = jnp.dot(h, w1_ref[...], preferred_element_type=F32)

    cq = _rms(p[:, _C_Q0:_C_KV0], qn_ref[...]).astype(BF16)
    ckv = _rms(p[:, _C_KV0:_C_KR0], kvn_ref[...]).astype(BF16)

    cosb = cos_ref[...]
    sinb = sin_ref[...]
    lane = lax.broadcasted_iota(jnp.int32, (tm, LANES), 1)

    def rope(xb):
        partner = jnp.where(lane < QK_NOPE + QK_ROPE // 2, pltpu.roll(xb, LANES - QK_ROPE // 2, 1),
                            pltpu.roll(xb, QK_ROPE // 2, 1))
        return xb * cosb + partner * sinb

    qc = jnp.dot(cq, wuq_ref[...], preferred_element_type=F32) * ((QK_NOPE + QK_ROPE) ** -0.5 * LOG2E)
    kc = jnp.dot(ckv, wk_ref[...], preferred_element_type=F32)
    kr = rope(p[:, _C_KR0:_C_QD0])
    for hh in range(N_HEADS):
        sl = slice(hh * LANES, (hh + 1) * LANES)
        qc_ref[0, :, sl] = rope(qc[:, sl]).astype(BF16)
        kc_ref[0, :, sl] = (kc[:, sl] + kr).astype(BF16)
    _store_pairs_transposed(vct_ref, jnp.dot(ckv, wv_ref[...], preferred_element_type=F32))
    _store_pairs_transposed(vdt_ref, p[:, _C_VD0:_C_F0])

    fl = p[:, _C_F0:_C_END] + bf_ref[...]
    y = jnp.minimum(fl, 0.0) - jnp.log(1.0 + jnp.exp(-jnp.abs(fl)))
    row = lax.broadcasted_iota(jnp.int32, (tm, LANES), 0)
    sh = 1
    while sh < tm:
        y = y + jnp.where(row >= sh, pltpu.roll(y, sh, 0), 0.0)
        sh *= 2

    @pl.when(t == 0)
    def _():
        carry_ref[...] = jnp.zeros_like(carry_ref)

    y = y + carry_ref[0:1, :]
    carry_ref[...] = jnp.broadcast_to(y[tm - 1:tm, :], carry_ref.shape)
    f2 = y * LOG2E

    hi = f2.astype(BF16)
    r1 = f2 - hi.astype(F32)
    mid = r1.astype(BF16)
    lo = (r1 - mid.astype(F32)).astype(BF16)
    fp = jnp.concatenate([hi, mid, lo], axis=1)
    qd = p[:, _C_QD0:_C_KD0] * (HEAD_DIM ** -0.5 * LOG2E)
    qd_ref[0] = (qd + jnp.dot(fp, selq_ref[...], preferred_element_type=F32) + oneq_ref[...]).astype(BF16)
    kd_ref[0] = (p[:, _C_KD0:_C_VD0] + jnp.dot(fp, selk_ref[...], preferred_element_type=F32)
                 + onek_ref[...]).astype(BF16)


def _cd_proj(x, g, w1, qn, wuq, kvn, wk, wv, bf, cos_t, sin_t, selq, selk, oneq, onek):
    b, s, d = x.shape
    tm = TM_PROJ
    const = lambda a: pl.BlockSpec(a.shape, lambda bi, ti: (0,) * a.ndim)
    tok = lambda nc: pl.BlockSpec((1, tm, nc), lambda bi, ti: (bi, ti, 0))
    vt_spec = pl.BlockSpec((1, N_PAIRS, LANES, tm), lambda bi, ti: (bi, 0, 0, ti))
    act = jax.ShapeDtypeStruct((b, s, _NPAD), BF16)
    vt = jax.ShapeDtypeStruct((b, N_PAIRS, LANES, s), BF16)
    return pl.pallas_call(
        _cd_proj_kernel,
        grid=(b, s // tm),
        in_specs=[
            tok(d), const(g), const(w1), const(qn), const(wuq), const(kvn), const(wk), const(wv), const(bf),
            pl.BlockSpec((tm, LANES), lambda bi, ti: (ti, 0)),
            pl.BlockSpec((tm, LANES), lambda bi, ti: (ti, 0)),
            const(selq), const(selk), const(oneq), const(onek),
        ],
        out_specs=[tok(_NPAD), tok(_NPAD), vt_spec, tok(_NPAD), tok(_NPAD), vt_spec],
        out_shape=[act, act, vt, act, act, vt],
        scratch_shapes=[pltpu.VMEM((8, LANES), F32)],
        compiler_params=pltpu.CompilerParams(
            dimension_semantics=("arbitrary", "arbitrary"), vmem_limit_bytes=VMEM_LIMIT),
        name="cd_norm_proj",
    )(x, g, w1, qn, wuq, kvn, wk, wv, bf, cos_t, sin_t, selq, selk, oneq, onek)


def _stack_heads(q):
    lane = lax.broadcasted_iota(jnp.int32, q.shape, 1)
    zero = jnp.zeros_like(q)
    return jnp.concatenate([jnp.where(lane < HEAD_DIM, q, zero), jnp.where(lane >= HEAD_DIM, q, zero)], axis=0)


def _chunk_attn_kernel(q_ref, k_ref, v_ref, bias_ref, o_ref):
    qi = pl.program_id(2)
    tq = q_ref.shape[1]
    q2 = _stack_heads(q_ref[0])

    def body(j, carry):
        m, l, acc = carry
        kstart = pl.multiple_of((qi - (N_WIN_A - 1) + j) * tq, tq)
        k = k_ref[0, pl.ds(kstart, tq), :]
        v = v_ref[0, pl.ds(kstart, tq), :]
        s = lax.dot_general(q2, k, NT_DIMS, preferred_element_type=F32)
        s = s + jnp.concatenate([bias_ref[0, j], bias_ref[1, j]], axis=0)
        m_new = jnp.maximum(m, jnp.max(s, axis=-1, keepdims=True))
        alpha = jnp.exp2(m - m_new)
        p = jnp.exp2(s - m_new)
        l = alpha * l + jnp.sum(p, axis=-1, keepdims=True)
        acc = alpha * acc + jnp.dot(p.astype(BF16), v, preferred_element_type=F32)
        return m_new, l, acc

    init = (jnp.full((2 * tq, 1), NEG_BIG, F32), jnp.zeros((2 * tq, 1), F32), jnp.zeros((2 * tq, LANES), F32))
    _, l, acc = lax.fori_loop(jnp.maximum(N_WIN_A - 1 - qi, 0), N_WIN_A, body, init)
    out = acc / l
    lane = lax.broadcasted_iota(jnp.int32, (tq, LANES), 1)
    o_ref[0] = jnp.where(lane < HEAD_DIM, out[:tq], out[tq:]).astype(o_ref.dtype)


def _chunk_attn(proj, bias):
    b, s, _ = proj.shape
    return pl.pallas_call(
        _chunk_attn_kernel,
        grid=(b, N_PAIRS, s // TQ_A),
        in_specs=[
            pl.BlockSpec((1, TQ_A, LANES), lambda bi, hp, qi: (bi, qi, hp)),
            pl.BlockSpec((1, s, LANES), lambda bi, hp, qi: (bi, 0, N_PAIRS + hp)),
            pl.BlockSpec((1, s, LANES), lambda bi, hp, qi: (bi, 0, 2 * N_PAIRS + hp)),
            pl.BlockSpec((2, N_WIN_A, TQ_A, TQ_A), lambda bi, hp, qi: (hp, 0, 0, 0)),
        ],
        out_specs=pl.BlockSpec((1, TQ_A, LANES), lambda bi, hp, qi: (bi, qi, hp)),
        out_shape=jax.ShapeDtypeStruct((b, s, N_HEADS * HEAD_DIM), BF16),
        compiler_params=pltpu.CompilerParams(
            dimension_semantics=("arbitrary", "arbitrary", "arbitrary"), vmem_limit_bytes=VMEM_LIMIT),
        name="chunk_attn",
    )(proj, proj, proj, bias)


def _chunk_bias_tiles(rel_bias):
    r = np.arange(TQ_A)[:, None]
    kj = np.arange(N_WIN_A * TQ_A)[None, :]
    rel = np.clip(r + LEFT_CHUNKS * CHUNK - kj, -MAX_REL, MAX_REL) + MAX_REL
    off = kj - CHUNK * (r // CHUNK)
    in_band = (off >= 0) & (off < BAND)
    onehot = jnp.asarray(rel[..., None] == np.arange(2 * MAX_REL + 1), F32)
    table = jnp.einsum("rkn,hn->hrk", onehot, rel_bias.astype(F32), precision=lax.Precision.HIGHEST)
    bias = jnp.where(in_band[None], table * LOG2E, NEG_BIG)
    h = rel_bias.shape[0]
    return bias.reshape(h, TQ_A, N_WIN_A, TQ_A).transpose(0, 2, 1, 3)


def _stick_kernel(q_ref, k_ref, vt_ref, tri_ref, o_ref, q2_ref, c_ref, acc_ref):
    qi = pl.program_id(2)
    tq = q_ref.shape[1]
    heads = range(acc_ref.shape[0])
    pair = lambda h: slice((h // 2) * LANES, (h // 2 + 1) * LANES)
    lane = lax.broadcasted_iota(jnp.int32, (tq, LANES), 1)
    for h in heads:
        q = q_ref[0, :, pair(h)]
        q2_ref[h] = jnp.where((lane >= HEAD_DIM) == bool(h % 2), q, jnp.zeros_like(q))
    c_ref[...] = jnp.zeros_like(c_ref)
    acc_ref[...] = jnp.zeros_like(acc_ref)

    def tile(kstart, mask):
        zs = [lax.dot_general(k_ref[0, pl.ds(kstart, tq), pair(h)], q2_ref[h], NT_DIMS,
                              preferred_element_type=F32) for h in heads]
        c_old = [c_ref[h] for h in heads]
        acc_old = [acc_ref[h] for h in heads]
        nlk = [jnp.maximum(z, 0.0) + jnp.log(1.0 + jnp.exp2(-jnp.abs(z))) * LOG2E for z in zs]
        if mask is not None:
            nlk = [jnp.where(mask, x, 0.0) for x in nlk]
        rs = [jnp.dot(tri_ref[...], x.astype(BF16), preferred_element_type=F32) for x in nlk]
        ws = [jnp.exp2(zs[h] - rs[h] - c_old[h]) for h in heads]
        if mask is not None:
            ws = [jnp.where(mask, w, 0.0) for w in ws]
        pvs = [jnp.dot(vt_ref[0, h // 2, :, pl.ds(kstart, tq)], ws[h].astype(BF16), preferred_element_type=F32)
               for h in heads]
        for h in heads:
            c_ref[h] = c_old[h] + jnp.sum(nlk[h], axis=0, keepdims=True)
            acc_ref[h] = acc_old[h] + pvs[h]

    key = lax.broadcasted_iota(jnp.int32, (tq, tq), 0)
    qry = lax.broadcasted_iota(jnp.int32, (tq, tq), 1)
    tile(pl.multiple_of(qi * tq, tq), key < qry)

    def body(it, carry):
        tile(pl.multiple_of((qi - 1 - it) * tq, tq), None)
        return carry

    lax.fori_loop(0, qi, body, 0)
    for pr in range(len(heads) // 2):
        o_ref[0, :, pr * LANES:(pr + 1) * LANES] = jnp.where(
            lane < HEAD_DIM, acc_ref[2 * pr].T, acc_ref[2 * pr + 1].T).astype(o_ref.dtype)


def _stick_attn(proj, vt, tri):
    b, s, _ = proj.shape
    nh = HEADS_PER_STEP
    width = nh * HEAD_DIM
    q0 = 3 * N_HEADS * HEAD_DIM // width
    k0 = 4 * N_HEADS * HEAD_DIM // width
    resident = dict(pipeline_mode=pl.Buffered(1))
    return pl.pallas_call(
        _stick_kernel,
        grid=(b, N_HEADS // nh, s // TQ),
        in_specs=[
            pl.BlockSpec((1, TQ, width), lambda bi, hg, qi: (bi, qi, q0 + hg)),
            pl.BlockSpec((1, s, width), lambda bi, hg, qi: (bi, 0, k0 + hg), **resident),
            pl.BlockSpec((1, nh // 2, LANES, s), lambda bi, hg, qi: (bi, hg, 0, 0), **resident),
            pl.BlockSpec((TQ, TQ), lambda bi, hg, qi: (0, 0)),
        ],
        out_specs=pl.BlockSpec((1, TQ, width), lambda bi, hg, qi: (bi, qi, hg)),
        out_shape=jax.ShapeDtypeStruct((b, s, N_HEADS * HEAD_DIM), BF16),
        scratch_shapes=[pltpu.VMEM((nh, TQ, LANES), BF16), pltpu.VMEM((nh, 1, TQ), F32),
                        pltpu.VMEM((nh, LANES, TQ), F32)],
        compiler_params=pltpu.CompilerParams(
            dimension_semantics=("arbitrary", "arbitrary", "arbitrary"), vmem_limit_bytes=VMEM_LIMIT),
        name="stick_attn",
    )(proj, proj, vt, tri)


def _softmax_attn_kernel(q_ref, k_ref, vt_ref, o_ref, m_ref, l_ref, acc_ref, *, chunk_mask):
    qi = pl.program_id(2)
    tq = q_ref.shape[1]
    heads = range(q_ref.shape[2] // LANES)
    sl = lambda h: slice(h * LANES, (h + 1) * LANES)
    m_ref[...] = jnp.full_like(m_ref, NEG_BIG)
    l_ref[...] = jnp.zeros_like(l_ref)
    acc_ref[...] = jnp.zeros_like(acc_ref)

    def tile(kstart, mask):
        ss = [lax.dot_general(k_ref[0, pl.ds(kstart, tq), sl(h)], q_ref[0, :, sl(h)], NT_DIMS,
                              preferred_element_type=F32) for h in heads]
        m_old = [m_ref[h] for h in heads]
        l_old = [l_ref[h] for h in heads]
        acc_old = [acc_ref[h] for h in heads]
        if mask is not None:
            ss = [jnp.where(mask, s, NEG_BIG) for s in ss]
        m_new = [jnp.maximum(m_old[h], jnp.max(ss[h], axis=0, keepdims=True)) for h in heads]
        alpha = [jnp.exp2(m_old[h] - m_new[h]) for h in heads]
        ps = [jnp.exp2(ss[h] - m_new[h]) for h in heads]
        l_new = [alpha[h] * l_old[h] + jnp.sum(ps[h], axis=0, keepdims=True) for h in heads]
        pvs = [jnp.dot(vt_ref[0, h // 2, :, pl.ds(kstart, tq)], ps[h].astype(BF16), preferred_element_type=F32)
               for h in heads]
        for h in heads:
            m_ref[h] = m_new[h]
            l_ref[h] = l_new[h]
            acc_ref[h] = alpha[h] * acc_old[h] + pvs[h]

    def body(j, carry):
        tile(pl.multiple_of(j * tq, tq), None)
        return carry

    lax.fori_loop(0, qi, body, 0)
    key = lax.broadcasted_iota(jnp.int32, (tq, tq), 0)
    qry = lax.broadcasted_iota(jnp.int32, (tq, tq), 1)
    tile(pl.multiple_of(qi * tq, tq), (key // CHUNK <= qry // CHUNK) if chunk_mask else (key <= qry))
    lane = lax.broadcasted_iota(jnp.int32, (tq, LANES), 1)
    for pr in range(len(heads) // 2):
        oa = (acc_ref[2 * pr] / l_ref[2 * pr]).T
        ob = (acc_ref[2 * pr + 1] / l_ref[2 * pr + 1]).T
        o_ref[0, :, pr * LANES:(pr + 1) * LANES] = jnp.where(lane < HEAD_DIM, oa, ob).astype(o_ref.dtype)


def _softmax_attn(q, k, vt, chunk_mask, name):
    b, s, _ = q.shape
    nh = HEADS_PER_STEP
    resident = dict(pipeline_mode=pl.Buffered(1))
    return pl.pallas_call(
        functools.partial(_softmax_attn_kernel, chunk_mask=chunk_mask),
        grid=(b, N_HEADS // nh, s // TQ),
        in_specs=[
            pl.BlockSpec((1, TQ, nh * LANES), lambda bi, hg, qi: (bi, qi, hg)),
            pl.BlockSpec((1, s, nh * LANES), lambda bi, hg, qi: (bi, 0, hg), **resident),
            pl.BlockSpec((1, nh // 2, LANES, s), lambda bi, hg, qi: (bi, hg, 0, 0), **resident),
        ],
        out_specs=pl.BlockSpec((1, TQ, nh * HEAD_DIM), lambda bi, hg, qi: (bi, qi, hg)),
        out_shape=jax.ShapeDtypeStruct((b, s, N_HEADS * HEAD_DIM), BF16),
        scratch_shapes=[pltpu.VMEM((nh, 1, TQ), F32), pltpu.VMEM((nh, 1, TQ), F32),
                        pltpu.VMEM((nh, LANES, TQ), F32)],
        compiler_params=pltpu.CompilerParams(
            dimension_semantics=("arbitrary", "arbitrary", "arbitrary"), vmem_limit_bytes=VMEM_LIMIT),
        name=name,
    )(q, k, vt)


def _out_ffn_kernel(*refs, tiles_per_seq, final_norm):
    if final_norm:
        (x_ref, o1_ref, o2_ref, wo_ref, g_ref, wg_ref, wu_ref, cw_ref, cb_ref, wd_ref, fg_ref,
         out_ref, x1_ref, h_ref, acc_ref, gbuf_ref, tail_ref) = refs
    else:
        (x_ref, o1_ref, o2_ref, wo_ref, g_ref, wg_ref, wu_ref, cw_ref, cb_ref, wd_ref,
         out_ref, x1_ref, h_ref, acc_ref, gbuf_ref, tail_ref) = refs
    i = pl.program_id(0)
    f = pl.program_id(1)
    nf = pl.num_programs(1)
    tm = x_ref.shape[0]
    half = o1_ref.shape[1]

    @pl.when(f == 0)
    def _():
        x1 = (x_ref[...]
              + jnp.dot(o1_ref[...], wo_ref[0:half, :], preferred_element_type=F32)
              + jnp.dot(o2_ref[...], wo_ref[half:2 * half, :], preferred_element_type=F32))
        x1_ref[...] = x1
        h_ref[...] = _rms(x1, g_ref[...]).astype(BF16)
        acc_ref[...] = jnp.zeros_like(acc_ref)

    h = h_ref[...]
    g = jnp.dot(h, wg_ref[...], preferred_element_type=F32)
    u = jnp.dot(h, wu_ref[...], preferred_element_type=F32)

    prev = jnp.where(i % tiles_per_seq == 0, 0.0, tail_ref[f])
    gbuf_ref[0:8, :] = prev
    gbuf_ref[8:8 + tm, :] = g
    tail_ref[f] = g[tm - 8:tm, :]
    gm1 = gbuf_ref[7:7 + tm, :]
    gm2 = gbuf_ref[6:6 + tm, :]
    cw = cw_ref[...]
    gc = cw[0:1, :] * gm2 + cw[1:2, :] * gm1 + cw[2:3, :] * g + cb_ref[...]
    y = (gc / (1.0 + jnp.exp(-gc)) * u).astype(BF16)
    acc_ref[...] += jnp.dot(y, wd_ref[...], preferred_element_type=F32)

    @pl.when(f == nf - 1)
    def _():
        res = x1_ref[...] + acc_ref[...]
        if final_norm:
            res = _rms(res, fg_ref[...])
        out_ref[...] = res


def _out_ffn(x2d, o1, o2, wo, g, wg, wu, cw, cb, wd, final_g, seq_len):
    n, d = x2d.shape
    half = o1.shape[1]
    dff = wg.shape[1]
    tm, tf = TM_FFN, TF_FFN
    nf = dff // tf
    final_norm = final_g is not None
    in_specs = [
        pl.BlockSpec((tm, d), lambda i, f: (i, 0)),
        pl.BlockSpec((tm, half), lambda i, f: (i, 0)),
        pl.BlockSpec((tm, half), lambda i, f: (i, 0)),
        pl.BlockSpec((d, d), lambda i, f: (0, 0)),
        pl.BlockSpec((1, d), lambda i, f: (0, 0)),
        pl.BlockSpec((d, tf), lambda i, f: (0, f)),
        pl.BlockSpec((d, tf), lambda i, f: (0, f)),
        pl.BlockSpec((3, tf), lambda i, f: (0, f)),
        pl.BlockSpec((1, tf), lambda i, f: (0, f)),
        pl.BlockSpec((tf, d), lambda i, f: (f, 0)),
    ]
    args = [x2d, o1, o2, wo, g, wg, wu, cw, cb, wd]
    if final_norm:
        in_specs.append(pl.BlockSpec((1, d), lambda i, f: (0, 0)))
        args.append(final_g)
    return pl.pallas_call(
        functools.partial(_out_ffn_kernel, tiles_per_seq=seq_len // tm, final_norm=final_norm),
        grid=(n // tm, nf),
        in_specs=in_specs,
        out_specs=pl.BlockSpec((tm, d), lambda i, f: (i, 0)),
        out_shape=jax.ShapeDtypeStruct((n, d), F32),
        scratch_shapes=[
            pltpu.VMEM((tm, d), F32),
            pltpu.VMEM((tm, d), BF16),
            pltpu.VMEM((tm, d), F32),
            pltpu.VMEM((tm + 8, tf), F32),
            pltpu.VMEM((nf, 8, tf), F32),
        ],
        compiler_params=pltpu.CompilerParams(
            dimension_semantics=("arbitrary", "arbitrary"), vmem_limit_bytes=VMEM_LIMIT),
        name="out_ffn_final" if final_norm else "out_ffn",
    )(*args)


def _rope_tables(seq_len):
    half = QK_ROPE // 2
    inv = ROPE_THETA ** (-jnp.arange(half, dtype=F32) / half)
    ang = jnp.arange(seq_len, dtype=F32)[:, None] * inv[None, :]
    cos, sin = jnp.cos(ang), jnp.sin(ang)
    ones = jnp.ones((seq_len, QK_NOPE), F32)
    zeros = jnp.zeros((seq_len, QK_NOPE), F32)
    pad1 = jnp.ones((seq_len, LANES - QK_NOPE - QK_ROPE), F32)
    pad0 = jnp.zeros((seq_len, LANES - QK_NOPE - QK_ROPE), F32)
    return (jnp.concatenate([ones, cos, cos, pad1], axis=1),
            jnp.concatenate([zeros, -sin, sin, pad0], axis=1))


def _pad_heads(w, width):
    rows = w.shape[0]
    return jnp.zeros((rows, N_HEADS, LANES), w.dtype).at[:, :, :width].set(
        w.reshape(rows, N_HEADS, width)).reshape(rows, N_HEADS * LANES)


def _cd_weights(w_in, w_uq, w_ukv, b_f):
    d = w_in.shape[0]
    nd = N_HEADS * HEAD_DIM
    o = Q_RANK + KV_RANK + QK_ROPE
    c_q, c_kv, k_rope = w_in[:, :Q_RANK], w_in[:, Q_RANK:Q_RANK + KV_RANK], w_in[:, Q_RANK + KV_RANK:o]
    q_d, k_d, v_d, f_logit = (w_in[:, o:o + nd], w_in[:, o + nd:o + 2 * nd], w_in[:, o + 2 * nd:o + 3 * nd],
                              w_in[:, o + 3 * nd:])
    kr_blk = jnp.zeros((d, LANES), w_in.dtype).at[:, QK_NOPE:QK_NOPE + QK_ROPE].set(k_rope)
    f_blk = jnp.zeros((d, LANES), w_in.dtype).at[:, :N_HEADS].set(f_logit)
    w1 = jnp.concatenate([c_q, c_kv, kr_blk, _pad_heads(q_d, HEAD_DIM), _pad_heads(k_d, HEAD_DIM), v_d, f_blk],
                         axis=1).astype(BF16)
    wuq = _pad_heads(w_uq, QK_NOPE + QK_ROPE).astype(BF16)
    ukv = w_ukv.reshape(KV_RANK, N_HEADS, QK_NOPE + HEAD_DIM)
    wk = _pad_heads(ukv[:, :, :QK_NOPE].reshape(KV_RANK, N_HEADS * QK_NOPE), QK_NOPE).astype(BF16)
    wv = ukv[:, :, QK_NOPE:].reshape(KV_RANK, N_HEADS * HEAD_DIM).astype(BF16)
    bf = jnp.zeros((1, LANES), F32).at[0, :N_HEADS].set(b_f.astype(F32))
    return w1, wuq, wk, wv, bf


def _forget_selectors():
    selq = np.zeros((N_PIECES * LANES, _NPAD), np.float32)
    selk = np.zeros((N_PIECES * LANES, _NPAD), np.float32)
    oneq = np.zeros((1, _NPAD), np.float32)
    onek = np.zeros((1, _NPAD), np.float32)
    for h in range(N_HEADS):
        for j in range(N_PIECES):
            selq[j * LANES + h, h * LANES + F_LANE0 + j] = 1.0
            onek[0, h * LANES + F_LANE0 + j] = 1.0
            selk[j * LANES + h, h * LANES + ONE_LANE0 + j] = -1.0
            oneq[0, h * LANES + ONE_LANE0 + j] = 1.0
    return jnp.asarray(selq, BF16), jnp.asarray(selk, BF16), jnp.asarray(oneq), jnp.asarray(onek)


def kernel(x, ab_norm, ab_w_in, ab_rel_bias, ab_w_o, cd_norm, cd_w_in, cd_q_norm, cd_w_uq, cd_kv_norm, cd_w_ukv,
           cd_b_f, cd_w_o, ffn_norm, ffn_w_gate, ffn_w_up, ffn_conv_w, ffn_conv_b, ffn_w_down, final_norm):
    b, s, d = x.shape
    n = b * s

    qscale = jnp.full((N_HEADS * HEAD_DIM,), HEAD_DIM ** -0.5 * LOG2E, F32)
    one = jnp.ones((2 * N_HEADS * HEAD_DIM,), F32)
    colscale = jnp.concatenate([qscale, one, qscale, one])[None, :]
    proj, vbt = _ab_proj(x, ab_norm[0][None, :], ab_w_in[0].astype(BF16), colscale)
    oa = _chunk_attn(proj, _chunk_bias_tiles(ab_rel_bias[0]))
    tri = jnp.asarray(np.triu(np.ones((TQ, TQ), np.float32)), BF16)
    ob = _stick_attn(proj, vbt, tri)
    x2d = _out_ffn(x.reshape(n, d), oa.reshape(n, -1), ob.reshape(n, -1), ab_w_o[0].astype(BF16),
                   ffn_norm[0][None, :], ffn_w_gate[0].astype(BF16), ffn_w_up[0].astype(BF16), ffn_conv_w[0],
                   ffn_conv_b[0][None, :], ffn_w_down[0].astype(BF16), None, s)

    w1, wuq, wk, wv, bf = _cd_weights(cd_w_in[0], cd_w_uq[0], cd_w_ukv[0], cd_b_f[0])
    cos_t, sin_t = _rope_tables(s)
    qc, kc, vct, qd, kd, vdt = _cd_proj(x2d.reshape(b, s, d), cd_norm[0][None, :], w1, cd_q_norm[0][None, :], wuq,
                                        cd_kv_norm[0][None, :], wk, wv, bf, cos_t, sin_t, *_forget_selectors())
    oc = _softmax_attn(qc, kc, vct, True, "mla_attn")
    od = _softmax_attn(qd, kd, vdt, False, "fox_attn")
    out = _out_ffn(x2d, oc.reshape(n, -1), od.reshape(n, -1), cd_w_o[0].astype(BF16), ffn_norm[1][None, :],
                   ffn_w_gate[1].astype(BF16), ffn_w_up[1].astype(BF16), ffn_conv_w[1], ffn_conv_b[1][None, :],
                   ffn_w_down[1].astype(BF16), final_norm[None, :], s)
    return out.reshape(b, s, d)
```

```python
import functools
import math

import numpy as np
import jax
import jax.numpy as jnp
from jax import lax
from jax.experimental import pallas as pl
from jax.experimental.pallas import tpu as pltpu

F32 = jnp.float32
BF16 = jnp.bfloat16

D_MODEL = 1024
HEAD_DIM = 64
CHUNK = 64
LEFT_CHUNKS = 8
BAND = (LEFT_CHUNKS + 1) * CHUNK
MAX_REL = 128
N_HEADS = 8
N_PAIRS = N_HEADS // 2
QK_NOPE = 64
QK_ROPE = 32
Q_RANK = 384
KV_RANK = 256
ROPE_THETA = 10000.0
D_FF = 2816
RMS_EPS = 1e-6

LANES = 128
LOG2E = math.log2(math.e)
NEG_BIG = -1e30
VMEM_LIMIT = 52 * 1024 * 1024

TM_PROJ = 256
TM_FFN = 512
TF_FFN = 1408
TK = 256
TQ = 256
N_WIN_A = LEFT_CHUNKS * CHUNK // TK + 1
HEADS_PER_STEP = 8

N_PIECES = 3
F_LANE0 = HEAD_DIM
ONE_LANE0 = HEAD_DIM + N_PIECES

NT_DIMS = (((1,), (1,)), ((), ()))


def _rms(x, g):
    ms = jnp.mean(x * x, axis=-1, keepdims=True)
    return x * lax.rsqrt(ms + RMS_EPS) * g


def _store_pairs_transposed(vt_ref, v):
    for p in range(N_PAIRS):
        vt_ref[0, p] = v[:, p * LANES:(p + 1) * LANES].T.astype(vt_ref.dtype)


def _ab_proj_kernel(x_ref, g_ref, w_ref, cs_ref, o_ref, vat_ref, vbt_ref):
    h = _rms(x_ref[0], g_ref[...]).astype(BF16)
    p = jnp.dot(h, w_ref[...], preferred_element_type=F32) * cs_ref[...]
    o_ref[0] = p.astype(o_ref.dtype)
    nv = N_HEADS * HEAD_DIM
    _store_pairs_transposed(vat_ref, p[:, 2 * nv:3 * nv])
    _store_pairs_transposed(vbt_ref, p[:, 5 * nv:6 * nv])


def _ab_proj(x, g, w, colscale):
    b, s, d = x.shape
    nc = w.shape[1]
    tm = TM_PROJ
    vt_spec = pl.BlockSpec((1, N_PAIRS, LANES, tm), lambda bi, ti: (bi, 0, 0, ti))
    vt_shape = jax.ShapeDtypeStruct((b, N_PAIRS, LANES, s), BF16)
    return pl.pallas_call(
        _ab_proj_kernel,
        grid=(b, s // tm),
        in_specs=[
            pl.BlockSpec((1, tm, d), lambda bi, ti: (bi, ti, 0)),
            pl.BlockSpec((1, d), lambda bi, ti: (0, 0)),
            pl.BlockSpec((d, nc), lambda bi, ti: (0, 0)),
            pl.BlockSpec((1, nc), lambda bi, ti: (0, 0)),
        ],
        out_specs=[pl.BlockSpec((1, tm, nc), lambda bi, ti: (bi, ti, 0)), vt_spec, vt_spec],
        out_shape=[jax.ShapeDtypeStruct((b, s, nc), BF16), vt_shape, vt_shape],
        compiler_params=pltpu.CompilerParams(
            dimension_semantics=("arbitrary", "arbitrary"), vmem_limit_bytes=VMEM_LIMIT),
        name="ab_norm_proj",
    )(x, g, w, colscale)


_NPAD = N_HEADS * LANES
_C_Q0 = 0
_C_KV0 = _C_Q0 + Q_RANK
_C_KR0 = _C_KV0 + KV_RANK
_C_QD0 = _C_KR0 + LANES
_C_KD0 = _C_QD0 + _NPAD
_C_VD0 = _C_KD0 + _NPAD
_C_F0 = _C_VD0 + N_HEADS * HEAD_DIM
_C_END = _C_F0 + LANES


def _cd_proj_kernel(x_ref, g_ref, w1_ref, qn_ref, wuq_ref, kvn_ref, wk_ref, wv_ref, bf_ref, cos_ref, sin_ref,
                    selq_ref, selk_ref, oneq_ref, onek_ref,
                    qc_ref, kc_ref, vct_ref, qd_ref, kd_ref, vdt_ref, carry_ref):
    t = pl.program_id(1)
    tm = x_ref.shape[1]
    h = _rms(x_ref[0], g_ref[...]).astype(BF16)
    p = jnp.dot(h, w1_ref[...], preferred_element_type=F32)

    cq = _rms(p[:, _C_Q0:_C_KV0], qn_ref[...]).astype(BF16)
    ckv = _rms(p[:, _C_KV0:_C_KR0], kvn_ref[...]).astype(BF16)

    cosb = cos_ref[...]
    sinb = sin_ref[...]
    lane = lax.broadcasted_iota(jnp.int32, (tm, LANES), 1)

    def rope(xb):
        partner = jnp.where(lane < QK_NOPE + QK_ROPE // 2, pltpu.roll(xb, LANES - QK_ROPE // 2, 1),
                            pltpu.roll(xb, QK_ROPE // 2, 1))
        return xb * cosb + partner * sinb

    qc = jnp.dot(cq, wuq_ref[...], preferred_element_type=F32) * ((QK_NOPE + QK_ROPE) ** -0.5 * LOG2E)
    kc = jnp.dot(ckv, wk_ref[...], preferred_element_type=F32)
    kr = rope(p[:, _C_KR0:_C_QD0])
    for hh in range(N_HEADS):
        sl = slice(hh * LANES, (hh + 1) * LANES)
        qc_ref[0, :, sl] = rope(qc[:, sl]).astype(BF16)
        kc_ref[0, :, sl] = (kc[:, sl] + kr).astype(BF16)
    _store_pairs_transposed(vct_ref, jnp.dot(ckv, wv_ref[...], preferred_element_type=F32))
    _store_pairs_transposed(vdt_ref, p[:, _C_VD0:_C_F0])

    fl = p[:, _C_F0:_C_END] + bf_ref[...]
    y = jnp.minimum(fl, 0.0) - jnp.log(1.0 + jnp.exp(-jnp.abs(fl)))
    row = lax.broadcasted_iota(jnp.int32, (tm, LANES), 0)
    sh = 1
    while sh < tm:
        y = y + jnp.where(row >= sh, pltpu.roll(y, sh, 0), 0.0)
        sh *= 2

    @pl.when(t == 0)
    def _():
        carry_ref[...] = jnp.zeros_like(carry_ref)

    y = y + carry_ref[0:1, :]
    carry_ref[...] = jnp.broadcast_to(y[tm - 1:tm, :], carry_ref.shape)
    f2 = y * LOG2E

    hi = f2.astype(BF16)
    r1 = f2 - hi.astype(F32)
    mid = r1.astype(BF16)
    lo = (r1 - mid.astype(F32)).astype(BF16)
    fp = jnp.concatenate([hi, mid, lo], axis=1)
    qd = p[:, _C_QD0:_C_KD0] * (HEAD_DIM ** -0.5 * LOG2E)
    qd_ref[0] = (qd + jnp.dot(fp, selq_ref[...], preferred_element_type=F32) + oneq_ref[...]).astype(BF16)
    kd_ref[0] = (p[:, _C_KD0:_C_VD0] + jnp.dot(fp, selk_ref[...], preferred_element_type=F32)
                 + onek_ref[...]).astype(BF16)


def _cd_proj(x, g, w1, qn, wuq, kvn, wk, wv, bf, cos_t, sin_t, selq, selk, oneq, onek):
    b, s, d = x.shape
    tm = TM_PROJ
    const = lambda a: pl.BlockSpec(a.shape, lambda bi, ti: (0,) * a.ndim)
    tok = lambda nc: pl.BlockSpec((1, tm, nc), lambda bi, ti: (bi, ti, 0))
    vt_spec = pl.BlockSpec((1, N_PAIRS, LANES, tm), lambda bi, ti: (bi, 0, 0, ti))
    act = jax.ShapeDtypeStruct((b, s, _NPAD), BF16)
    vt = jax.ShapeDtypeStruct((b, N_PAIRS, LANES, s), BF16)
    return pl.pallas_call(
        _cd_proj_kernel,
        grid=(b, s // tm),
        in_specs=[
            tok(d), const(g), const(w1), const(qn), const(wuq), const(kvn), const(wk), const(wv), const(bf),
            pl.BlockSpec((tm, LANES), lambda bi, ti: (ti, 0)),
            pl.BlockSpec((tm, LANES), lambda bi, ti: (ti, 0)),
            const(selq), const(selk), const(oneq), const(onek),
        ],
        out_specs=[tok(_NPAD), tok(_NPAD), vt_spec, tok(_NPAD), tok(_NPAD), vt_spec],
        out_shape=[act, act, vt, act, act, vt],
        scratch_shapes=[pltpu.VMEM((8, LANES), F32)],
        compiler_params=pltpu.CompilerParams(
            dimension_semantics=("arbitrary", "arbitrary"), vmem_limit_bytes=VMEM_LIMIT),
        name="cd_norm_proj",
    )(x, g, w1, qn, wuq, kvn, wk, wv, bf, cos_t, sin_t, selq, selk, oneq, onek)


def _mask_pair_heads(q_ref, q2_ref):
    tq = q_ref.shape[1]
    lane = lax.broadcasted_iota(jnp.int32, (tq, LANES), 1)
    for h in range(q2_ref.shape[0]):
        q = q_ref[0, :, (h // 2) * LANES:(h // 2 + 1) * LANES]
        q2_ref[h] = jnp.where((lane >= HEAD_DIM) == bool(h % 2), q, jnp.zeros_like(q))


def _store_pair_heads(o_ref, outs):
    tq = outs[0].shape[1]
    lane = lax.broadcasted_iota(jnp.int32, (tq, LANES), 1)
    for pr in range(len(outs) // 2):
        o_ref[0, :, pr * LANES:(pr + 1) * LANES] = jnp.where(
            lane < HEAD_DIM, outs[2 * pr].T, outs[2 * pr + 1].T).astype(o_ref.dtype)


def _softmax_tile_update(ss, vts, m_ref, l_ref, acc_ref):
    heads = range(len(ss))
    m_old = [m_ref[h] for h in heads]
    l_old = [l_ref[h] for h in heads]
    acc_old = [acc_ref[h] for h in heads]
    m_new = [functools.reduce(jnp.maximum, [jnp.max(s, axis=0, keepdims=True) for s in ss[h]], m_old[h])
             for h in heads]
    alpha = [jnp.exp2(m_old[h] - m_new[h]) for h in heads]
    ps = [[jnp.exp2(s - m_new[h]) for s in ss[h]] for h in heads]
    l_new = [alpha[h] * l_old[h] + sum(jnp.sum(p, axis=0, keepdims=True) for p in ps[h]) for h in heads]
    pvs = [sum(jnp.dot(vt, p.astype(BF16), preferred_element_type=F32) for vt, p in zip(vts[h], ps[h]))
           for h in heads]
    for h in heads:
        m_ref[h] = m_new[h]
        l_ref[h] = l_new[h]
        acc_ref[h] = alpha[h] * acc_old[h] + pvs[h]


def _chunk_attn_kernel(q_ref, k_ref, vt_ref, bias_ref, o_ref, q2_ref, m_ref, l_ref, acc_ref):
    qi = pl.program_id(2)
    tq = q_ref.shape[1]
    heads = range(acc_ref.shape[0])
    _mask_pair_heads(q_ref, q2_ref)
    m_ref[...] = jnp.full_like(m_ref, NEG_BIG)
    l_ref[...] = jnp.zeros_like(l_ref)
    acc_ref[...] = jnp.zeros_like(acc_ref)

    def run(js):
        kstarts = [pl.multiple_of((qi - (N_WIN_A - 1) + j) * tq, tq) for j in js]
        ss = [[lax.dot_general(k_ref[0, pl.ds(ks, tq), (h // 2) * LANES:(h // 2 + 1) * LANES], q2_ref[h],
                               NT_DIMS, preferred_element_type=F32) + bias_ref[h, j] for j, ks in zip(js, kstarts)]
              for h in heads]
        vts = [[vt_ref[0, h // 2, :, pl.ds(ks, tq)] for ks in kstarts] for h in heads]
        _softmax_tile_update(ss, vts, m_ref, l_ref, acc_ref)
        _store_pair_heads(o_ref, [acc_ref[h] / l_ref[h] for h in heads])

    first = jnp.maximum(N_WIN_A - 1 - qi, 0)
    for f in range(N_WIN_A):
        pl.when(first == f)(functools.partial(run, list(range(f, N_WIN_A))))


def _chunk_attn(proj, vt, bias):
    b, s, _ = proj.shape
    nh = HEADS_PER_STEP
    width = nh * HEAD_DIM
    k0 = N_HEADS * HEAD_DIM // width
    resident = dict(pipeline_mode=pl.Buffered(1))
    return pl.pallas_call(
        _chunk_attn_kernel,
        grid=(b, N_HEADS // nh, s // TK),
        in_specs=[
            pl.BlockSpec((1, TK, width), lambda bi, hg, qi: (bi, qi, hg)),
            pl.BlockSpec((1, s, width), lambda bi, hg, qi: (bi, 0, k0 + hg), **resident),
            pl.BlockSpec((1, nh // 2, LANES, s), lambda bi, hg, qi: (bi, hg, 0, 0), **resident),
            pl.BlockSpec((nh, N_WIN_A, TK, TK), lambda bi, hg, qi: (hg, 0, 0, 0), **resident),
        ],
        out_specs=pl.BlockSpec((1, TK, width), lambda bi, hg, qi: (bi, qi, hg)),
        out_shape=jax.ShapeDtypeStruct((b, s, N_HEADS * HEAD_DIM), BF16),
        scratch_shapes=[pltpu.VMEM((nh, TK, LANES), BF16), pltpu.VMEM((nh, 1, TK), F32),
                        pltpu.VMEM((nh, 1, TK), F32), pltpu.VMEM((nh, LANES, TK), F32)],
        compiler_params=pltpu.CompilerParams(
            dimension_semantics=("arbitrary", "arbitrary", "arbitrary"), vmem_limit_bytes=VMEM_LIMIT),
        name="chunk_attn",
    )(proj, proj, vt, bias)


def _chunk_bias_tiles(rel_bias):
    h = rel_bias.shape[0]
    nq = TK
    nk = N_WIN_A * TK
    period = 1024
    assert nq + nk <= period + 1
    u = np.arange(period)
    signed = np.where(u < nk, u, u - period)
    idx = np.clip(LEFT_CHUNKS * CHUNK - signed, -MAX_REL, MAX_REL) + MAX_REL
    v = rel_bias.astype(F32)[:, idx] * LOG2E
    toep = jnp.tile(v, (1, nq))[:, :nq * (period - 1)].reshape(h, nq, period - 1)[:, :, :nk]
    r = np.arange(nq)[:, None]
    off = np.arange(nk)[None, :] - CHUNK * (r // CHUNK)
    in_band = (off >= 0) & (off < BAND)
    bias = jnp.where(in_band[None], toep, NEG_BIG)
    return bias.reshape(h, nq, N_WIN_A, TK).transpose(0, 2, 3, 1)


def _stick_kernel(q_ref, k_ref, vt_ref, tri_ref, o_ref, q2_ref, c_ref, acc_ref):
    qi = pl.program_id(2)
    tq = q_ref.shape[1]
    tk = tri_ref.shape[0]
    heads = range(acc_ref.shape[0])
    pair = lambda h: slice((h // 2) * LANES, (h // 2 + 1) * LANES)
    _mask_pair_heads(q_ref, q2_ref)
    c_ref[...] = jnp.zeros_like(c_ref)
    acc_ref[...] = jnp.zeros_like(acc_ref)
    sign_bit = jnp.uint32(0x80000000)

    def tiles(js, masked):
        nt = range(len(js))
        kstarts = [pl.multiple_of(j * tk, tk) for j in js]
        zs = [[lax.dot_general(k_ref[0, pl.ds(kstarts[t], tk), pair(h)], q2_ref[h], NT_DIMS,
                               preferred_element_type=F32) for t in nt] for h in heads]
        c_old = [c_ref[h] for h in heads]
        acc_old = [acc_ref[h] for h in heads]

        def neg_log_keep(z, t):
            neg_abs = lax.bitcast_convert_type(lax.bitcast_convert_type(z, jnp.uint32) | sign_bit, F32)
            nlk = jnp.maximum(z, 0.0) + jnp.log(1.0 + jnp.exp2(neg_abs)) * LOG2E
            return jnp.where(masks[t], nlk, 0.0) if masked[t] else nlk

        masks = [None] * len(js)
        for t in nt:
            if masked[t]:
                key = kstarts[t] + lax.broadcasted_iota(jnp.int32, (tk, tq), 0)
                qry = qi * tq + lax.broadcasted_iota(jnp.int32, (tk, tq), 1)
                masks[t] = key < qry
        nlk = [[neg_log_keep(zs[h][t], t) for t in nt] for h in heads]
        rs = [[jnp.dot(tri_ref[...], nlk[h][t].astype(BF16), preferred_element_type=F32) for t in nt] for h in heads]
        pvs = []
        c_new = []
        for h in heads:
            c = c_old[h]
            pv = None
            for t in nt:
                w = jnp.exp2(zs[h][t] - rs[h][t] - c)
                if masked[t]:
                    w = jnp.where(masks[t], w, 0.0)
                d = jnp.dot(vt_ref[0, h // 2, :, pl.ds(kstarts[t], tk)], w.astype(BF16),
                            preferred_element_type=F32)
                pv = d if pv is None else pv + d
                c = c + jnp.sum(nlk[h][t], axis=0, keepdims=True)
            pvs.append(pv)
            c_new.append(c)
        for h in heads:
            c_ref[h] = c_new[h]
            acc_ref[h] = acc_old[h] + pvs[h]

    assert tq == tk
    pl.when(qi % 2 == 1)(lambda: tiles([qi, qi - 1], [True, False]))
    pl.when(qi % 2 == 0)(lambda: tiles([qi], [True]))
    top = qi - 1 - qi % 2

    def body(it, carry):
        tiles([top - 2 * it, top - 2 * it - 1], [False, False])
        return carry

    lax.fori_loop(0, qi // 2, body, 0)
    _store_pair_heads(o_ref, [acc_ref[h] for h in heads])


def _stick_attn(proj, vt, tri):
    b, s, _ = proj.shape
    nh = HEADS_PER_STEP
    width = nh * HEAD_DIM
    q0 = 3 * N_HEADS * HEAD_DIM // width
    k0 = 4 * N_HEADS * HEAD_DIM // width
    resident = dict(pipeline_mode=pl.Buffered(1))
    return pl.pallas_call(
        _stick_kernel,
        grid=(b, N_HEADS // nh, s // TQ),
        in_specs=[
            pl.BlockSpec((1, TQ, width), lambda bi, hg, qi: (bi, qi, q0 + hg)),
            pl.BlockSpec((1, s, width), lambda bi, hg, qi: (bi, 0, k0 + hg), **resident),
            pl.BlockSpec((1, nh // 2, LANES, s), lambda bi, hg, qi: (bi, hg, 0, 0), **resident),
            pl.BlockSpec((TK, TK), lambda bi, hg, qi: (0, 0)),
        ],
        out_specs=pl.BlockSpec((1, TQ, width), lambda bi, hg, qi: (bi, qi, hg)),
        out_shape=jax.ShapeDtypeStruct((b, s, N_HEADS * HEAD_DIM), BF16),
        scratch_shapes=[pltpu.VMEM((nh, TQ, LANES), BF16), pltpu.VMEM((nh, 1, TQ), F32),
                        pltpu.VMEM((nh, LANES, TQ), F32)],
        compiler_params=pltpu.CompilerParams(
            dimension_semantics=("arbitrary", "arbitrary", "arbitrary"), vmem_limit_bytes=VMEM_LIMIT),
        name="stick_attn",
    )(proj, proj, vt, tri)


def _softmax_attn_kernel(q_ref, k_ref, vt_ref, o_ref, m_ref, l_ref, acc_ref, *, chunk_mask):
    qi = pl.program_id(2)
    tq = q_ref.shape[1]
    tk = TK
    heads = range(q_ref.shape[2] // LANES)
    sl = lambda h: slice(h * LANES, (h + 1) * LANES)
    m_ref[...] = jnp.full_like(m_ref, NEG_BIG)
    l_ref[...] = jnp.zeros_like(l_ref)
    acc_ref[...] = jnp.zeros_like(acc_ref)

    def tiles(js, masked):
        kstarts = [pl.multiple_of(j * tk, tk) for j in js]

        def scores(h, t):
            s = lax.dot_general(k_ref[0, pl.ds(kstarts[t], tk), sl(h)], q_ref[0, :, sl(h)], NT_DIMS,
                                preferred_element_type=F32)
            if masked[t]:
                key = kstarts[t] + lax.broadcasted_iota(jnp.int32, (tk, tq), 0)
                qry = qi * tq + lax.broadcasted_iota(jnp.int32, (tk, tq), 1)
                s = jnp.where((key // CHUNK <= qry // CHUNK) if chunk_mask else (key <= qry), s, NEG_BIG)
            return s

        ss = [[scores(h, t) for t in range(len(js))] for h in heads]
        vts = [[vt_ref[0, h // 2, :, pl.ds(ks, tk)] for ks in kstarts] for h in heads]
        _softmax_tile_update(ss, vts, m_ref, l_ref, acc_ref)

    def body(i, carry):
        tiles([2 * i, 2 * i + 1], [False, False])
        return carry

    assert tq == tk
    lax.fori_loop(0, qi // 2, body, 0)
    pl.when(qi % 2 == 1)(lambda: tiles([qi - 1, qi], [False, True]))
    pl.when(qi % 2 == 0)(lambda: tiles([qi], [True]))
    _store_pair_heads(o_ref, [acc_ref[h] / l_ref[h] for h in heads])


def _softmax_attn(q, k, vt, chunk_mask, name):
    b, s, _ = q.shape
    nh = HEADS_PER_STEP
    resident = dict(pipeline_mode=pl.Buffered(1))
    return pl.pallas_call(
        functools.partial(_softmax_attn_kernel, chunk_mask=chunk_mask),
        grid=(b, N_HEADS // nh, s // TQ),
        in_specs=[
            pl.BlockSpec((1, TQ, nh * LANES), lambda bi, hg, qi: (bi, qi, hg)),
            pl.BlockSpec((1, s, nh * LANES), lambda bi, hg, qi: (bi, 0, hg), **resident),
            pl.BlockSpec((1, nh // 2, LANES, s), lambda bi, hg, qi: (bi, hg, 0, 0), **resident),
        ],
        out_specs=pl.BlockSpec((1, TQ, nh * HEAD_DIM), lambda bi, hg, qi: (bi, qi, hg)),
        out_shape=jax.ShapeDtypeStruct((b, s, N_HEADS * HEAD_DIM), BF16),
        scratch_shapes=[pltpu.VMEM((nh, 1, TQ), F32), pltpu.VMEM((nh, 1, TQ), F32),
                        pltpu.VMEM((nh, LANES, TQ), F32)],
        compiler_params=pltpu.CompilerParams(
            dimension_semantics=("arbitrary", "arbitrary", "arbitrary"), vmem_limit_bytes=VMEM_LIMIT),
        name=name,
    )(q, k, vt)


def _out_ffn_kernel(*refs, tiles_per_seq, final_norm):
    if final_norm:
        (x_ref, o1_ref, o2_ref, wo_ref, g_ref, wg_ref, wu_ref, cw_ref, cb_ref, wd_ref, fg_ref,
         out_ref, x1_ref, h_ref, acc_ref, gbuf_ref, tail_ref) = refs
    else:
        (x_ref, o1_ref, o2_ref, wo_ref, g_ref, wg_ref, wu_ref, cw_ref, cb_ref, wd_ref,
         out_ref, x1_ref, h_ref, acc_ref, gbuf_ref, tail_ref) = refs
    i = pl.program_id(0)
    f = pl.program_id(1)
    nf = pl.num_programs(1)
    tm = x_ref.shape[0]
    half = o1_ref.shape[1]

    @pl.when(f == 0)
    def _():
        x1 = (x_ref[...]
              + jnp.dot(o1_ref[...], wo_ref[0:half, :], preferred_element_type=F32)
              + jnp.dot(o2_ref[...], wo_ref[half:2 * half, :], preferred_element_type=F32))
        x1_ref[...] = x1
        h_ref[...] = _rms(x1, g_ref[...]).astype(BF16)
        acc_ref[...] = jnp.zeros_like(acc_ref)

    h = h_ref[...]
    g = jnp.dot(h, wg_ref[...], preferred_element_type=F32)
    u = jnp.dot(h, wu_ref[...], preferred_element_type=F32)

    prev = jnp.where(i % tiles_per_seq == 0, 0.0, tail_ref[f])
    gbuf_ref[0:8, :] = prev
    gbuf_ref[8:8 + tm, :] = g
    tail_ref[f] = g[tm - 8:tm, :]
    gm1 = gbuf_ref[7:7 + tm, :]
    gm2 = gbuf_ref[6:6 + tm, :]
    cw = cw_ref[...]
    gc = cw[0:1, :] * gm2 + cw[1:2, :] * gm1 + cw[2:3, :] * g + cb_ref[...]
    y = (gc / (1.0 + jnp.exp(-gc)) * u).astype(BF16)
    acc_ref[...] += jnp.dot(y, wd_ref[...], preferred_element_type=F32)

    @pl.when(f == nf - 1)
    def _():
        res = x1_ref[...] + acc_ref[...]
        if final_norm:
            res = _rms(res, fg_ref[...])
        out_ref[...] = res


def _out_ffn(x2d, o1, o2, wo, g, wg, wu, cw, cb, wd, final_g, seq_len):
    n, d = x2d.shape
    half = o1.shape[1]
    dff = wg.shape[1]
    tm, tf = TM_FFN, TF_FFN
    nf = dff // tf
    final_norm = final_g is not None
    in_specs = [
        pl.BlockSpec((tm, d), lambda i, f: (i, 0)),
        pl.BlockSpec((tm, half), lambda i, f: (i, 0)),
        pl.BlockSpec((tm, half), lambda i, f: (i, 0)),
        pl.BlockSpec((d, d), lambda i, f: (0, 0)),
        pl.BlockSpec((1, d), lambda i, f: (0, 0)),
        pl.BlockSpec((d, tf), lambda i, f: (0, f)),
        pl.BlockSpec((d, tf), lambda i, f: (0, f)),
        pl.BlockSpec((3, tf), lambda i, f: (0, f)),
        pl.BlockSpec((1, tf), lambda i, f: (0, f)),
        pl.BlockSpec((tf, d), lambda i, f: (f, 0)),
    ]
    args = [x2d, o1, o2, wo, g, wg, wu, cw, cb, wd]
    if final_norm:
        in_specs.append(pl.BlockSpec((1, d), lambda i, f: (0, 0)))
        args.append(final_g)
    return pl.pallas_call(
        functools.partial(_out_ffn_kernel, tiles_per_seq=seq_len // tm, final_norm=final_norm),
        grid=(n // tm, nf),
        in_specs=in_specs,
        out_specs=pl.BlockSpec((tm, d), lambda i, f: (i, 0)),
        out_shape=jax.ShapeDtypeStruct((n, d), F32),
        scratch_shapes=[
            pltpu.VMEM((tm, d), F32),
            pltpu.VMEM((tm, d), BF16),
            pltpu.VMEM((tm, d), F32),
            pltpu.VMEM((tm + 8, tf), F32),
            pltpu.VMEM((nf, 8, tf), F32),
        ],
        compiler_params=pltpu.CompilerParams(
            dimension_semantics=("arbitrary", "arbitrary"), vmem_limit_bytes=VMEM_LIMIT),
        name="out_ffn_final" if final_norm else "out_ffn",
    )(*args)


def _rope_tables(seq_len):
    half = QK_ROPE // 2
    inv = ROPE_THETA ** (-jnp.arange(half, dtype=F32) / half)
    ang = jnp.arange(seq_len, dtype=F32)[:, None] * inv[None, :]
    cos, sin = jnp.cos(ang), jnp.sin(ang)
    ones = jnp.ones((seq_len, QK_NOPE), F32)
    zeros = jnp.zeros((seq_len, QK_NOPE), F32)
    pad1 = jnp.ones((seq_len, LANES - QK_NOPE - QK_ROPE), F32)
    pad0 = jnp.zeros((seq_len, LANES - QK_NOPE - QK_ROPE), F32)
    return (jnp.concatenate([ones, cos, cos, pad1], axis=1),
            jnp.concatenate([zeros, -sin, sin, pad0], axis=1))


def _pad_heads(w, width):
    rows = w.shape[0]
    return jnp.zeros((rows, N_HEADS, LANES), w.dtype).at[:, :, :width].set(
        w.reshape(rows, N_HEADS, width)).reshape(rows, N_HEADS * LANES)


def _cd_weights(w_in, w_uq, w_ukv, b_f):
    d = w_in.shape[0]
    nd = N_HEADS * HEAD_DIM
    o = Q_RANK + KV_RANK + QK_ROPE
    c_q, c_kv, k_rope = w_in[:, :Q_RANK], w_in[:, Q_RANK:Q_RANK + KV_RANK], w_in[:, Q_RANK + KV_RANK:o]
    q_d, k_d, v_d, f_logit = (w_in[:, o:o + nd], w_in[:, o + nd:o + 2 * nd], w_in[:, o + 2 * nd:o + 3 * nd],
                              w_in[:, o + 3 * nd:])
    kr_blk = jnp.zeros((d, LANES), w_in.dtype).at[:, QK_NOPE:QK_NOPE + QK_ROPE].set(k_rope)
    f_blk = jnp.zeros((d, LANES), w_in.dtype).at[:, :N_HEADS].set(f_logit)
    w1 = jnp.concatenate([c_q, c_kv, kr_blk, _pad_heads(q_d, HEAD_DIM), _pad_heads(k_d, HEAD_DIM), v_d, f_blk],
                         axis=1).astype(BF16)
    wuq = _pad_heads(w_uq, QK_NOPE + QK_ROPE).astype(BF16)
    ukv = w_ukv.reshape(KV_RANK, N_HEADS, QK_NOPE + HEAD_DIM)
    wk = _pad_heads(ukv[:, :, :QK_NOPE].reshape(KV_RANK, N_HEADS * QK_NOPE), QK_NOPE).astype(BF16)
    wv = ukv[:, :, QK_NOPE:].reshape(KV_RANK, N_HEADS * HEAD_DIM).astype(BF16)
    bf = jnp.zeros((1, LANES), F32).at[0, :N_HEADS].set(b_f.astype(F32))
    return w1, wuq, wk, wv, bf


def _forget_selectors():
    selq = np.zeros((N_PIECES * LANES, _NPAD), np.float32)
    selk = np.zeros((N_PIECES * LANES, _NPAD), np.float32)
    oneq = np.zeros((1, _NPAD), np.float32)
    onek = np.zeros((1, _NPAD), np.float32)
    for h in range(N_HEADS):
        for j in range(N_PIECES):
            selq[j * LANES + h, h * LANES + F_LANE0 + j] = 1.0
            onek[0, h * LANES + F_LANE0 + j] = 1.0
            selk[j * LANES + h, h * LANES + ONE_LANE0 + j] = -1.0
            oneq[0, h * LANES + ONE_LANE0 + j] = 1.0
    return jnp.asarray(selq, BF16), jnp.asarray(selk, BF16), jnp.asarray(oneq), jnp.asarray(onek)


def kernel(x, ab_norm, ab_w_in, ab_rel_bias, ab_w_o, cd_norm, cd_w_in, cd_q_norm, cd_w_uq, cd_kv_norm, cd_w_ukv,
           cd_b_f, cd_w_o, ffn_norm, ffn_w_gate, ffn_w_up, ffn_conv_w, ffn_conv_b, ffn_w_down, final_norm):
    b, s, d = x.shape
    n = b * s

    qscale = jnp.full((N_HEADS * HEAD_DIM,), HEAD_DIM ** -0.5 * LOG2E, F32)
    one = jnp.ones((2 * N_HEADS * HEAD_DIM,), F32)
    colscale = jnp.concatenate([qscale, one, qscale, one])[None, :]
    proj, vat, vbt = _ab_proj(x, ab_norm[0][None, :], ab_w_in[0].astype(BF16), colscale)
    oa = _chunk_attn(proj, vat, _chunk_bias_tiles(ab_rel_bias[0]))
    tri = jnp.asarray(np.triu(np.ones((TK, TK), np.float32)), BF16)
    ob = _stick_attn(proj, vbt, tri)
    x2d = _out_ffn(x.reshape(n, d), oa.reshape(n, -1), ob.reshape(n, -1), ab_w_o[0].astype(BF16),
                   ffn_norm[0][None, :], ffn_w_gate[0].astype(BF16), ffn_w_up[0].astype(BF16), ffn_conv_w[0],
                   ffn_conv_b[0][None, :], ffn_w_down[0].astype(BF16), None, s)

    w1, wuq, wk, wv, bf = _cd_weights(cd_w_in[0], cd_w_uq[0], cd_w_ukv[0], cd_b_f[0])
    cos_t, sin_t = _rope_tables(s)
    qc, kc, vct, qd, kd, vdt = _cd_proj(x2d.reshape(b, s, d), cd_norm[0][None, :], w1, cd_q_norm[0][None, :], wuq,
                                        cd_kv_norm[0][None, :], wk, wv, bf, cos_t, sin_t, *_forget_selectors())
    oc = _softmax_attn(qc, kc, vct, True, "mla_attn")
    od = _softmax_attn(qd, kd, vdt, False, "fox_attn")
    out = _out_ffn(x2d, oc.reshape(n, -1), od.reshape(n, -1), cd_w_o[0].astype(BF16), ffn_norm[1][None, :],
                   ffn_w_gate[1].astype(BF16), ffn_w_up[1].astype(BF16), ffn_conv_w[1], ffn_conv_b[1][None, :],
                   ffn_w_down[1].astype(BF16), final_norm[None, :], s)
    return out.reshape(b, s, d)
```

```python
import functools
import math

import numpy as np
import jax
import jax.numpy as jnp
from jax import lax
from jax.experimental import pallas as pl
from jax.experimental.pallas import tpu as pltpu

F32 = jnp.float32
BF16 = jnp.bfloat16

D_MODEL = 1024
HEAD_DIM = 64
CHUNK = 64
LEFT_CHUNKS = 8
BAND = (LEFT_CHUNKS + 1) * CHUNK
MAX_REL = 128
N_HEADS = 8
N_PAIRS = N_HEADS // 2
QK_NOPE = 64
QK_ROPE = 32
Q_RANK = 384
KV_RANK = 256
ROPE_THETA = 10000.0
D_FF = 2816
RMS_EPS = 1e-6

LANES = 128
LOG2E = math.log2(math.e)
NEG_BIG = -1e30
STICK_UNDERFLOW_LOG2 = 200.0
VMEM_LIMIT = 52 * 1024 * 1024

TM_PROJ = 256
TM_FFN = 512
TF_FFN = 1408
TK = 256
TQ = 256
N_WIN_A = LEFT_CHUNKS * CHUNK // TK + 1
HEADS_PER_STEP = 8

N_PIECES = 3
F_LANE0 = HEAD_DIM
ONE_LANE0 = HEAD_DIM + N_PIECES

NT_DIMS = (((1,), (1,)), ((), ()))


def _rms(x, g):
    ms = jnp.mean(x * x, axis=-1, keepdims=True)
    return x * lax.rsqrt(ms + RMS_EPS) * g


def _store_pairs_transposed(vt_ref, v):
    for p in range(N_PAIRS):
        vt_ref[0, p] = v[:, p * LANES:(p + 1) * LANES].T.astype(vt_ref.dtype)


def _ab_proj_kernel(x_ref, g_ref, w_ref, cs_ref, o_ref, vat_ref, vbt_ref):
    h = _rms(x_ref[0], g_ref[...]).astype(BF16)
    p = jnp.dot(h, w_ref[...], preferred_element_type=F32) * cs_ref[...]
    o_ref[0] = p.astype(o_ref.dtype)
    nv = N_HEADS * HEAD_DIM
    _store_pairs_transposed(vat_ref, p[:, 2 * nv:3 * nv])
    _store_pairs_transposed(vbt_ref, p[:, 5 * nv:6 * nv])


def _ab_proj(x, g, w, colscale):
    b, s, d = x.shape
    nc = w.shape[1]
    tm = TM_PROJ
    vt_spec = pl.BlockSpec((1, N_PAIRS, LANES, tm), lambda bi, ti: (bi, 0, 0, ti))
    vt_shape = jax.ShapeDtypeStruct((b, N_PAIRS, LANES, s), BF16)
    return pl.pallas_call(
        _ab_proj_kernel,
        grid=(b, s // tm),
        in_specs=[
            pl.BlockSpec((1, tm, d), lambda bi, ti: (bi, ti, 0)),
            pl.BlockSpec((1, d), lambda bi, ti: (0, 0)),
            pl.BlockSpec((d, nc), lambda bi, ti: (0, 0)),
            pl.BlockSpec((1, nc), lambda bi, ti: (0, 0)),
        ],
        out_specs=[pl.BlockSpec((1, tm, nc), lambda bi, ti: (bi, ti, 0)), vt_spec, vt_spec],
        out_shape=[jax.ShapeDtypeStruct((b, s, nc), BF16), vt_shape, vt_shape],
        compiler_params=pltpu.CompilerParams(
            dimension_semantics=("arbitrary", "arbitrary"), vmem_limit_bytes=VMEM_LIMIT),
        name="ab_norm_proj",
    )(x, g, w, colscale)


_NPAD = N_HEADS * LANES
_C_Q0 = 0
_C_KV0 = _C_Q0 + Q_RANK
_C_KR0 = _C_KV0 + KV_RANK
_C_QD0 = _C_KR0 + LANES
_C_KD0 = _C_QD0 + _NPAD
_C_VD0 = _C_KD0 + _NPAD
_C_F0 = _C_VD0 + N_HEADS * HEAD_DIM
_C_END = _C_F0 + LANES


def _cd_proj_kernel(x_ref, g_ref, w1_ref, qn_ref, wuq_ref, kvn_ref, wk_ref, wv_ref, bf_ref, cos_ref, sin_ref,
                    selq_ref, selk_ref, oneq_ref, onek_ref,
                    qc_ref, kc_ref, vct_ref, qd_ref, kd_ref, vdt_ref, carry_ref):
    t = pl.program_id(1)
    tm = x_ref.shape[1]
    h = _rms(x_ref[0], g_ref[...]).astype(BF16)
    p = jnp.dot(h, w1_ref[...], preferred_element_type=F32)

    cq = _rms(p[:, _C_Q0:_C_KV0], qn_ref[...]).astype(BF16)
    ckv = _rms(p[:, _C_KV0:_C_KR0], kvn_ref[...]).astype(BF16)

    cosb = cos_ref[...]
    sinb = sin_ref[...]
    lane = lax.broadcasted_iota(jnp.int32, (tm, LANES), 1)

    def rope(xb):
        partner = jnp.where(lane < QK_NOPE + QK_ROPE // 2, pltpu.roll(xb, LANES - QK_ROPE // 2, 1),
                            pltpu.roll(xb, QK_ROPE // 2, 1))
        return xb * cosb + partner * sinb

    qc = jnp.dot(cq, wuq_ref[...], preferred_element_type=F32) * ((QK_NOPE + QK_ROPE) ** -0.5 * LOG2E)
    kc = jnp.dot(ckv, wk_ref[...], preferred_element_type=F32)
    kr = rope(p[:, _C_KR0:_C_QD0])
    for hh in range(N_HEADS):
        sl = slice(hh * LANES, (hh + 1) * LANES)
        qc_ref[0, :, sl] = rope(qc[:, sl]).astype(BF16)
        kc_ref[0, :, sl] = (kc[:, sl] + kr).astype(BF16)
    _store_pairs_transposed(vct_ref, jnp.dot(ckv, wv_ref[...], preferred_element_type=F32))
    _store_pairs_transposed(vdt_ref, p[:, _C_VD0:_C_F0])

    fl = p[:, _C_F0:_C_END] + bf_ref[...]
    y = jnp.minimum(fl, 0.0) - jnp.log(1.0 + jnp.exp(-jnp.abs(fl)))
    row = lax.broadcasted_iota(jnp.int32, (tm, LANES), 0)
    sh = 1
    while sh < tm:
        y = y + jnp.where(row >= sh, pltpu.roll(y, sh, 0), 0.0)
        sh *= 2

    @pl.when(t == 0)
    def _():
        carry_ref[...] = jnp.zeros_like(carry_ref)

    y = y + carry_ref[0:1, :]
    carry_ref[...] = jnp.broadcast_to(y[tm - 1:tm, :], carry_ref.shape)
    f2 = y * LOG2E

    hi = f2.astype(BF16)
    r1 = f2 - hi.astype(F32)
    mid = r1.astype(BF16)
    lo = (r1 - mid.astype(F32)).astype(BF16)
    fp = jnp.concatenate([hi, mid, lo], axis=1)
    qd = p[:, _C_QD0:_C_KD0] * (HEAD_DIM ** -0.5 * LOG2E)
    qd_ref[0] = (qd + jnp.dot(fp, selq_ref[...], preferred_element_type=F32) + oneq_ref[...]).astype(BF16)
    kd_ref[0] = (p[:, _C_KD0:_C_VD0] + jnp.dot(fp, selk_ref[...], preferred_element_type=F32)
                 + onek_ref[...]).astype(BF16)


def _cd_proj(x, g, w1, qn, wuq, kvn, wk, wv, bf, cos_t, sin_t, selq, selk, oneq, onek):
    b, s, d = x.shape
    tm = TM_PROJ
    const = lambda a: pl.BlockSpec(a.shape, lambda bi, ti: (0,) * a.ndim)
    tok = lambda nc: pl.BlockSpec((1, tm, nc), lambda bi, ti: (bi, ti, 0))
    vt_spec = pl.BlockSpec((1, N_PAIRS, LANES, tm), lambda bi, ti: (bi, 0, 0, ti))
    act = jax.ShapeDtypeStruct((b, s, _NPAD), BF16)
    vt = jax.ShapeDtypeStruct((b, N_PAIRS, LANES, s), BF16)
    return pl.pallas_call(
        _cd_proj_kernel,
        grid=(b, s // tm),
        in_specs=[
            tok(d), const(g), const(w1), const(qn), const(wuq), const(kvn), const(wk), const(wv), const(bf),
            pl.BlockSpec((tm, LANES), lambda bi, ti: (ti, 0)),
            pl.BlockSpec((tm, LANES), lambda bi, ti: (ti, 0)),
            const(selq), const(selk), const(oneq), const(onek),
        ],
        out_specs=[tok(_NPAD), tok(_NPAD), vt_spec, tok(_NPAD), tok(_NPAD), vt_spec],
        out_shape=[act, act, vt, act, act, vt],
        scratch_shapes=[pltpu.VMEM((8, LANES), F32)],
        compiler_params=pltpu.CompilerParams(
            dimension_semantics=("arbitrary", "arbitrary"), vmem_limit_bytes=VMEM_LIMIT),
        name="cd_norm_proj",
    )(x, g, w1, qn, wuq, kvn, wk, wv, bf, cos_t, sin_t, selq, selk, oneq, onek)


def _mask_pair_heads(q_ref, q2_ref):
    tq = q_ref.shape[1]
    lane = lax.broadcasted_iota(jnp.int32, (tq, LANES), 1)
    for h in range(q2_ref.shape[0]):
        q = q_ref[0, :, (h // 2) * LANES:(h // 2 + 1) * LANES]
        q2_ref[h] = jnp.where((lane >= HEAD_DIM) == bool(h % 2), q, jnp.zeros_like(q))


def _store_pair_heads(o_ref, outs):
    tq = outs[0].shape[1]
    lane = lax.broadcasted_iota(jnp.int32, (tq, LANES), 1)
    for pr in range(len(outs) // 2):
        o_ref[0, :, pr * LANES:(pr + 1) * LANES] = jnp.where(
            lane < HEAD_DIM, outs[2 * pr].T, outs[2 * pr + 1].T).astype(o_ref.dtype)


def _softmax_tile_update(ss, vts, m_ref, l_ref, acc_ref):
    heads = range(len(ss))
    m_old = [m_ref[h] for h in heads]
    l_old = [l_ref[h] for h in heads]
    acc_old = [acc_ref[h] for h in heads]
    m_new = [functools.reduce(jnp.maximum, [jnp.max(s, axis=0, keepdims=True) for s in ss[h]], m_old[h])
             for h in heads]
    alpha = [jnp.exp2(m_old[h] - m_new[h]) for h in heads]
    ps = [[jnp.exp2(s - m_new[h]) for s in ss[h]] for h in heads]
    l_new = [alpha[h] * l_old[h] + sum(jnp.sum(p, axis=0, keepdims=True) for p in ps[h]) for h in heads]
    pvs = [sum(jnp.dot(vt, p.astype(BF16), preferred_element_type=F32) for vt, p in zip(vts[h], ps[h]))
           for h in heads]
    for h in heads:
        m_ref[h] = m_new[h]
        l_ref[h] = l_new[h]
        acc_ref[h] = alpha[h] * acc_old[h] + pvs[h]


def _chunk_attn_kernel(q_ref, k_ref, vt_ref, bias_ref, o_ref, q2_ref, m_ref, l_ref, acc_ref):
    qi = pl.program_id(2)
    tq = q_ref.shape[1]
    heads = range(acc_ref.shape[0])
    _mask_pair_heads(q_ref, q2_ref)
    m_ref[...] = jnp.full_like(m_ref, NEG_BIG)
    l_ref[...] = jnp.zeros_like(l_ref)
    acc_ref[...] = jnp.zeros_like(acc_ref)

    def run(js):
        kstarts = [pl.multiple_of((qi - (N_WIN_A - 1) + j) * tq, tq) for j in js]
        ss = [[lax.dot_general(k_ref[0, pl.ds(ks, tq), (h // 2) * LANES:(h // 2 + 1) * LANES], q2_ref[h],
                               NT_DIMS, preferred_element_type=F32) + bias_ref[h, j] for j, ks in zip(js, kstarts)]
              for h in heads]
        vts = [[vt_ref[0, h // 2, :, pl.ds(ks, tq)] for ks in kstarts] for h in heads]
        _softmax_tile_update(ss, vts, m_ref, l_ref, acc_ref)
        _store_pair_heads(o_ref, [acc_ref[h] / l_ref[h] for h in heads])

    first = jnp.maximum(N_WIN_A - 1 - qi, 0)
    for f in range(N_WIN_A):
        pl.when(first == f)(functools.partial(run, list(range(f, N_WIN_A))))


def _chunk_attn(proj, vt, bias):
    b, s, _ = proj.shape
    nh = HEADS_PER_STEP
    width = nh * HEAD_DIM
    k0 = N_HEADS * HEAD_DIM // width
    resident = dict(pipeline_mode=pl.Buffered(1))
    return pl.pallas_call(
        _chunk_attn_kernel,
        grid=(b, N_HEADS // nh, s // TK),
        in_specs=[
            pl.BlockSpec((1, TK, width), lambda bi, hg, qi: (bi, qi, hg)),
            pl.BlockSpec((1, s, width), lambda bi, hg, qi: (bi, 0, k0 + hg), **resident),
            pl.BlockSpec((1, nh // 2, LANES, s), lambda bi, hg, qi: (bi, hg, 0, 0), **resident),
            pl.BlockSpec((nh, N_WIN_A, TK, TK), lambda bi, hg, qi: (hg, 0, 0, 0), **resident),
        ],
        out_specs=pl.BlockSpec((1, TK, width), lambda bi, hg, qi: (bi, qi, hg)),
        out_shape=jax.ShapeDtypeStruct((b, s, N_HEADS * HEAD_DIM), BF16),
        scratch_shapes=[pltpu.VMEM((nh, TK, LANES), BF16), pltpu.VMEM((nh, 1, TK), F32),
                        pltpu.VMEM((nh, 1, TK), F32), pltpu.VMEM((nh, LANES, TK), F32)],
        compiler_params=pltpu.CompilerParams(
            dimension_semantics=("arbitrary", "arbitrary", "arbitrary"), vmem_limit_bytes=VMEM_LIMIT),
        name="chunk_attn",
    )(proj, proj, vt, bias)


def _chunk_bias_tiles(rel_bias):
    h = rel_bias.shape[0]
    nq = TK
    nk = N_WIN_A * TK
    period = 1024
    assert nq + nk <= period + 1
    u = np.arange(period)
    signed = np.where(u < nk, u, u - period)
    idx = np.clip(LEFT_CHUNKS * CHUNK - signed, -MAX_REL, MAX_REL) + MAX_REL
    v = rel_bias.astype(F32)[:, idx] * LOG2E
    toep = jnp.tile(v, (1, nq))[:, :nq * (period - 1)].reshape(h, nq, period - 1)[:, :, :nk]
    r = np.arange(nq)[:, None]
    off = np.arange(nk)[None, :] - CHUNK * (r // CHUNK)
    in_band = (off >= 0) & (off < BAND)
    bias = jnp.where(in_band[None], toep, NEG_BIG)
    return bias.reshape(h, nq, N_WIN_A, TK).transpose(0, 2, 3, 1)


def _stick_kernel(q_ref, k_ref, vt_ref, tri_ref, o_ref, q2_ref, c_ref, acc_ref):
    qi = pl.program_id(2)
    tq = q_ref.shape[1]
    tk = tri_ref.shape[0]
    heads = range(acc_ref.shape[0])
    pair = lambda h: slice((h // 2) * LANES, (h // 2 + 1) * LANES)
    _mask_pair_heads(q_ref, q2_ref)
    c_ref[...] = jnp.zeros_like(c_ref)
    acc_ref[...] = jnp.zeros_like(acc_ref)
    sign_bit = jnp.uint32(0x80000000)

    def tiles(js, masked):
        nt = range(len(js))
        kstarts = [pl.multiple_of(j * tk, tk) for j in js]
        zs = [[lax.dot_general(k_ref[0, pl.ds(kstarts[t], tk), pair(h)], q2_ref[h], NT_DIMS,
                               preferred_element_type=F32) for t in nt] for h in heads]
        c_old = [c_ref[h] for h in heads]
        acc_old = [acc_ref[h] for h in heads]

        def neg_log_keep(z, t):
            neg_abs = lax.bitcast_convert_type(lax.bitcast_convert_type(z, jnp.uint32) | sign_bit, F32)
            nlk = jnp.maximum(z, 0.0) + jnp.log(1.0 + jnp.exp2(neg_abs)) * LOG2E
            return jnp.where(masks[t], nlk, 0.0) if masked[t] else nlk

        masks = [None] * len(js)
        for t in nt:
            if masked[t]:
                key = kstarts[t] + lax.broadcasted_iota(jnp.int32, (tk, tq), 0)
                qry = qi * tq + lax.broadcasted_iota(jnp.int32, (tk, tq), 1)
                masks[t] = key < qry
        nlk = [[neg_log_keep(zs[h][t], t) for t in nt] for h in heads]
        rs = [[jnp.dot(tri_ref[...], nlk[h][t].astype(BF16), preferred_element_type=F32) for t in nt] for h in heads]
        pvs = []
        c_new = []
        for h in heads:
            c = c_old[h]
            pv = None
            for t in nt:
                w = jnp.exp2(zs[h][t] - rs[h][t] - c)
                if masked[t]:
                    w = jnp.where(masks[t], w, 0.0)
                d = jnp.dot(vt_ref[0, h // 2, :, pl.ds(kstarts[t], tk)], w.astype(BF16),
                            preferred_element_type=F32)
                pv = d if pv is None else pv + d
                c = c + jnp.sum(nlk[h][t], axis=0, keepdims=True)
            pvs.append(pv)
            c_new.append(c)
        for h in heads:
            c_ref[h] = c_new[h]
            acc_ref[h] = acc_old[h] + pvs[h]

    assert tq == tk
    pl.when(qi % 2 == 1)(lambda: tiles([qi, qi - 1], [True, False]))
    pl.when(qi % 2 == 0)(lambda: tiles([qi], [True]))
    top = qi - 1 - qi % 2

    def more(it):
        return jnp.logical_and(it < qi // 2, jnp.min(c_ref[...]) < STICK_UNDERFLOW_LOG2)

    def body(it):
        tiles([top - 2 * it, top - 2 * it - 1], [False, False])
        return it + 1

    lax.while_loop(more, body, 0)
    _store_pair_heads(o_ref, [acc_ref[h] for h in heads])


def _stick_attn(proj, vt, tri):
    b, s, _ = proj.shape
    nh = HEADS_PER_STEP
    width = nh * HEAD_DIM
    q0 = 3 * N_HEADS * HEAD_DIM // width
    k0 = 4 * N_HEADS * HEAD_DIM // width
    resident = dict(pipeline_mode=pl.Buffered(1))
    return pl.pallas_call(
        _stick_kernel,
        grid=(b, N_HEADS // nh, s // TQ),
        in_specs=[
            pl.BlockSpec((1, TQ, width), lambda bi, hg, qi: (bi, qi, q0 + hg)),
            pl.BlockSpec((1, s, width), lambda bi, hg, qi: (bi, 0, k0 + hg), **resident),
            pl.BlockSpec((1, nh // 2, LANES, s), lambda bi, hg, qi: (bi, hg, 0, 0), **resident),
            pl.BlockSpec((TK, TK), lambda bi, hg, qi: (0, 0)),
        ],
        out_specs=pl.BlockSpec((1, TQ, width), lambda bi, hg, qi: (bi, qi, hg)),
        out_shape=jax.ShapeDtypeStruct((b, s, N_HEADS * HEAD_DIM), BF16),
        scratch_shapes=[pltpu.VMEM((nh, TQ, LANES), BF16), pltpu.VMEM((nh, 1, TQ), F32),
                        pltpu.VMEM((nh, LANES, TQ), F32)],
        compiler_params=pltpu.CompilerParams(
            dimension_semantics=("arbitrary", "arbitrary", "arbitrary"), vmem_limit_bytes=VMEM_LIMIT),
        name="stick_attn",
    )(proj, proj, vt, tri)


def _softmax_attn_kernel(q_ref, k_ref, vt_ref, o_ref, m_ref, l_ref, acc_ref, *, chunk_mask):
    qi = pl.program_id(2)
    tq = q_ref.shape[1]
    tk = TK
    heads = range(q_ref.shape[2] // LANES)
    sl = lambda h: slice(h * LANES, (h + 1) * LANES)
    m_ref[...] = jnp.full_like(m_ref, NEG_BIG)
    l_ref[...] = jnp.zeros_like(l_ref)
    acc_ref[...] = jnp.zeros_like(acc_ref)

    def tiles(js, masked):
        kstarts = [pl.multiple_of(j * tk, tk) for j in js]

        def scores(h, t):
            s = lax.dot_general(k_ref[0, pl.ds(kstarts[t], tk), sl(h)], q_ref[0, :, sl(h)], NT_DIMS,
                                preferred_element_type=F32)
            if masked[t]:
                key = kstarts[t] + lax.broadcasted_iota(jnp.int32, (tk, tq), 0)
                qry = qi * tq + lax.broadcasted_iota(jnp.int32, (tk, tq), 1)
                s = jnp.where((key // CHUNK <= qry // CHUNK) if chunk_mask else (key <= qry), s, NEG_BIG)
            return s

        ss = [[scores(h, t) for t in range(len(js))] for h in heads]
        vts = [[vt_ref[0, h // 2, :, pl.ds(ks, tk)] for ks in kstarts] for h in heads]
        _softmax_tile_update(ss, vts, m_ref, l_ref, acc_ref)

    def body(i, carry):
        tiles([2 * i, 2 * i + 1], [False, False])
        return carry

    assert tq == tk
    lax.fori_loop(0, qi // 2, body, 0)
    pl.when(qi % 2 == 1)(lambda: tiles([qi - 1, qi], [False, True]))
    pl.when(qi % 2 == 0)(lambda: tiles([qi], [True]))
    _store_pair_heads(o_ref, [acc_ref[h] / l_ref[h] for h in heads])


def _softmax_attn(q, k, vt, chunk_mask, name):
    b, s, _ = q.shape
    nh = HEADS_PER_STEP
    resident = dict(pipeline_mode=pl.Buffered(1))
    return pl.pallas_call(
        functools.partial(_softmax_attn_kernel, chunk_mask=chunk_mask),
        grid=(b, N_HEADS // nh, s // TQ),
        in_specs=[
            pl.BlockSpec((1, TQ, nh * LANES), lambda bi, hg, qi: (bi, qi, hg)),
            pl.BlockSpec((1, s, nh * LANES), lambda bi, hg, qi: (bi, 0, hg), **resident),
            pl.BlockSpec((1, nh // 2, LANES, s), lambda bi, hg, qi: (bi, hg, 0, 0), **resident),
        ],
        out_specs=pl.BlockSpec((1, TQ, nh * HEAD_DIM), lambda bi, hg, qi: (bi, qi, hg)),
        out_shape=jax.ShapeDtypeStruct((b, s, N_HEADS * HEAD_DIM), BF16),
        scratch_shapes=[pltpu.VMEM((nh, 1, TQ), F32), pltpu.VMEM((nh, 1, TQ), F32),
                        pltpu.VMEM((nh, LANES, TQ), F32)],
        compiler_params=pltpu.CompilerParams(
            dimension_semantics=("arbitrary", "arbitrary", "arbitrary"), vmem_limit_bytes=VMEM_LIMIT),
        name=name,
    )(q, k, vt)


def _out_ffn_kernel(*refs, tiles_per_seq, final_norm):
    if final_norm:
        (x_ref, o1_ref, o2_ref, wo_ref, g_ref, wg_ref, wu_ref, cw_ref, cb_ref, wd_ref, fg_ref,
         out_ref, x1_ref, h_ref, acc_ref, gbuf_ref, tail_ref) = refs
    else:
        (x_ref, o1_ref, o2_ref, wo_ref, g_ref, wg_ref, wu_ref, cw_ref, cb_ref, wd_ref,
         out_ref, x1_ref, h_ref, acc_ref, gbuf_ref, tail_ref) = refs
    i = pl.program_id(0)
    f = pl.program_id(1)
    nf = pl.num_programs(1)
    tm = x_ref.shape[0]
    half = o1_ref.shape[1]

    @pl.when(f == 0)
    def _():
        x1 = (x_ref[...]
              + jnp.dot(o1_ref[...], wo_ref[0:half, :], preferred_element_type=F32)
              + jnp.dot(o2_ref[...], wo_ref[half:2 * half, :], preferred_element_type=F32))
        x1_ref[...] = x1
        h_ref[...] = _rms(x1, g_ref[...]).astype(BF16)
        acc_ref[...] = jnp.zeros_like(acc_ref)

    h = h_ref[...]
    g = jnp.dot(h, wg_ref[...], preferred_element_type=F32)
    u = jnp.dot(h, wu_ref[...], preferred_element_type=F32)

    prev = jnp.where(i % tiles_per_seq == 0, 0.0, tail_ref[f])
    gbuf_ref[0:8, :] = prev
    gbuf_ref[8:8 + tm, :] = g
    tail_ref[f] = g[tm - 8:tm, :]
    gm1 = gbuf_ref[7:7 + tm, :]
    gm2 = gbuf_ref[6:6 + tm, :]
    cw = cw_ref[...]
    gc = cw[0:1, :] * gm2 + cw[1:2, :] * gm1 + cw[2:3, :] * g + cb_ref[...]
    y = (gc / (1.0 + jnp.exp(-gc)) * u).astype(BF16)
    acc_ref[...] += jnp.dot(y, wd_ref[...], preferred_element_type=F32)

    @pl.when(f == nf - 1)
    def _():
        res = x1_ref[...] + acc_ref[...]
        if final_norm:
            res = _rms(res, fg_ref[...])
        out_ref[...] = res


def _out_ffn(x2d, o1, o2, wo, g, wg, wu, cw, cb, wd, final_g, seq_len):
    n, d = x2d.shape
    half = o1.shape[1]
    dff = wg.shape[1]
    tm, tf = TM_FFN, TF_FFN
    nf = dff // tf
    final_norm = final_g is not None
    in_specs = [
        pl.BlockSpec((tm, d), lambda i, f: (i, 0)),
        pl.BlockSpec((tm, half), lambda i, f: (i, 0)),
        pl.BlockSpec((tm, half), lambda i, f: (i, 0)),
        pl.BlockSpec((d, d), lambda i, f: (0, 0)),
        pl.BlockSpec((1, d), lambda i, f: (0, 0)),
        pl.BlockSpec((d, tf), lambda i, f: (0, f)),
        pl.BlockSpec((d, tf), lambda i, f: (0, f)),
        pl.BlockSpec((3, tf), lambda i, f: (0, f)),
        pl.BlockSpec((1, tf), lambda i, f: (0, f)),
        pl.BlockSpec((tf, d), lambda i, f: (f, 0)),
    ]
    args = [x2d, o1, o2, wo, g, wg, wu, cw, cb, wd]
    if final_norm:
        in_specs.append(pl.BlockSpec((1, d), lambda i, f: (0, 0)))
        args.append(final_g)
    return pl.pallas_call(
        functools.partial(_out_ffn_kernel, tiles_per_seq=seq_len // tm, final_norm=final_norm),
        grid=(n // tm, nf),
        in_specs=in_specs,
        out_specs=pl.BlockSpec((tm, d), lambda i, f: (i, 0)),
        out_shape=jax.ShapeDtypeStruct((n, d), F32),
        scratch_shapes=[
            pltpu.VMEM((tm, d), F32),
            pltpu.VMEM((tm, d), BF16),
            pltpu.VMEM((tm, d), F32),
            pltpu.VMEM((tm + 8, tf), F32),
            pltpu.VMEM((nf, 8, tf), F32),
        ],
        compiler_params=pltpu.CompilerParams(
            dimension_semantics=("arbitrary", "arbitrary"), vmem_limit_bytes=VMEM_LIMIT),
        name="out_ffn_final" if final_norm else "out_ffn",
    )(*args)


def _rope_tables(seq_len):
    half = QK_ROPE // 2
    inv = ROPE_THETA ** (-jnp.arange(half, dtype=F32) / half)
    ang = jnp.arange(seq_len, dtype=F32)[:, None] * inv[None, :]
    cos, sin = jnp.cos(ang), jnp.sin(ang)
    ones = jnp.ones((seq_len, QK_NOPE), F32)
    zeros = jnp.zeros((seq_len, QK_NOPE), F32)
    pad1 = jnp.ones((seq_len, LANES - QK_NOPE - QK_ROPE), F32)
    pad0 = jnp.zeros((seq_len, LANES - QK_NOPE - QK_ROPE), F32)
    return (jnp.concatenate([ones, cos, cos, pad1], axis=1),
            jnp.concatenate([zeros, -sin, sin, pad0], axis=1))


def _pad_heads(w, width):
    rows = w.shape[0]
    return jnp.zeros((rows, N_HEADS, LANES), w.dtype).at[:, :, :width].set(
        w.reshape(rows, N_HEADS, width)).reshape(rows, N_HEADS * LANES)


def _cd_weights(w_in, w_uq, w_ukv, b_f):
    d = w_in.shape[0]
    nd = N_HEADS * HEAD_DIM
    o = Q_RANK + KV_RANK + QK_ROPE
    c_q, c_kv, k_rope = w_in[:, :Q_RANK], w_in[:, Q_RANK:Q_RANK + KV_RANK], w_in[:, Q_RANK + KV_RANK:o]
    q_d, k_d, v_d, f_logit = (w_in[:, o:o + nd], w_in[:, o + nd:o + 2 * nd], w_in[:, o + 2 * nd:o + 3 * nd],
                              w_in[:, o + 3 * nd:])
    kr_blk = jnp.zeros((d, LANES), w_in.dtype).at[:, QK_NOPE:QK_NOPE + QK_ROPE].set(k_rope)
    f_blk = jnp.zeros((d, LANES), w_in.dtype).at[:, :N_HEADS].set(f_logit)
    w1 = jnp.concatenate([c_q, c_kv, kr_blk, _pad_heads(q_d, HEAD_DIM), _pad_heads(k_d, HEAD_DIM), v_d, f_blk],
                         axis=1).astype(BF16)
    wuq = _pad_heads(w_uq, QK_NOPE + QK_ROPE).astype(BF16)
    ukv = w_ukv.reshape(KV_RANK, N_HEADS, QK_NOPE + HEAD_DIM)
    wk = _pad_heads(ukv[:, :, :QK_NOPE].reshape(KV_RANK, N_HEADS * QK_NOPE), QK_NOPE).astype(BF16)
    wv = ukv[:, :, QK_NOPE:].reshape(KV_RANK, N_HEADS * HEAD_DIM).astype(BF16)
    bf = jnp.zeros((1, LANES), F32).at[0, :N_HEADS].set(b_f.astype(F32))
    return w1, wuq, wk, wv, bf


def _forget_selectors():
    selq = np.zeros((N_PIECES * LANES, _NPAD), np.float32)
    selk = np.zeros((N_PIECES * LANES, _NPAD), np.float32)
    oneq = np.zeros((1, _NPAD), np.float32)
    onek = np.zeros((1, _NPAD), np.float32)
    for h in range(N_HEADS):
        for j in range(N_PIECES):
            selq[j * LANES + h, h * LANES + F_LANE0 + j] = 1.0
            onek[0, h * LANES + F_LANE0 + j] = 1.0
            selk[j * LANES + h, h * LANES + ONE_LANE0 + j] = -1.0
            oneq[0, h * LANES + ONE_LANE0 + j] = 1.0
    return jnp.asarray(selq, BF16), jnp.asarray(selk, BF16), jnp.asarray(oneq), jnp.asarray(onek)


def kernel(x, ab_norm, ab_w_in, ab_rel_bias, ab_w_o, cd_norm, cd_w_in, cd_q_norm, cd_w_uq, cd_kv_norm, cd_w_ukv,
           cd_b_f, cd_w_o, ffn_norm, ffn_w_gate, ffn_w_up, ffn_conv_w, ffn_conv_b, ffn_w_down, final_norm):
    b, s, d = x.shape
    n = b * s

    qscale = jnp.full((N_HEADS * HEAD_DIM,), HEAD_DIM ** -0.5 * LOG2E, F32)
    one = jnp.ones((2 * N_HEADS * HEAD_DIM,), F32)
    colscale = jnp.concatenate([qscale, one, qscale, one])[None, :]
    proj, vat, vbt = _ab_proj(x, ab_norm[0][None, :], ab_w_in[0].astype(BF16), colscale)
    oa = _chunk_attn(proj, vat, _chunk_bias_tiles(ab_rel_bias[0]))
    tri = jnp.asarray(np.triu(np.ones((TK, TK), np.float32)), BF16)
    ob = _stick_attn(proj, vbt, tri)
    x2d = _out_ffn(x.reshape(n, d), oa.reshape(n, -1), ob.reshape(n, -1), ab_w_o[0].astype(BF16),
                   ffn_norm[0][None, :], ffn_w_gate[0].astype(BF16), ffn_w_up[0].astype(BF16), ffn_conv_w[0],
                   ffn_conv_b[0][None, :], ffn_w_down[0].astype(BF16), None, s)

    w1, wuq, wk, wv, bf = _cd_weights(cd_w_in[0], cd_w_uq[0], cd_w_ukv[0], cd_b_f[0])
    cos_t, sin_t = _rope_tables(s)
    qc, kc, vct, qd, kd, vdt = _cd_proj(x2d.reshape(b, s, d), cd_norm[0][None, :], w1, cd_q_norm[0][None, :], wuq,
                                        cd_kv_norm[0][None, :], wk, wv, bf, cos_t, sin_t, *_forget_selectors())
    oc = _softmax_attn(qc, kc, vct, True, "mla_attn")
    od = _softmax_attn(qd, kd, vdt, False, "fox_attn")
    out = _out_ffn(x2d, oc.reshape(n, -1), od.reshape(n, -1), cd_w_o[0].astype(BF16), ffn_norm[1][None, :],
                   ffn_w_gate[1].astype(BF16), ffn_w_up[1].astype(BF16), ffn_conv_w[1], ffn_conv_b[1][None, :],
                   ffn_w_down[1].astype(BF16), final_norm[None, :], s)
    return out.reshape(b, s, d)
```

```python
import functools
import math

import numpy as np
import jax
import jax.numpy as jnp
from jax import lax
from jax.experimental import pallas as pl
from jax.experimental.pallas import tpu as pltpu

F32 = jnp.float32
BF16 = jnp.bfloat16

D_MODEL = 1024
HEAD_DIM = 64
CHUNK = 64
LEFT_CHUNKS = 8
BAND = (LEFT_CHUNKS + 1) * CHUNK
MAX_REL = 128
N_HEADS = 8
N_PAIRS = N_HEADS // 2
QK_NOPE = 64
QK_ROPE = 32
Q_RANK = 384
KV_RANK = 256
ROPE_THETA = 10000.0
D_FF = 2816
RMS_EPS = 1e-6

LANES = 128
LOG2E = math.log2(math.e)
NEG_BIG = -1e30
STICK_UNDERFLOW_LOG2 = 200.0
SOFTMAX_UNDERFLOW_LOG2 = 160.0
NORM_MARGIN = 1.01
VMEM_LIMIT = 52 * 1024 * 1024

TM_PROJ = 256
TM_FFN = 256
TF_FFN = 2816
TK = 256
TQ = 256
N_WIN_A = LEFT_CHUNKS * CHUNK // TK + 1
HEADS_PER_STEP = 8
SUM_ROWS = 16

N_PIECES = 3
F_LANE0 = HEAD_DIM
ONE_LANE0 = HEAD_DIM + N_PIECES

NT_DIMS = (((1,), (1,)), ((), ()))


def _rms(x, g):
    ms = jnp.mean(x * x, axis=-1, keepdims=True)
    return x * lax.rsqrt(ms + RMS_EPS) * g


def _store_pairs_transposed(vt_ref, v):
    for p in range(N_PAIRS):
        vt_ref[0, p] = v[:, p * LANES:(p + 1) * LANES].T.astype(vt_ref.dtype)


def _ab_proj_kernel(x_ref, g_ref, w_ref, cs_ref, o_ref, vat_ref, vbt_ref):
    h = _rms(x_ref[0], g_ref[...]).astype(BF16)
    p = jnp.dot(h, w_ref[...], preferred_element_type=F32) * cs_ref[...]
    o_ref[0] = p.astype(o_ref.dtype)
    nv = N_HEADS * HEAD_DIM
    _store_pairs_transposed(vat_ref, p[:, 2 * nv:3 * nv])
    _store_pairs_transposed(vbt_ref, p[:, 5 * nv:6 * nv])


def _ab_proj(x, g, w, colscale):
    b, s, d = x.shape
    nc = w.shape[1]
    tm = TM_PROJ
    vt_spec = pl.BlockSpec((1, N_PAIRS, LANES, tm), lambda bi, ti: (bi, 0, 0, ti))
    vt_shape = jax.ShapeDtypeStruct((b, N_PAIRS, LANES, s), BF16)
    return pl.pallas_call(
        _ab_proj_kernel,
        grid=(b, s // tm),
        in_specs=[
            pl.BlockSpec((1, tm, d), lambda bi, ti: (bi, ti, 0)),
            pl.BlockSpec((1, d), lambda bi, ti: (0, 0)),
            pl.BlockSpec((d, nc), lambda bi, ti: (0, 0)),
            pl.BlockSpec((1, nc), lambda bi, ti: (0, 0)),
        ],
        out_specs=[pl.BlockSpec((1, tm, nc), lambda bi, ti: (bi, ti, 0)), vt_spec, vt_spec],
        out_shape=[jax.ShapeDtypeStruct((b, s, nc), BF16), vt_shape, vt_shape],
        compiler_params=pltpu.CompilerParams(
            dimension_semantics=("arbitrary", "arbitrary"), vmem_limit_bytes=VMEM_LIMIT),
        name="ab_norm_proj",
    )(x, g, w, colscale)


_NPAD = N_HEADS * LANES
_C_Q0 = 0
_C_KV0 = _C_Q0 + Q_RANK
_C_KR0 = _C_KV0 + KV_RANK
_C_QD0 = _C_KR0 + LANES
_C_KD0 = _C_QD0 + _NPAD
_C_VD0 = _C_KD0 + _NPAD
_C_F0 = _C_VD0 + N_HEADS * HEAD_DIM
_C_END = _C_F0 + LANES


def _cd_proj_kernel(x_ref, g_ref, w1_ref, qn_ref, wuq_ref, kvn_ref, wk_ref, wv_ref, bf_ref, cos_ref, sin_ref,
                    selq_ref, selk_ref, oneq_ref, onek_ref, hsel_ref,
                    qc_ref, kc_ref, vct_ref, qd_ref, kd_ref, vdt_ref, knorm_ref, fend_ref, carry_ref, kcarry_ref):
    t = pl.program_id(1)
    tm = x_ref.shape[1]
    h = _rms(x_ref[0], g_ref[...]).astype(BF16)
    p = jnp.dot(h, w1_ref[...], preferred_element_type=F32)

    cq = _rms(p[:, _C_Q0:_C_KV0], qn_ref[...]).astype(BF16)
    ckv = _rms(p[:, _C_KV0:_C_KR0], kvn_ref[...]).astype(BF16)

    cosb = cos_ref[...]
    sinb = sin_ref[...]
    lane = lax.broadcasted_iota(jnp.int32, (tm, LANES), 1)

    def rope(xb):
        partner = jnp.where(lane < QK_NOPE + QK_ROPE // 2, pltpu.roll(xb, LANES - QK_ROPE // 2, 1),
                            pltpu.roll(xb, QK_ROPE // 2, 1))
        return xb * cosb + partner * sinb

    qc = jnp.dot(cq, wuq_ref[...], preferred_element_type=F32) * ((QK_NOPE + QK_ROPE) ** -0.5 * LOG2E)
    kc = jnp.dot(ckv, wk_ref[...], preferred_element_type=F32)
    kr = rope(p[:, _C_KR0:_C_QD0])
    for hh in range(N_HEADS):
        sl = slice(hh * LANES, (hh + 1) * LANES)
        qc_ref[0, :, sl] = rope(qc[:, sl]).astype(BF16)
        kc_ref[0, :, sl] = (kc[:, sl] + kr).astype(BF16)
    _store_pairs_transposed(vct_ref, jnp.dot(ckv, wv_ref[...], preferred_element_type=F32))
    _store_pairs_transposed(vdt_ref, p[:, _C_VD0:_C_F0])

    fl = p[:, _C_F0:_C_END] + bf_ref[...]
    y = jnp.minimum(fl, 0.0) - jnp.log(1.0 + jnp.exp(-jnp.abs(fl)))
    row = lax.broadcasted_iota(jnp.int32, (tm, LANES), 0)
    sh = 1
    while sh < tm:
        y = y + jnp.where(row >= sh, pltpu.roll(y, sh, 0), 0.0)
        sh *= 2

    @pl.when(t == 0)
    def _():
        carry_ref[...] = jnp.zeros_like(carry_ref)
        kcarry_ref[...] = jnp.zeros_like(kcarry_ref)

    y = y + carry_ref[0:1, :]
    carry_ref[...] = jnp.broadcast_to(y[tm - 1:tm, :], carry_ref.shape)
    f2 = y * LOG2E

    hi = f2.astype(BF16).astype(F32)
    r1 = f2 - hi
    mid = r1.astype(BF16).astype(F32)
    lo = r1 - mid
    fp = jnp.where(lane < N_HEADS, hi, jnp.where(lane < 2 * N_HEADS, pltpu.roll(mid, N_HEADS, 1),
                                                 pltpu.roll(lo, 2 * N_HEADS, 1))).astype(BF16)
    qd = p[:, _C_QD0:_C_KD0] * (HEAD_DIM ** -0.5 * LOG2E)
    kd = p[:, _C_KD0:_C_VD0]
    qd_ref[0] = (qd + jnp.dot(fp, selq_ref[...], preferred_element_type=F32) + oneq_ref[...]).astype(BF16)
    kd_ref[0] = (kd + jnp.dot(fp, selk_ref[...], preferred_element_type=F32) + onek_ref[...]).astype(BF16)

    n2 = jnp.dot((kd * kd).astype(BF16), hsel_ref[...], preferred_element_type=F32)
    kmax = jnp.maximum(kcarry_ref[0:1, :], jnp.sqrt(jnp.max(n2, axis=0, keepdims=True)) * NORM_MARGIN)
    kcarry_ref[...] = jnp.broadcast_to(kmax, kcarry_ref.shape)
    knorm_ref[0, 0] = jnp.broadcast_to(kmax, knorm_ref.shape[2:])
    fend_ref[0, 0] = jnp.broadcast_to(f2[tm - 1:tm, :], fend_ref.shape[2:])


def _cd_proj(x, g, w1, qn, wuq, kvn, wk, wv, bf, cos_t, sin_t, selq, selk, oneq, onek, hsel):
    b, s, d = x.shape
    tm = TM_PROJ
    assert tm == TK
    const = lambda a: pl.BlockSpec(a.shape, lambda bi, ti: (0,) * a.ndim)
    tok = lambda nc: pl.BlockSpec((1, tm, nc), lambda bi, ti: (bi, ti, 0))
    vt_spec = pl.BlockSpec((1, N_PAIRS, LANES, tm), lambda bi, ti: (bi, 0, 0, ti))
    tab_spec = pl.BlockSpec((1, 1, 8, LANES), lambda bi, ti: (bi, ti, 0, 0))
    act = jax.ShapeDtypeStruct((b, s, _NPAD), BF16)
    vt = jax.ShapeDtypeStruct((b, N_PAIRS, LANES, s), BF16)
    tab = jax.ShapeDtypeStruct((b, s // tm, 8, LANES), F32)
    return pl.pallas_call(
        _cd_proj_kernel,
        grid=(b, s // tm),
        in_specs=[
            tok(d), const(g), const(w1), const(qn), const(wuq), const(kvn), const(wk), const(wv), const(bf),
            pl.BlockSpec((tm, LANES), lambda bi, ti: (ti, 0)),
            pl.BlockSpec((tm, LANES), lambda bi, ti: (ti, 0)),
            const(selq), const(selk), const(oneq), const(onek), const(hsel),
        ],
        out_specs=[tok(_NPAD), tok(_NPAD), vt_spec, tok(_NPAD), tok(_NPAD), vt_spec, tab_spec, tab_spec],
        out_shape=[act, act, vt, act, act, vt, tab, tab],
        scratch_shapes=[pltpu.VMEM((8, LANES), F32), pltpu.VMEM((8, LANES), F32)],
        compiler_params=pltpu.CompilerParams(
            dimension_semantics=("arbitrary", "arbitrary"), vmem_limit_bytes=VMEM_LIMIT),
        name="cd_norm_proj",
    )(x, g, w1, qn, wuq, kvn, wk, wv, bf, cos_t, sin_t, selq, selk, oneq, onek, hsel)


def _mask_pair_heads(q_ref, q2_ref):
    tq = q_ref.shape[1]
    lane = lax.broadcasted_iota(jnp.int32, (tq, LANES), 1)
    for h in range(q2_ref.shape[0]):
        q = q_ref[0, :, (h // 2) * LANES:(h // 2 + 1) * LANES]
        q2_ref[h] = jnp.where((lane >= HEAD_DIM) == bool(h % 2), q, jnp.zeros_like(q))


def _store_pair_heads(o_ref, outs):
    tq = outs[0].shape[1]
    lane = lax.broadcasted_iota(jnp.int32, (tq, LANES), 1)
    for pr in range(len(outs) // 2):
        o_ref[0, :, pr * LANES:(pr + 1) * LANES] = jnp.where(
            lane < HEAD_DIM, outs[2 * pr].T, outs[2 * pr + 1].T).astype(o_ref.dtype)


def _softmax_tile_update(ss, vts, m_ref, acc_ref):
    heads = range(len(ss))
    m_old = [m_ref[h] for h in heads]
    acc_old = [acc_ref[h] for h in heads]
    m_new = [functools.reduce(jnp.maximum, [jnp.max(s, axis=0, keepdims=True) for s in ss[h]], m_old[h])
             for h in heads]
    alpha = [jnp.exp2(m_old[h] - m_new[h]) for h in heads]
    ps = [[jnp.exp2(s - m_new[h]).astype(BF16) for s in ss[h]] for h in heads]
    ones = jnp.ones((SUM_ROWS, ss[0][0].shape[0]), BF16)
    pvs = [sum(jnp.dot(jnp.concatenate([vt, ones], axis=0), p, preferred_element_type=F32)
               for vt, p in zip(vts[h], ps[h])) for h in heads]
    for h in heads:
        m_ref[h] = m_new[h]
        acc_ref[h] = alpha[h] * acc_old[h] + pvs[h]


def _normalised(acc_ref, h):
    return acc_ref[h, 0:LANES, :] / acc_ref[h, LANES:LANES + 1, :]


def _chunk_attn_kernel(q_ref, k_ref, vt_ref, bias_ref, o_ref, q2_ref, m_ref, acc_ref):
    qi = pl.program_id(2)
    tq = q_ref.shape[1]
    heads = range(acc_ref.shape[0])
    _mask_pair_heads(q_ref, q2_ref)
    m_ref[...] = jnp.full_like(m_ref, NEG_BIG)
    acc_ref[...] = jnp.zeros_like(acc_ref)

    def run(js):
        kstarts = [pl.multiple_of((qi - (N_WIN_A - 1) + j) * tq, tq) for j in js]
        ss = [[lax.dot_general(k_ref[0, pl.ds(ks, tq), (h // 2) * LANES:(h // 2 + 1) * LANES], q2_ref[h],
                               NT_DIMS, preferred_element_type=F32) + bias_ref[h, j] for j, ks in zip(js, kstarts)]
              for h in heads]
        vts = [[vt_ref[0, h // 2, :, pl.ds(ks, tq)] for ks in kstarts] for h in heads]
        _softmax_tile_update(ss, vts, m_ref, acc_ref)
        _store_pair_heads(o_ref, [_normalised(acc_ref, h) for h in heads])

    first = jnp.maximum(N_WIN_A - 1 - qi, 0)
    for f in range(N_WIN_A):
        pl.when(first == f)(functools.partial(run, list(range(f, N_WIN_A))))


def _chunk_attn(proj, vt, bias):
    b, s, _ = proj.shape
    nh = HEADS_PER_STEP
    width = nh * HEAD_DIM
    k0 = N_HEADS * HEAD_DIM // width
    resident = dict(pipeline_mode=pl.Buffered(1))
    return pl.pallas_call(
        _chunk_attn_kernel,
        grid=(b, N_HEADS // nh, s // TK),
        in_specs=[
            pl.BlockSpec((1, TK, width), lambda bi, hg, qi: (bi, qi, hg)),
            pl.BlockSpec((1, s, width), lambda bi, hg, qi: (bi, 0, k0 + hg), **resident),
            pl.BlockSpec((1, nh // 2, LANES, s), lambda bi, hg, qi: (bi, hg, 0, 0), **resident),
            pl.BlockSpec((nh, N_WIN_A, TK, TK), lambda bi, hg, qi: (hg, 0, 0, 0), **resident),
        ],
        out_specs=pl.BlockSpec((1, TK, width), lambda bi, hg, qi: (bi, qi, hg)),
        out_shape=jax.ShapeDtypeStruct((b, s, N_HEADS * HEAD_DIM), BF16),
        scratch_shapes=[pltpu.VMEM((nh, TK, LANES), BF16), pltpu.VMEM((nh, 1, TK), F32),
                        pltpu.VMEM((nh, LANES + SUM_ROWS, TK), F32)],
        compiler_params=pltpu.CompilerParams(
            dimension_semantics=("arbitrary", "arbitrary", "arbitrary"), vmem_limit_bytes=VMEM_LIMIT),
        name="chunk_attn",
    )(proj, proj, vt, bias)


def _chunk_bias_tiles(rel_bias):
    h = rel_bias.shape[0]
    nq = TK
    nk = N_WIN_A * TK
    period = 1024
    assert nq + nk <= period + 1
    u = np.arange(period)
    signed = np.where(u < nk, u, u - period)
    idx = np.clip(LEFT_CHUNKS * CHUNK - signed, -MAX_REL, MAX_REL) + MAX_REL
    v = rel_bias.astype(F32)[:, idx] * LOG2E
    toep = jnp.tile(v, (1, nq))[:, :nq * (period - 1)].reshape(h, nq, period - 1)[:, :, :nk]
    r = np.arange(nq)[:, None]
    off = np.arange(nk)[None, :] - CHUNK * (r // CHUNK)
    in_band = (off >= 0) & (off < BAND)
    bias = jnp.where(in_band[None], toep, NEG_BIG)
    return bias.reshape(h, nq, N_WIN_A, TK).transpose(0, 2, 3, 1)


def _stick_kernel(q_ref, k_ref, vt_ref, tri_ref, o_ref, q2_ref, c_ref, acc_ref):
    qi = pl.program_id(2)
    tq = q_ref.shape[1]
    tk = tri_ref.shape[0]
    heads = range(acc_ref.shape[0])
    pair = lambda h: slice((h // 2) * LANES, (h // 2 + 1) * LANES)
    _mask_pair_heads(q_ref, q2_ref)
    c_ref[...] = jnp.zeros_like(c_ref)
    acc_ref[...] = jnp.zeros_like(acc_ref)
    sign_bit = jnp.uint32(0x80000000)

    def tiles(js, masked):
        nt = range(len(js))
        kstarts = [pl.multiple_of(j * tk, tk) for j in js]
        zs = [[lax.dot_general(k_ref[0, pl.ds(kstarts[t], tk), pair(h)], q2_ref[h], NT_DIMS,
                               preferred_element_type=F32) for t in nt] for h in heads]
        c_old = [c_ref[h] for h in heads]
        acc_old = [acc_ref[h] for h in heads]

        def neg_log_keep(z, t):
            neg_abs = lax.bitcast_convert_type(lax.bitcast_convert_type(z, jnp.uint32) | sign_bit, F32)
            nlk = jnp.maximum(z, 0.0) + jnp.log(1.0 + jnp.exp2(neg_abs)) * LOG2E
            return jnp.where(masks[t], nlk, 0.0) if masked[t] else nlk

        masks = [None] * len(js)
        for t in nt:
            if masked[t]:
                key = kstarts[t] + lax.broadcasted_iota(jnp.int32, (tk, tq), 0)
                qry = qi * tq + lax.broadcasted_iota(jnp.int32, (tk, tq), 1)
                masks[t] = key < qry
        nlk = [[neg_log_keep(zs[h][t], t) for t in nt] for h in heads]
        rs = [[jnp.dot(tri_ref[...], nlk[h][t].astype(BF16), preferred_element_type=F32) for t in nt] for h in heads]
        pvs = []
        c_new = []
        for h in heads:
            c = c_old[h]
            pv = None
            for t in nt:
                w = jnp.exp2(zs[h][t] - rs[h][t] - c)
                if masked[t]:
                    w = jnp.where(masks[t], w, 0.0)
                d = jnp.dot(vt_ref[0, h // 2, :, pl.ds(kstarts[t], tk)], w.astype(BF16),
                            preferred_element_type=F32)
                pv = d if pv is None else pv + d
                c = c + jnp.sum(nlk[h][t], axis=0, keepdims=True)
            pvs.append(pv)
            c_new.append(c)
        for h in heads:
            c_ref[h] = c_new[h]
            acc_ref[h] = acc_old[h] + pvs[h]

    assert tq == tk
    pl.when(qi % 2 == 1)(lambda: tiles([qi, qi - 1], [True, False]))
    pl.when(qi % 2 == 0)(lambda: tiles([qi], [True]))
    top = qi - 1 - qi % 2

    def more(it):
        return jnp.logical_and(it < qi // 2, jnp.min(c_ref[...]) < STICK_UNDERFLOW_LOG2)

    def body(it):
        tiles([top - 2 * it, top - 2 * it - 1], [False, False])
        return it + 1

    lax.while_loop(more, body, 0)
    _store_pair_heads(o_ref, [acc_ref[h] for h in heads])


def _stick_attn(proj, vt, tri):
    b, s, _ = proj.shape
    nh = HEADS_PER_STEP
    width = nh * HEAD_DIM
    q0 = 3 * N_HEADS * HEAD_DIM // width
    k0 = 4 * N_HEADS * HEAD_DIM // width
    resident = dict(pipeline_mode=pl.Buffered(1))
    return pl.pallas_call(
        _stick_kernel,
        grid=(b, N_HEADS // nh, s // TQ),
        in_specs=[
            pl.BlockSpec((1, TQ, width), lambda bi, hg, qi: (bi, qi, q0 + hg)),
            pl.BlockSpec((1, s, width), lambda bi, hg, qi: (bi, 0, k0 + hg), **resident),
            pl.BlockSpec((1, nh // 2, LANES, s), lambda bi, hg, qi: (bi, hg, 0, 0), **resident),
            pl.BlockSpec((TK, TK), lambda bi, hg, qi: (0, 0)),
        ],
        out_specs=pl.BlockSpec((1, TQ, width), lambda bi, hg, qi: (bi, qi, hg)),
        out_shape=jax.ShapeDtypeStruct((b, s, N_HEADS * HEAD_DIM), BF16),
        scratch_shapes=[pltpu.VMEM((nh, TQ, LANES), BF16), pltpu.VMEM((nh, 1, TQ), F32),
                        pltpu.VMEM((nh, LANES, TQ), F32)],
        compiler_params=pltpu.CompilerParams(
            dimension_semantics=("arbitrary", "arbitrary", "arbitrary"), vmem_limit_bytes=VMEM_LIMIT),
        name="stick_attn",
    )(proj, proj, vt, tri)


def _softmax_attn_kernel(*refs, chunk_mask, decay_skip):
    if decay_skip:
        q_ref, k_ref, vt_ref, knorm_ref, fend_ref, o_ref, m_ref, acc_ref = refs
    else:
        q_ref, k_ref, vt_ref, o_ref, m_ref, acc_ref = refs
    bi = pl.program_id(0)
    qi = pl.program_id(2)
    tq = q_ref.shape[1]
    tk = TK
    heads = range(q_ref.shape[2] // LANES)
    sl = lambda h: slice(h * LANES, (h + 1) * LANES)
    m_ref[...] = jnp.full_like(m_ref, NEG_BIG)
    acc_ref[...] = jnp.zeros_like(acc_ref)

    def tiles(js, masked):
        kstarts = [pl.multiple_of(j * tk, tk) for j in js]

        def scores(h, t):
            s = lax.dot_general(k_ref[0, pl.ds(kstarts[t], tk), sl(h)], q_ref[0, :, sl(h)], NT_DIMS,
                                preferred_element_type=F32)
            if masked[t]:
                key = kstarts[t] + lax.broadcasted_iota(jnp.int32, (tk, tq), 0)
                qry = qi * tq + lax.broadcasted_iota(jnp.int32, (tk, tq), 1)
                s = jnp.where((key // CHUNK <= qry // CHUNK) if chunk_mask else (key <= qry), s, NEG_BIG)
            return s

        ss = [[scores(h, t) for t in range(len(js))] for h in heads]
        vts = [[vt_ref[0, h // 2, :, pl.ds(ks, tk)] for ks in kstarts] for h in heads]
        _softmax_tile_update(ss, vts, m_ref, acc_ref)

    assert tq == tk
    if not decay_skip:
        def body(i, carry):
            tiles([2 * i, 2 * i + 1], [False, False])
            return carry

        lax.fori_loop(0, qi // 2, body, 0)
        pl.when(qi % 2 == 1)(lambda: tiles([qi - 1, qi], [False, True]))
        pl.when(qi % 2 == 0)(lambda: tiles([qi], [True]))
    else:
        pl.when(qi % 2 == 1)(lambda: tiles([qi - 1, qi], [False, True]))
        pl.when(qi % 2 == 0)(lambda: tiles([qi], [True]))
        top = qi - 1 - qi % 2
        lane8 = lax.broadcasted_iota(jnp.int32, (8, LANES), 1)
        dims = jnp.where(lane8 < HEAD_DIM, 1.0, 0.0).astype(BF16)
        pieces = jnp.where((lane8 >= F_LANE0) & (lane8 < F_LANE0 + N_PIECES), 1.0, 0.0).astype(BF16)
        qnorm, fq = [], []
        for h in heads:
            q = q_ref[0, :, sl(h)]
            q32 = q.astype(F32)
            n2 = lax.dot_general(dims, (q32 * q32).astype(BF16), NT_DIMS, preferred_element_type=F32)[0:1]
            qnorm.append(jnp.sqrt(n2) * NORM_MARGIN)
            fq.append(lax.dot_general(pieces, q, NT_DIMS, preferred_element_type=F32)[0:1])

        def more(it):
            j = jnp.maximum(top - 2 * it, 0)
            gap = [qnorm[h] * knorm_ref[bi, j, h] + fq[h] - fend_ref[bi, j, h] - m_ref[h] for h in heads]
            reach = jnp.max(functools.reduce(jnp.maximum, gap))
            return jnp.logical_and(it < qi // 2, reach > -SOFTMAX_UNDERFLOW_LOG2)

        def body(it):
            tiles([top - 2 * it, top - 2 * it - 1], [False, False])
            return it + 1

        lax.while_loop(more, body, 0)
    _store_pair_heads(o_ref, [_normalised(acc_ref, h) for h in heads])


def _softmax_attn(q, k, vt, chunk_mask, name, skip_tables=None):
    b, s, _ = q.shape
    nh = HEADS_PER_STEP
    resident = dict(pipeline_mode=pl.Buffered(1))
    in_specs = [
        pl.BlockSpec((1, TQ, nh * LANES), lambda bi, hg, qi: (bi, qi, hg)),
        pl.BlockSpec((1, s, nh * LANES), lambda bi, hg, qi: (bi, 0, hg), **resident),
        pl.BlockSpec((1, nh // 2, LANES, s), lambda bi, hg, qi: (bi, hg, 0, 0), **resident),
    ]
    args = [q, k, vt]
    if skip_tables is not None:
        assert nh == N_HEADS
        in_specs += [pl.BlockSpec(memory_space=pltpu.SMEM)] * 2
        args += list(skip_tables)
    return pl.pallas_call(
        functools.partial(_softmax_attn_kernel, chunk_mask=chunk_mask, decay_skip=skip_tables is not None),
        grid=(b, N_HEADS // nh, s // TQ),
        in_specs=in_specs,
        out_specs=pl.BlockSpec((1, TQ, nh * HEAD_DIM), lambda bi, hg, qi: (bi, qi, hg)),
        out_shape=jax.ShapeDtypeStruct((b, s, N_HEADS * HEAD_DIM), BF16),
        scratch_shapes=[pltpu.VMEM((nh, 1, TQ), F32), pltpu.VMEM((nh, LANES + SUM_ROWS, TQ), F32)],
        compiler_params=pltpu.CompilerParams(
            dimension_semantics=("arbitrary", "arbitrary", "arbitrary"), vmem_limit_bytes=VMEM_LIMIT),
        name=name,
    )(*args)


def _out_ffn_kernel(*refs, tiles_per_seq, final_norm):
    if final_norm:
        (x_ref, o1_ref, o2_ref, wo_ref, g_ref, wg_ref, wu_ref, cw_ref, cb_ref, wd_ref, fg_ref,
         out_ref, x1_ref, h_ref, acc_ref, gbuf_ref, tail_ref) = refs
    else:
        (x_ref, o1_ref, o2_ref, wo_ref, g_ref, wg_ref, wu_ref, cw_ref, cb_ref, wd_ref,
         out_ref, x1_ref, h_ref, acc_ref, gbuf_ref, tail_ref) = refs
    i = pl.program_id(0)
    f = pl.program_id(1)
    nf = pl.num_programs(1)
    tm = x_ref.shape[0]
    half = o1_ref.shape[1]

    @pl.when(f == 0)
    def _():
        x1 = (x_ref[...]
              + jnp.dot(o1_ref[...], wo_ref[0:half, :], preferred_element_type=F32)
              + jnp.dot(o2_ref[...], wo_ref[half:2 * half, :], preferred_element_type=F32))
        x1_ref[...] = x1
        h_ref[...] = _rms(x1, g_ref[...]).astype(BF16)
        acc_ref[...] = jnp.zeros_like(acc_ref)

    h = h_ref[...]
    g = jnp.dot(h, wg_ref[...], preferred_element_type=F32)
    u = jnp.dot(h, wu_ref[...], preferred_element_type=F32)

    prev = jnp.where(i % tiles_per_seq == 0, 0.0, tail_ref[f])
    gbuf_ref[0:8, :] = prev
    gbuf_ref[8:8 + tm, :] = g
    tail_ref[f] = g[tm - 8:tm, :]
    gm1 = gbuf_ref[7:7 + tm, :]
    gm2 = gbuf_ref[6:6 + tm, :]
    cw = cw_ref[...]
    gc = cw[0:1, :] * gm2 + cw[1:2, :] * gm1 + cw[2:3, :] * g + cb_ref[...]
    y = (gc / (1.0 + jnp.exp(-gc)) * u).astype(BF16)
    acc_ref[...] += jnp.dot(y, wd_ref[...], preferred_element_type=F32)

    @pl.when(f == nf - 1)
    def _():
        res = x1_ref[...] + acc_ref[...]
        if final_norm:
            res = _rms(res, fg_ref[...])
        out_ref[...] = res


def _out_ffn(x2d, o1, o2, wo, g, wg, wu, cw, cb, wd, final_g, seq_len):
    n, d = x2d.shape
    half = o1.shape[1]
    dff = wg.shape[1]
    tm, tf = TM_FFN, TF_FFN
    nf = dff // tf
    final_norm = final_g is not None
    once = dict(pipeline_mode=pl.Buffered(1))
    per_f = once if nf == 1 else {}
    in_specs = [
        pl.BlockSpec((tm, d), lambda i, f: (i, 0)),
        pl.BlockSpec((tm, half), lambda i, f: (i, 0)),
        pl.BlockSpec((tm, half), lambda i, f: (i, 0)),
        pl.BlockSpec((d, d), lambda i, f: (0, 0), **once),
        pl.BlockSpec((1, d), lambda i, f: (0, 0)),
        pl.BlockSpec((d, tf), lambda i, f: (0, f), **per_f),
        pl.BlockSpec((d, tf), lambda i, f: (0, f), **per_f),
        pl.BlockSpec((3, tf), lambda i, f: (0, f)),
        pl.BlockSpec((1, tf), lambda i, f: (0, f)),
        pl.BlockSpec((tf, d), lambda i, f: (f, 0), **per_f),
    ]
    args = [x2d, o1, o2, wo, g, wg, wu, cw, cb, wd]
    if final_norm:
        in_specs.append(pl.BlockSpec((1, d), lambda i, f: (0, 0)))
        args.append(final_g)
    return pl.pallas_call(
        functools.partial(_out_ffn_kernel, tiles_per_seq=seq_len // tm, final_norm=final_norm),
        grid=(n // tm, nf),
        in_specs=in_specs,
        out_specs=pl.BlockSpec((tm, d), lambda i, f: (i, 0)),
        out_shape=jax.ShapeDtypeStruct((n, d), F32),
        scratch_shapes=[
            pltpu.VMEM((tm, d), F32),
            pltpu.VMEM((tm, d), BF16),
            pltpu.VMEM((tm, d), F32),
            pltpu.VMEM((tm + 8, tf), F32),
            pltpu.VMEM((nf, 8, tf), F32),
        ],
        compiler_params=pltpu.CompilerParams(
            dimension_semantics=("arbitrary", "arbitrary"), vmem_limit_bytes=VMEM_LIMIT),
        name="out_ffn_final" if final_norm else "out_ffn",
    )(*args)


def _rope_tables(seq_len):
    half = QK_ROPE // 2
    inv = ROPE_THETA ** (-jnp.arange(half, dtype=F32) / half)
    ang = jnp.arange(seq_len, dtype=F32)[:, None] * inv[None, :]
    cos, sin = jnp.cos(ang), jnp.sin(ang)
    ones = jnp.ones((seq_len, QK_NOPE), F32)
    zeros = jnp.zeros((seq_len, QK_NOPE), F32)
    pad1 = jnp.ones((seq_len, LANES - QK_NOPE - QK_ROPE), F32)
    pad0 = jnp.zeros((seq_len, LANES - QK_NOPE - QK_ROPE), F32)
    return (jnp.concatenate([ones, cos, cos, pad1], axis=1),
            jnp.concatenate([zeros, -sin, sin, pad0], axis=1))


def _pad_heads(w, width):
    rows = w.shape[0]
    return jnp.zeros((rows, N_HEADS, LANES), w.dtype).at[:, :, :width].set(
        w.reshape(rows, N_HEADS, width)).reshape(rows, N_HEADS * LANES)


def _cd_weights(w_in, w_uq, w_ukv, b_f):
    d = w_in.shape[0]
    nd = N_HEADS * HEAD_DIM
    o = Q_RANK + KV_RANK + QK_ROPE
    c_q, c_kv, k_rope = w_in[:, :Q_RANK], w_in[:, Q_RANK:Q_RANK + KV_RANK], w_in[:, Q_RANK + KV_RANK:o]
    q_d, k_d, v_d, f_logit = (w_in[:, o:o + nd], w_in[:, o + nd:o + 2 * nd], w_in[:, o + 2 * nd:o + 3 * nd],
                              w_in[:, o + 3 * nd:])
    kr_blk = jnp.zeros((d, LANES), w_in.dtype).at[:, QK_NOPE:QK_NOPE + QK_ROPE].set(k_rope)
    f_blk = jnp.zeros((d, LANES), w_in.dtype).at[:, :N_HEADS].set(f_logit)
    w1 = jnp.concatenate([c_q, c_kv, kr_blk, _pad_heads(q_d, HEAD_DIM), _pad_heads(k_d, HEAD_DIM), v_d, f_blk],
                         axis=1).astype(BF16)
    wuq = _pad_heads(w_uq, QK_NOPE + QK_ROPE).astype(BF16)
    ukv = w_ukv.reshape(KV_RANK, N_HEADS, QK_NOPE + HEAD_DIM)
    wk = _pad_heads(ukv[:, :, :QK_NOPE].reshape(KV_RANK, N_HEADS * QK_NOPE), QK_NOPE).astype(BF16)
    wv = ukv[:, :, QK_NOPE:].reshape(KV_RANK, N_HEADS * HEAD_DIM).astype(BF16)
    bf = jnp.zeros((1, LANES), F32).at[0, :N_HEADS].set(b_f.astype(F32))
    return w1, wuq, wk, wv, bf


def _forget_selectors():
    selq = np.zeros((LANES, _NPAD), np.float32)
    selk = np.zeros((LANES, _NPAD), np.float32)
    oneq = np.zeros((1, _NPAD), np.float32)
    onek = np.zeros((1, _NPAD), np.float32)
    hsel = np.zeros((_NPAD, LANES), np.float32)
    for h in range(N_HEADS):
        hsel[h * LANES:(h + 1) * LANES, h] = 1.0
        for j in range(N_PIECES):
            selq[j * N_HEADS + h, h * LANES + F_LANE0 + j] = 1.0
            onek[0, h * LANES + F_LANE0 + j] = 1.0
            selk[j * N_HEADS + h, h * LANES + ONE_LANE0 + j] = -1.0
            oneq[0, h * LANES + ONE_LANE0 + j] = 1.0
    return (jnp.asarray(selq, BF16), jnp.asarray(selk, BF16), jnp.asarray(oneq), jnp.asarray(onek),
            jnp.asarray(hsel, BF16))


def kernel(x, ab_norm, ab_w_in, ab_rel_bias, ab_w_o, cd_norm, cd_w_in, cd_q_norm, cd_w_uq, cd_kv_norm, cd_w_ukv,
           cd_b_f, cd_w_o, ffn_norm, ffn_w_gate, ffn_w_up, ffn_conv_w, ffn_conv_b, ffn_w_down, final_norm):
    b, s, d = x.shape
    n = b * s

    qscale = jnp.full((N_HEADS * HEAD_DIM,), HEAD_DIM ** -0.5 * LOG2E, F32)
    one = jnp.ones((2 * N_HEADS * HEAD_DIM,), F32)
    colscale = jnp.concatenate([qscale, one, qscale, one])[None, :]
    proj, vat, vbt = _ab_proj(x, ab_norm[0][None, :], ab_w_in[0].astype(BF16), colscale)
    oa = _chunk_attn(proj, vat, _chunk_bias_tiles(ab_rel_bias[0]))
    tri = jnp.asarray(np.triu(np.ones((TK, TK), np.float32)), BF16)
    ob = _stick_attn(proj, vbt, tri)
    x2d = _out_ffn(x.reshape(n, d), oa.reshape(n, -1), ob.reshape(n, -1), ab_w_o[0].astype(BF16),
                   ffn_norm[0][None, :], ffn_w_gate[0].astype(BF16), ffn_w_up[0].astype(BF16), ffn_conv_w[0],
                   ffn_conv_b[0][None, :], ffn_w_down[0].astype(BF16), None, s)

    w1, wuq, wk, wv, bf = _cd_weights(cd_w_in[0], cd_w_uq[0], cd_w_ukv[0], cd_b_f[0])
    cos_t, sin_t = _rope_tables(s)
    qc, kc, vct, qd, kd, vdt, knorm, fend = _cd_proj(
        x2d.reshape(b, s, d), cd_norm[0][None, :], w1, cd_q_norm[0][None, :], wuq, cd_kv_norm[0][None, :], wk, wv, bf,
        cos_t, sin_t, *_forget_selectors())
    oc = _softmax_attn(qc, kc, vct, True, "mla_attn")
    od = _softmax_attn(qd, kd, vdt, False, "fox_attn", (knorm[:, :, 0, :N_HEADS], fend[:, :, 0, :N_HEADS]))
    out = _out_ffn(x2d, oc.reshape(n, -1), od.reshape(n, -1), cd_w_o[0].astype(BF16), ffn_norm[1][None, :],
                   ffn_w_gate[1].astype(BF16), ffn_w_up[1].astype(BF16), ffn_conv_w[1], ffn_conv_b[1][None, :],
                   ffn_w_down[1].astype(BF16), final_norm[None, :], s)
    return out.reshape(b, s, d)
```

```python
import functools
import math

import numpy as np
import jax
import jax.numpy as jnp
from jax import lax
from jax.experimental import pallas as pl
from jax.experimental.pallas import tpu as pltpu

F32 = jnp.float32
BF16 = jnp.bfloat16

D_MODEL = 1024
HEAD_DIM = 64
CHUNK = 64
LEFT_CHUNKS = 8
BAND = (LEFT_CHUNKS + 1) * CHUNK
MAX_REL = 128
N_HEADS = 8
N_PAIRS = N_HEADS // 2
QK_NOPE = 64
QK_ROPE = 32
Q_RANK = 384
KV_RANK = 256
ROPE_THETA = 10000.0
D_FF = 2816
RMS_EPS = 1e-6

LANES = 128
SUBLANES = 8
LOG2E = math.log2(math.e)
NEG_BIG = -1e30
STICK_UNDERFLOW_LOG2 = 200.0
SOFTMAX_UNDERFLOW_LOG2 = 160.0
NORM_MARGIN = 1.01
VMEM_LIMIT = 52 * 1024 * 1024

TM_PROJ = 256
TM_FFN = 256
TF_FFN = 2816
TK = 256
TQ = 256
N_WIN_A = LEFT_CHUNKS * CHUNK // TK + 1
HEADS_PER_STEP = 8
TILES_PER_STEP = 2
SUM_ROWS = 16

N_PIECES = 3
F_LANE0 = HEAD_DIM
ONE_LANE0 = HEAD_DIM + N_PIECES

NT_DIMS = (((1,), (1,)), ((), ()))


def _rms(x, g):
    ms = jnp.mean(x * x, axis=-1, keepdims=True)
    return x * lax.rsqrt(ms + RMS_EPS) * g


def _store_pairs_transposed(vt_ref, v):
    for p in range(N_PAIRS):
        vt_ref[0, p] = v[:, p * LANES:(p + 1) * LANES].T.astype(vt_ref.dtype)


def _ab_proj_kernel(x_ref, g_ref, w_ref, cs_ref, o_ref, vat_ref, vbt_ref):
    h = _rms(x_ref[0], g_ref[...]).astype(BF16)
    p = jnp.dot(h, w_ref[...], preferred_element_type=F32) * cs_ref[...]
    o_ref[0] = p.astype(o_ref.dtype)
    nv = N_HEADS * HEAD_DIM
    _store_pairs_transposed(vat_ref, p[:, 2 * nv:3 * nv])
    _store_pairs_transposed(vbt_ref, p[:, 5 * nv:6 * nv])


def _ab_proj(x, g, w, colscale):
    b, s, d = x.shape
    nc = w.shape[1]
    tm = TM_PROJ
    vt_spec = pl.BlockSpec((1, N_PAIRS, LANES, tm), lambda bi, ti: (bi, 0, 0, ti))
    vt_shape = jax.ShapeDtypeStruct((b, N_PAIRS, LANES, s), BF16)
    return pl.pallas_call(
        _ab_proj_kernel,
        grid=(b, s // tm),
        in_specs=[
            pl.BlockSpec((1, tm, d), lambda bi, ti: (bi, ti, 0)),
            pl.BlockSpec((1, d), lambda bi, ti: (0, 0)),
            pl.BlockSpec((d, nc), lambda bi, ti: (0, 0)),
            pl.BlockSpec((1, nc), lambda bi, ti: (0, 0)),
        ],
        out_specs=[pl.BlockSpec((1, tm, nc), lambda bi, ti: (bi, ti, 0)), vt_spec, vt_spec],
        out_shape=[jax.ShapeDtypeStruct((b, s, nc), BF16), vt_shape, vt_shape],
        compiler_params=pltpu.CompilerParams(
            dimension_semantics=("arbitrary", "arbitrary"), vmem_limit_bytes=VMEM_LIMIT),
        name="ab_norm_proj",
    )(x, g, w, colscale)


_NPAD = N_HEADS * LANES
_C_Q0 = 0
_C_KV0 = _C_Q0 + Q_RANK
_C_KR0 = _C_KV0 + KV_RANK
_C_QD0 = _C_KR0 + LANES
_C_KD0 = _C_QD0 + _NPAD
_C_VD0 = _C_KD0 + _NPAD
_C_F0 = _C_VD0 + N_HEADS * HEAD_DIM
_C_END = _C_F0 + LANES


def _cd_proj_kernel(x_ref, g_ref, w1_ref, qn_ref, wuq_ref, kvn_ref, wk_ref, wv_ref, bf_ref, cos_ref, sin_ref,
                    selq_ref, selk_ref, oneq_ref, onek_ref, hsel_ref,
                    qc_ref, kc_ref, vct_ref, qd_ref, kd_ref, vdt_ref, knorm_ref, fend_ref, carry_ref, kcarry_ref):
    t = pl.program_id(1)
    tm = x_ref.shape[1]
    h = _rms(x_ref[0], g_ref[...]).astype(BF16)
    p = jnp.dot(h, w1_ref[...], preferred_element_type=F32)

    cq = _rms(p[:, _C_Q0:_C_KV0], qn_ref[...]).astype(BF16)
    ckv = _rms(p[:, _C_KV0:_C_KR0], kvn_ref[...]).astype(BF16)

    cosb = cos_ref[...]
    sinb = sin_ref[...]
    lane = lax.broadcasted_iota(jnp.int32, (tm, LANES), 1)

    def rope(xb):
        partner = jnp.where(lane < QK_NOPE + QK_ROPE // 2, pltpu.roll(xb, LANES - QK_ROPE // 2, 1),
                            pltpu.roll(xb, QK_ROPE // 2, 1))
        return xb * cosb + partner * sinb

    qc = jnp.dot(cq, wuq_ref[...], preferred_element_type=F32) * ((QK_NOPE + QK_ROPE) ** -0.5 * LOG2E)
    kc = jnp.dot(ckv, wk_ref[...], preferred_element_type=F32)
    kr = rope(p[:, _C_KR0:_C_QD0])
    for hh in range(N_HEADS):
        sl = slice(hh * LANES, (hh + 1) * LANES)
        qc_ref[0, :, sl] = rope(qc[:, sl]).astype(BF16)
        kc_ref[0, :, sl] = (kc[:, sl] + kr).astype(BF16)
    _store_pairs_transposed(vct_ref, jnp.dot(ckv, wv_ref[...], preferred_element_type=F32))
    _store_pairs_transposed(vdt_ref, p[:, _C_VD0:_C_F0])

    fl = p[:, _C_F0:_C_END] + bf_ref[...]
    y = jnp.minimum(fl, 0.0) - jnp.log(1.0 + jnp.exp(-jnp.abs(fl)))
    row = lax.broadcasted_iota(jnp.int32, (tm, LANES), 0)
    sh = 1
    while sh < tm:
        y = y + jnp.where(row >= sh, pltpu.roll(y, sh, 0), 0.0)
        sh *= 2

    @pl.when(t == 0)
    def _():
        carry_ref[...] = jnp.zeros_like(carry_ref)
        kcarry_ref[...] = jnp.zeros_like(kcarry_ref)

    y = y + carry_ref[0:1, :]
    carry_ref[...] = jnp.broadcast_to(y[tm - 1:tm, :], carry_ref.shape)
    f2 = y * LOG2E

    hi = f2.astype(BF16).astype(F32)
    r1 = f2 - hi
    mid = r1.astype(BF16).astype(F32)
    lo = r1 - mid
    fp = jnp.where(lane < N_HEADS, hi, jnp.where(lane < 2 * N_HEADS, pltpu.roll(mid, N_HEADS, 1),
                                                 pltpu.roll(lo, 2 * N_HEADS, 1))).astype(BF16)
    qd = p[:, _C_QD0:_C_KD0] * (HEAD_DIM ** -0.5 * LOG2E)
    kd = p[:, _C_KD0:_C_VD0]
    qd_ref[0] = (qd + jnp.dot(fp, selq_ref[...], preferred_element_type=F32) + oneq_ref[...]).astype(BF16)
    kd_ref[0] = (kd + jnp.dot(fp, selk_ref[...], preferred_element_type=F32) + onek_ref[...]).astype(BF16)

    n2 = jnp.dot((kd * kd).astype(BF16), hsel_ref[...], preferred_element_type=F32)
    kmax = jnp.maximum(kcarry_ref[0:1, :], jnp.sqrt(jnp.max(n2, axis=0, keepdims=True)) * NORM_MARGIN)
    kcarry_ref[...] = jnp.broadcast_to(kmax, kcarry_ref.shape)
    knorm_ref[0, 0] = jnp.broadcast_to(kmax, knorm_ref.shape[2:])
    fend_ref[0, 0] = jnp.broadcast_to(f2[tm - 1:tm, :], fend_ref.shape[2:])


def _cd_proj(x, g, w1, qn, wuq, kvn, wk, wv, bf, cos_t, sin_t, selq, selk, oneq, onek, hsel):
    b, s, d = x.shape
    tm = TM_PROJ
    assert tm == TK
    const = lambda a: pl.BlockSpec(a.shape, lambda bi, ti: (0,) * a.ndim)
    tok = lambda nc: pl.BlockSpec((1, tm, nc), lambda bi, ti: (bi, ti, 0))
    vt_spec = pl.BlockSpec((1, N_PAIRS, LANES, tm), lambda bi, ti: (bi, 0, 0, ti))
    tab_spec = pl.BlockSpec((1, 1, SUBLANES, LANES), lambda bi, ti: (bi, ti, 0, 0))
    act = jax.ShapeDtypeStruct((b, s, _NPAD), BF16)
    vt = jax.ShapeDtypeStruct((b, N_PAIRS, LANES, s), BF16)
    tab = jax.ShapeDtypeStruct((b, s // tm, SUBLANES, LANES), F32)
    return pl.pallas_call(
        _cd_proj_kernel,
        grid=(b, s // tm),
        in_specs=[
            tok(d), const(g), const(w1), const(qn), const(wuq), const(kvn), const(wk), const(wv), const(bf),
            pl.BlockSpec((tm, LANES), lambda bi, ti: (ti, 0)),
            pl.BlockSpec((tm, LANES), lambda bi, ti: (ti, 0)),
            const(selq), const(selk), const(oneq), const(onek), const(hsel),
        ],
        out_specs=[tok(_NPAD), tok(_NPAD), vt_spec, tok(_NPAD), tok(_NPAD), vt_spec, tab_spec, tab_spec],
        out_shape=[act, act, vt, act, act, vt, tab, tab],
        scratch_shapes=[pltpu.VMEM((SUBLANES, LANES), F32), pltpu.VMEM((SUBLANES, LANES), F32)],
        compiler_params=pltpu.CompilerParams(
            dimension_semantics=("arbitrary", "arbitrary"), vmem_limit_bytes=VMEM_LIMIT),
        name="cd_norm_proj",
    )(x, g, w1, qn, wuq, kvn, wk, wv, bf, cos_t, sin_t, selq, selk, oneq, onek, hsel)


def _mask_pair_heads(q_ref, q2_ref):
    tq = q_ref.shape[1]
    lane = lax.broadcasted_iota(jnp.int32, (tq, LANES), 1)
    for h in range(q2_ref.shape[0]):
        q = q_ref[0, :, (h // 2) * LANES:(h // 2 + 1) * LANES]
        q2_ref[h] = jnp.where((lane >= HEAD_DIM) == bool(h % 2), q, jnp.zeros_like(q))


def _store_pair_heads(o_ref, outs):
    tq = outs[0].shape[1]
    lane = lax.broadcasted_iota(jnp.int32, (tq, LANES), 1)
    for pr in range(len(outs) // 2):
        o_ref[0, :, pr * LANES:(pr + 1) * LANES] = jnp.where(
            lane < HEAD_DIM, outs[2 * pr].T, outs[2 * pr + 1].T).astype(o_ref.dtype)


def _softmax_tile_update(ss, vts, m_ref, acc_ref):
    heads = range(len(ss))
    m_old = [m_ref[h] for h in heads]
    acc_old = [acc_ref[h] for h in heads]
    m_new = [functools.reduce(jnp.maximum, [jnp.max(s, axis=0, keepdims=True) for s in ss[h]], m_old[h])
             for h in heads]
    alpha = [jnp.exp2(m_old[h] - m_new[h]) for h in heads]
    ps = [[jnp.exp2(s - m_new[h]).astype(BF16) for s in ss[h]] for h in heads]
    ones = jnp.ones((SUM_ROWS, ss[0][0].shape[0]), BF16)
    pvs = [sum(jnp.dot(jnp.concatenate([vt, ones], axis=0), p, preferred_element_type=F32)
               for vt, p in zip(vts[h], ps[h])) for h in heads]
    for h in heads:
        m_ref[h] = m_new[h]
        acc_ref[h] = alpha[h] * acc_old[h] + pvs[h]


def _normalised(acc_ref, h):
    return acc_ref[h, 0:LANES, :] / acc_ref[h, LANES:LANES + 1, :]


def _chunk_attn_kernel(q_ref, k_ref, vt_ref, bias_ref, o_ref, q2_ref, m_ref, acc_ref):
    qi = pl.program_id(2)
    tq = q_ref.shape[1]
    heads = range(acc_ref.shape[0])
    _mask_pair_heads(q_ref, q2_ref)
    m_ref[...] = jnp.full_like(m_ref, NEG_BIG)
    acc_ref[...] = jnp.zeros_like(acc_ref)

    def run(js):
        kstarts = [pl.multiple_of((qi - (N_WIN_A - 1) + j) * tq, tq) for j in js]
        ss = [[lax.dot_general(k_ref[0, pl.ds(ks, tq), (h // 2) * LANES:(h // 2 + 1) * LANES], q2_ref[h],
                               NT_DIMS, preferred_element_type=F32) + bias_ref[h, j] for j, ks in zip(js, kstarts)]
              for h in heads]
        vts = [[vt_ref[0, h // 2, :, pl.ds(ks, tq)] for ks in kstarts] for h in heads]
        _softmax_tile_update(ss, vts, m_ref, acc_ref)
        _store_pair_heads(o_ref, [_normalised(acc_ref, h) for h in heads])

    first = jnp.maximum(N_WIN_A - 1 - qi, 0)
    for f in range(N_WIN_A):
        pl.when(first == f)(functools.partial(run, list(range(f, N_WIN_A))))


def _chunk_attn(proj, vt, bias):
    b, s, _ = proj.shape
    nh = HEADS_PER_STEP
    width = nh * HEAD_DIM
    k0 = N_HEADS * HEAD_DIM // width
    resident = dict(pipeline_mode=pl.Buffered(1))
    return pl.pallas_call(
        _chunk_attn_kernel,
        grid=(b, N_HEADS // nh, s // TK),
        in_specs=[
            pl.BlockSpec((1, TK, width), lambda bi, hg, qi: (bi, qi, hg)),
            pl.BlockSpec((1, s, width), lambda bi, hg, qi: (bi, 0, k0 + hg), **resident),
            pl.BlockSpec((1, nh // 2, LANES, s), lambda bi, hg, qi: (bi, hg, 0, 0), **resident),
            pl.BlockSpec((nh, N_WIN_A, TK, TK), lambda bi, hg, qi: (hg, 0, 0, 0), **resident),
        ],
        out_specs=pl.BlockSpec((1, TK, width), lambda bi, hg, qi: (bi, qi, hg)),
        out_shape=jax.ShapeDtypeStruct((b, s, N_HEADS * HEAD_DIM), BF16),
        scratch_shapes=[pltpu.VMEM((nh, TK, LANES), BF16), pltpu.VMEM((nh, 1, TK), F32),
                        pltpu.VMEM((nh, LANES + SUM_ROWS, TK), F32)],
        compiler_params=pltpu.CompilerParams(
            dimension_semantics=("arbitrary", "arbitrary", "arbitrary"), vmem_limit_bytes=VMEM_LIMIT),
        name="chunk_attn",
    )(proj, proj, vt, bias)


def _chunk_bias_tiles(rel_bias):
    h = rel_bias.shape[0]
    nq = TK
    nk = N_WIN_A * TK
    period = 1 << (nq + nk - 1).bit_length()
    u = np.arange(period)
    signed = np.where(u < nk, u, u - period)
    idx = np.clip(LEFT_CHUNKS * CHUNK - signed, -MAX_REL, MAX_REL) + MAX_REL
    v = rel_bias.astype(F32)[:, idx] * LOG2E
    toep = jnp.tile(v, (1, nq))[:, :nq * (period - 1)].reshape(h, nq, period - 1)[:, :, :nk]
    r = np.arange(nq)[:, None]
    off = np.arange(nk)[None, :] - CHUNK * (r // CHUNK)
    in_band = (off >= 0) & (off < BAND)
    bias = jnp.where(in_band[None], toep, NEG_BIG)
    return bias.reshape(h, nq, N_WIN_A, TK).transpose(0, 2, 3, 1)


def _stick_kernel(q_ref, k_ref, vt_ref, tri_ref, o_ref, q2_ref, c_ref, acc_ref):
    qi = pl.program_id(2)
    tq = q_ref.shape[1]
    tk = tri_ref.shape[0]
    heads = range(acc_ref.shape[0])
    pair = lambda h: slice((h // 2) * LANES, (h // 2 + 1) * LANES)
    _mask_pair_heads(q_ref, q2_ref)
    c_ref[...] = jnp.zeros_like(c_ref)
    acc_ref[...] = jnp.zeros_like(acc_ref)
    sign_bit = jnp.uint32(0x80000000)

    def tiles(js, masked):
        nt = range(len(js))
        kstarts = [pl.multiple_of(j * tk, tk) for j in js]
        zs = [[lax.dot_general(k_ref[0, pl.ds(kstarts[t], tk), pair(h)], q2_ref[h], NT_DIMS,
                               preferred_element_type=F32) for t in nt] for h in heads]
        c_old = [c_ref[h] for h in heads]
        acc_old = [acc_ref[h] for h in heads]

        def neg_log_keep(z, t):
            neg_abs = lax.bitcast_convert_type(lax.bitcast_convert_type(z, jnp.uint32) | sign_bit, F32)
            nlk = jnp.maximum(z, 0.0) + jnp.log(1.0 + jnp.exp2(neg_abs)) * LOG2E
            return jnp.where(masks[t], nlk, 0.0) if masked[t] else nlk

        masks = [None] * len(js)
        for t in nt:
            if masked[t]:
                key = kstarts[t] + lax.broadcasted_iota(jnp.int32, (tk, tq), 0)
                qry = qi * tq + lax.broadcasted_iota(jnp.int32, (tk, tq), 1)
                masks[t] = key < qry
        nlk = [[neg_log_keep(zs[h][t], t) for t in nt] for h in heads]
        rs = [[jnp.dot(tri_ref[...], nlk[h][t].astype(BF16), preferred_element_type=F32) for t in nt] for h in heads]
        pvs = []
        c_new = []
        for h in heads:
            c = c_old[h]
            pv = None
            for t in nt:
                w = jnp.exp2(zs[h][t] - rs[h][t] - c)
                if masked[t]:
                    w = jnp.where(masks[t], w, 0.0)
                d = jnp.dot(vt_ref[0, h // 2, :, pl.ds(kstarts[t], tk)], w.astype(BF16),
                            preferred_element_type=F32)
                pv = d if pv is None else pv + d
                c = c + jnp.sum(nlk[h][t], axis=0, keepdims=True)
            pvs.append(pv)
            c_new.append(c)
        for h in heads:
            c_ref[h] = c_new[h]
            acc_ref[h] = acc_old[h] + pvs[h]

    assert tq == tk
    pl.when(qi > 0)(lambda: tiles([qi, qi - 1], [True, False]))
    pl.when(qi == 0)(lambda: tiles([qi], [True]))

    def more(j):
        return jnp.logical_and(j >= 0, jnp.min(c_ref[...]) < STICK_UNDERFLOW_LOG2)

    def body(j):
        tiles([j], [False])
        return j - 1

    lax.while_loop(more, body, qi - 2)
    _store_pair_heads(o_ref, [acc_ref[h] for h in heads])


def _stick_attn(proj, vt, tri):
    b, s, _ = proj.shape
    nh = HEADS_PER_STEP
    width = nh * HEAD_DIM
    q0 = 3 * N_HEADS * HEAD_DIM // width
    k0 = 4 * N_HEADS * HEAD_DIM // width
    resident = dict(pipeline_mode=pl.Buffered(1))
    return pl.pallas_call(
        _stick_kernel,
        grid=(b, N_HEADS // nh, s // TQ),
        in_specs=[
            pl.BlockSpec((1, TQ, width), lambda bi, hg, qi: (bi, qi, q0 + hg)),
            pl.BlockSpec((1, s, width), lambda bi, hg, qi: (bi, 0, k0 + hg), **resident),
            pl.BlockSpec((1, nh // 2, LANES, s), lambda bi, hg, qi: (bi, hg, 0, 0), **resident),
            pl.BlockSpec((TK, TK), lambda bi, hg, qi: (0, 0)),
        ],
        out_specs=pl.BlockSpec((1, TQ, width), lambda bi, hg, qi: (bi, qi, hg)),
        out_shape=jax.ShapeDtypeStruct((b, s, N_HEADS * HEAD_DIM), BF16),
        scratch_shapes=[pltpu.VMEM((nh, TQ, LANES), BF16), pltpu.VMEM((nh, 1, TQ), F32),
                        pltpu.VMEM((nh, LANES, TQ), F32)],
        compiler_params=pltpu.CompilerParams(
            dimension_semantics=("arbitrary", "arbitrary", "arbitrary"), vmem_limit_bytes=VMEM_LIMIT),
        name="stick_attn",
    )(proj, proj, vt, tri)


def _softmax_attn_kernel(*refs, chunk_mask, decay_skip):
    if decay_skip:
        q_ref, k_ref, vt_ref, knorm_ref, fend_ref, o_ref, m_ref, acc_ref = refs
    else:
        q_ref, k_ref, vt_ref, o_ref, m_ref, acc_ref = refs
    bi = pl.program_id(0)
    qi = pl.program_id(2)
    tq = q_ref.shape[1]
    tk = TK
    heads = range(q_ref.shape[2] // LANES)
    sl = lambda h: slice(h * LANES, (h + 1) * LANES)
    m_ref[...] = jnp.full_like(m_ref, NEG_BIG)
    acc_ref[...] = jnp.zeros_like(acc_ref)

    def tiles(js, masked):
        kstarts = [pl.multiple_of(j * tk, tk) for j in js]

        def scores(h, t):
            s = lax.dot_general(k_ref[0, pl.ds(kstarts[t], tk), sl(h)], q_ref[0, :, sl(h)], NT_DIMS,
                                preferred_element_type=F32)
            if masked[t]:
                key = kstarts[t] + lax.broadcasted_iota(jnp.int32, (tk, tq), 0)
                qry = qi * tq + lax.broadcasted_iota(jnp.int32, (tk, tq), 1)
                s = jnp.where((key // CHUNK <= qry // CHUNK) if chunk_mask else (key <= qry), s, NEG_BIG)
            return s

        ss = [[scores(h, t) for t in range(len(js))] for h in heads]
        vts = [[vt_ref[0, h // 2, :, pl.ds(ks, tk)] for ks in kstarts] for h in heads]
        _softmax_tile_update(ss, vts, m_ref, acc_ref)

    assert tq == tk
    g = TILES_PER_STEP
    rem = qi % g

    def diagonal_step():
        for r in range(g):
            pl.when(rem == r)(functools.partial(tiles, [qi - r + t for t in range(r + 1)], [False] * r + [True]))

    if not decay_skip:
        def body(i, carry):
            tiles([g * i + t for t in range(g)], [False] * g)
            return carry

        lax.fori_loop(0, qi // g, body, 0)
        diagonal_step()
    else:
        diagonal_step()
        top = qi - 1 - rem
        lane = lax.broadcasted_iota(jnp.int32, (SUBLANES, LANES), 1)
        dims = jnp.where(lane < HEAD_DIM, 1.0, 0.0).astype(BF16)
        pieces = jnp.where((lane >= F_LANE0) & (lane < F_LANE0 + N_PIECES), 1.0, 0.0).astype(BF16)
        qnorm, fq = [], []
        for h in heads:
            q = q_ref[0, :, sl(h)]
            q32 = q.astype(F32)
            n2 = lax.dot_general(dims, (q32 * q32).astype(BF16), NT_DIMS, preferred_element_type=F32)[0:1]
            qnorm.append(jnp.sqrt(n2) * NORM_MARGIN)
            fq.append(lax.dot_general(pieces, q, NT_DIMS, preferred_element_type=F32)[0:1])

        def more(it):
            j = jnp.maximum(top - g * it, 0)
            gap = [qnorm[h] * knorm_ref[bi, j, h] + fq[h] - fend_ref[bi, j, h] - m_ref[h] for h in heads]
            reach = jnp.max(functools.reduce(jnp.maximum, gap))
            return jnp.logical_and(it < qi // g, reach > -SOFTMAX_UNDERFLOW_LOG2)

        def body(it):
            tiles([top - g * it - t for t in range(g)], [False] * g)
            return it + 1

        lax.while_loop(more, body, 0)
    _store_pair_heads(o_ref, [_normalised(acc_ref, h) for h in heads])


def _softmax_attn(q, k, vt, chunk_mask, name, skip_tables=None):
    b, s, _ = q.shape
    nh = HEADS_PER_STEP
    resident = dict(pipeline_mode=pl.Buffered(1))
    in_specs = [
        pl.BlockSpec((1, TQ, nh * LANES), lambda bi, hg, qi: (bi, qi, hg)),
        pl.BlockSpec((1, s, nh * LANES), lambda bi, hg, qi: (bi, 0, hg), **resident),
        pl.BlockSpec((1, nh // 2, LANES, s), lambda bi, hg, qi: (bi, hg, 0, 0), **resident),
    ]
    args = [q, k, vt]
    if skip_tables is not None:
        assert nh == N_HEADS
        in_specs += [pl.BlockSpec(memory_space=pltpu.SMEM)] * 2
        args += list(skip_tables)
    return pl.pallas_call(
        functools.partial(_softmax_attn_kernel, chunk_mask=chunk_mask, decay_skip=skip_tables is not None),
        grid=(b, N_HEADS // nh, s // TQ),
        in_specs=in_specs,
        out_specs=pl.BlockSpec((1, TQ, nh * HEAD_DIM), lambda bi, hg, qi: (bi, qi, hg)),
        out_shape=jax.ShapeDtypeStruct((b, s, N_HEADS * HEAD_DIM), BF16),
        scratch_shapes=[pltpu.VMEM((nh, 1, TQ), F32), pltpu.VMEM((nh, LANES + SUM_ROWS, TQ), F32)],
        compiler_params=pltpu.CompilerParams(
            dimension_semantics=("arbitrary", "arbitrary", "arbitrary"), vmem_limit_bytes=VMEM_LIMIT),
        name=name,
    )(*args)


def _out_ffn_kernel(*refs, tiles_per_seq, final_norm):
    if final_norm:
        (x_ref, o1_ref, o2_ref, wo_ref, g_ref, wg_ref, wu_ref, cw_ref, cb_ref, wd_ref, fg_ref,
         out_ref, x1_ref, h_ref, acc_ref, gbuf_ref, tail_ref) = refs
    else:
        (x_ref, o1_ref, o2_ref, wo_ref, g_ref, wg_ref, wu_ref, cw_ref, cb_ref, wd_ref,
         out_ref, x1_ref, h_ref, acc_ref, gbuf_ref, tail_ref) = refs
    i = pl.program_id(0)
    f = pl.program_id(1)
    nf = pl.num_programs(1)
    tm = x_ref.shape[0]
    half = o1_ref.shape[1]

    @pl.when(f == 0)
    def _():
        x1 = (x_ref[...]
              + jnp.dot(o1_ref[...], wo_ref[0:half, :], preferred_element_type=F32)
              + jnp.dot(o2_ref[...], wo_ref[half:2 * half, :], preferred_element_type=F32))
        x1_ref[...] = x1
        h_ref[...] = _rms(x1, g_ref[...]).astype(BF16)
        acc_ref[...] = jnp.zeros_like(acc_ref)

    h = h_ref[...]
    g = jnp.dot(h, wg_ref[...], preferred_element_type=F32)
    u = jnp.dot(h, wu_ref[...], preferred_element_type=F32)

    prev = jnp.where(i % tiles_per_seq == 0, 0.0, tail_ref[f])
    halo = SUBLANES
    gbuf_ref[0:halo, :] = prev
    gbuf_ref[halo:halo + tm, :] = g
    tail_ref[f] = g[tm - halo:tm, :]
    gm1 = gbuf_ref[halo - 1:halo - 1 + tm, :]
    gm2 = gbuf_ref[halo - 2:halo - 2 + tm, :]
    cw = cw_ref[...]
    gc = cw[0:1, :] * gm2 + cw[1:2, :] * gm1 + cw[2:3, :] * g + cb_ref[...]
    y = (gc / (1.0 + jnp.exp(-gc)) * u).astype(BF16)
    acc_ref[...] += jnp.dot(y, wd_ref[...], preferred_element_type=F32)

    @pl.when(f == nf - 1)
    def _():
        res = x1_ref[...] + acc_ref[...]
        if final_norm:
            res = _rms(res, fg_ref[...])
        out_ref[...] = res


def _out_ffn(x2d, o1, o2, wo, g, wg, wu, cw, cb, wd, final_g, seq_len):
    n, d = x2d.shape
    half = o1.shape[1]
    dff = wg.shape[1]
    tm, tf = TM_FFN, TF_FFN
    nf = dff // tf
    final_norm = final_g is not None
    once = dict(pipeline_mode=pl.Buffered(1))
    per_f = once if nf == 1 else {}
    in_specs = [
        pl.BlockSpec((tm, d), lambda i, f: (i, 0)),
        pl.BlockSpec((tm, half), lambda i, f: (i, 0)),
        pl.BlockSpec((tm, half), lambda i, f: (i, 0)),
        pl.BlockSpec((d, d), lambda i, f: (0, 0), **once),
        pl.BlockSpec((1, d), lambda i, f: (0, 0)),
        pl.BlockSpec((d, tf), lambda i, f: (0, f), **per_f),
        pl.BlockSpec((d, tf), lambda i, f: (0, f), **per_f),
        pl.BlockSpec((3, tf), lambda i, f: (0, f)),
        pl.BlockSpec((1, tf), lambda i, f: (0, f)),
        pl.BlockSpec((tf, d), lambda i, f: (f, 0), **per_f),
    ]
    args = [x2d, o1, o2, wo, g, wg, wu, cw, cb, wd]
    if final_norm:
        in_specs.append(pl.BlockSpec((1, d), lambda i, f: (0, 0)))
        args.append(final_g)
    return pl.pallas_call(
        functools.partial(_out_ffn_kernel, tiles_per_seq=seq_len // tm, final_norm=final_norm),
        grid=(n // tm, nf),
        in_specs=in_specs,
        out_specs=pl.BlockSpec((tm, d), lambda i, f: (i, 0)),
        out_shape=jax.ShapeDtypeStruct((n, d), F32),
        scratch_shapes=[
            pltpu.VMEM((tm, d), F32),
            pltpu.VMEM((tm, d), BF16),
            pltpu.VMEM((tm, d), F32),
            pltpu.VMEM((tm + SUBLANES, tf), F32),
            pltpu.VMEM((nf, SUBLANES, tf), F32),
        ],
        compiler_params=pltpu.CompilerParams(
            dimension_semantics=("arbitrary", "arbitrary"), vmem_limit_bytes=VMEM_LIMIT),
        name="out_ffn_final" if final_norm else "out_ffn",
    )(*args)


def _rope_tables(seq_len):
    half = QK_ROPE // 2
    inv = ROPE_THETA ** (-jnp.arange(half, dtype=F32) / half)
    ang = jnp.arange(seq_len, dtype=F32)[:, None] * inv[None, :]
    cos, sin = jnp.cos(ang), jnp.sin(ang)
    ones = jnp.ones((seq_len, QK_NOPE), F32)
    zeros = jnp.zeros((seq_len, QK_NOPE), F32)
    pad1 = jnp.ones((seq_len, LANES - QK_NOPE - QK_ROPE), F32)
    pad0 = jnp.zeros((seq_len, LANES - QK_NOPE - QK_ROPE), F32)
    return (jnp.concatenate([ones, cos, cos, pad1], axis=1),
            jnp.concatenate([zeros, -sin, sin, pad0], axis=1))


def _pad_heads(w, width):
    rows = w.shape[0]
    return jnp.zeros((rows, N_HEADS, LANES), w.dtype).at[:, :, :width].set(
        w.reshape(rows, N_HEADS, width)).reshape(rows, N_HEADS * LANES)


def _cd_weights(w_in, w_uq, w_ukv, b_f):
    d = w_in.shape[0]
    nd = N_HEADS * HEAD_DIM
    o = Q_RANK + KV_RANK + QK_ROPE
    c_q, c_kv, k_rope = w_in[:, :Q_RANK], w_in[:, Q_RANK:Q_RANK + KV_RANK], w_in[:, Q_RANK + KV_RANK:o]
    q_d, k_d, v_d, f_logit = (w_in[:, o:o + nd], w_in[:, o + nd:o + 2 * nd], w_in[:, o + 2 * nd:o + 3 * nd],
                              w_in[:, o + 3 * nd:])
    kr_blk = jnp.zeros((d, LANES), w_in.dtype).at[:, QK_NOPE:QK_NOPE + QK_ROPE].set(k_rope)
    f_blk = jnp.zeros((d, LANES), w_in.dtype).at[:, :N_HEADS].set(f_logit)
    w1 = jnp.concatenate([c_q, c_kv, kr_blk, _pad_heads(q_d, HEAD_DIM), _pad_heads(k_d, HEAD_DIM), v_d, f_blk],
                         axis=1).astype(BF16)
    wuq = _pad_heads(w_uq, QK_NOPE + QK_ROPE).astype(BF16)
    ukv = w_ukv.reshape(KV_RANK, N_HEADS, QK_NOPE + HEAD_DIM)
    wk = _pad_heads(ukv[:, :, :QK_NOPE].reshape(KV_RANK, N_HEADS * QK_NOPE), QK_NOPE).astype(BF16)
    wv = ukv[:, :, QK_NOPE:].reshape(KV_RANK, N_HEADS * HEAD_DIM).astype(BF16)
    bf = jnp.zeros((1, LANES), F32).at[0, :N_HEADS].set(b_f.astype(F32))
    return w1, wuq, wk, wv, bf


def _forget_selectors():
    selq = np.zeros((LANES, _NPAD), np.float32)
    selk = np.zeros((LANES, _NPAD), np.float32)
    oneq = np.zeros((1, _NPAD), np.float32)
    onek = np.zeros((1, _NPAD), np.float32)
    hsel = np.zeros((_NPAD, LANES), np.float32)
    for h in range(N_HEADS):
        hsel[h * LANES:(h + 1) * LANES, h] = 1.0
        for j in range(N_PIECES):
            selq[j * N_HEADS + h, h * LANES + F_LANE0 + j] = 1.0
            onek[0, h * LANES + F_LANE0 + j] = 1.0
            selk[j * N_HEADS + h, h * LANES + ONE_LANE0 + j] = -1.0
            oneq[0, h * LANES + ONE_LANE0 + j] = 1.0
    return (jnp.asarray(selq, BF16), jnp.asarray(selk, BF16), jnp.asarray(oneq), jnp.asarray(onek),
            jnp.asarray(hsel, BF16))


def kernel(x, ab_norm, ab_w_in, ab_rel_bias, ab_w_o, cd_norm, cd_w_in, cd_q_norm, cd_w_uq, cd_kv_norm, cd_w_ukv,
           cd_b_f, cd_w_o, ffn_norm, ffn_w_gate, ffn_w_up, ffn_conv_w, ffn_conv_b, ffn_w_down, final_norm):
    b, s, d = x.shape
    n = b * s

    qscale = jnp.full((N_HEADS * HEAD_DIM,), HEAD_DIM ** -0.5 * LOG2E, F32)
    one = jnp.ones((2 * N_HEADS * HEAD_DIM,), F32)
    colscale = jnp.concatenate([qscale, one, qscale, one])[None, :]
    proj, vat, vbt = _ab_proj(x, ab_norm[0][None, :], ab_w_in[0].astype(BF16), colscale)
    oa = _chunk_attn(proj, vat, _chunk_bias_tiles(ab_rel_bias[0]))
    tri = jnp.asarray(np.triu(np.ones((TK, TK), np.float32)), BF16)
    ob = _stick_attn(proj, vbt, tri)
    x2d = _out_ffn(x.reshape(n, d), oa.reshape(n, -1), ob.reshape(n, -1), ab_w_o[0].astype(BF16),
                   ffn_norm[0][None, :], ffn_w_gate[0].astype(BF16), ffn_w_up[0].astype(BF16), ffn_conv_w[0],
                   ffn_conv_b[0][None, :], ffn_w_down[0].astype(BF16), None, s)

    w1, wuq, wk, wv, bf = _cd_weights(cd_w_in[0], cd_w_uq[0], cd_w_ukv[0], cd_b_f[0])
    cos_t, sin_t = _rope_tables(s)
    qc, kc, vct, qd, kd, vdt, knorm, fend = _cd_proj(
        x2d.reshape(b, s, d), cd_norm[0][None, :], w1, cd_q_norm[0][None, :], wuq, cd_kv_norm[0][None, :], wk, wv, bf,
        cos_t, sin_t, *_forget_selectors())
    oc = _softmax_attn(qc, kc, vct, True, "mla_attn")
    od = _softmax_attn(qd, kd, vdt, False, "fox_attn", (knorm[:, :, 0, :N_HEADS], fend[:, :, 0, :N_HEADS]))
    out = _out_ffn(x2d, oc.reshape(n, -1), od.reshape(n, -1), cd_w_o[0].astype(BF16), ffn_norm[1][None, :],
                   ffn_w_gate[1].astype(BF16), ffn_w_up[1].astype(BF16), ffn_conv_w[1], ffn_conv_b[1][None, :],
                   ffn_w_down[1].astype(BF16), final_norm[None, :], s)
    return out.reshape(b, s, d)
```

```python
import functools
import math

import numpy as np
import jax
import jax.numpy as jnp
from jax import lax
from jax.experimental import pallas as pl
from jax.experimental.pallas import tpu as pltpu

F32 = jnp.float32
BF16 = jnp.bfloat16

D_MODEL = 1024
HEAD_DIM = 64
CHUNK = 64
LEFT_CHUNKS = 8
BAND = (LEFT_CHUNKS + 1) * CHUNK
MAX_REL = 128
N_HEADS = 8
N_PAIRS = N_HEADS // 2
QK_NOPE = 64
QK_ROPE = 32
Q_RANK = 384
KV_RANK = 256
ROPE_THETA = 10000.0
D_FF = 2816
RMS_EPS = 1e-6

LANES = 128
SUBLANES = 8
LOG2E = math.log2(math.e)
NEG_BIG = -1e30
STICK_UNDERFLOW_LOG2 = 200.0
SOFTMAX_UNDERFLOW_LOG2 = 160.0
NORM_MARGIN = 1.01
VMEM_LIMIT = 52 * 1024 * 1024

TM_PROJ = 256
TM_FFN = 256
TF_FFN = 2816
TK = 256
TQ = 256
N_WIN_A = LEFT_CHUNKS * CHUNK // TK + 1
HEADS_PER_STEP = 8
TILES_PER_STEP = 2
SUM_ROWS = 16

N_PIECES = 3
F_LANE0 = HEAD_DIM
ONE_LANE0 = HEAD_DIM + N_PIECES

NT_DIMS = (((1,), (1,)), ((), ()))


def _rms(x, g):
    ms = jnp.mean(x * x, axis=-1, keepdims=True)
    return x * lax.rsqrt(ms + RMS_EPS) * g


def _store_pairs_transposed(vt_ref, v):
    for p in range(N_PAIRS):
        vt_ref[0, p] = v[:, p * LANES:(p + 1) * LANES].T.astype(vt_ref.dtype)


def _ab_proj_kernel(x_ref, g_ref, w_ref, cs_ref, o_ref, vat_ref, vbt_ref):
    h = _rms(x_ref[0], g_ref[...]).astype(BF16)
    p = jnp.dot(h, w_ref[...], preferred_element_type=F32) * cs_ref[...]
    o_ref[0] = p.astype(o_ref.dtype)
    nv = N_HEADS * HEAD_DIM
    _store_pairs_transposed(vat_ref, p[:, 2 * nv:3 * nv])
    _store_pairs_transposed(vbt_ref, p[:, 5 * nv:6 * nv])


def _ab_proj(x, g, w, colscale):
    b, s, d = x.shape
    nc = w.shape[1]
    tm = TM_PROJ
    vt_spec = pl.BlockSpec((1, N_PAIRS, LANES, tm), lambda bi, ti: (bi, 0, 0, ti))
    vt_shape = jax.ShapeDtypeStruct((b, N_PAIRS, LANES, s), BF16)
    return pl.pallas_call(
        _ab_proj_kernel,
        grid=(b, s // tm),
        in_specs=[
            pl.BlockSpec((1, tm, d), lambda bi, ti: (bi, ti, 0)),
            pl.BlockSpec((1, d), lambda bi, ti: (0, 0)),
            pl.BlockSpec((d, nc), lambda bi, ti: (0, 0)),
            pl.BlockSpec((1, nc), lambda bi, ti: (0, 0)),
        ],
        out_specs=[pl.BlockSpec((1, tm, nc), lambda bi, ti: (bi, ti, 0)), vt_spec, vt_spec],
        out_shape=[jax.ShapeDtypeStruct((b, s, nc), BF16), vt_shape, vt_shape],
        compiler_params=pltpu.CompilerParams(
            dimension_semantics=("arbitrary", "arbitrary"), vmem_limit_bytes=VMEM_LIMIT),
        name="ab_norm_proj",
    )(x, g, w, colscale)


_NPAD = N_HEADS * LANES
_C_Q0 = 0
_C_KV0 = _C_Q0 + Q_RANK
_C_KR0 = _C_KV0 + KV_RANK
_C_QD0 = _C_KR0 + LANES
_C_KD0 = _C_QD0 + N_HEADS * HEAD_DIM
_C_VD0 = _C_KD0 + N_HEADS * HEAD_DIM
_C_F0 = _C_VD0 + N_HEADS * HEAD_DIM
_C_END = _C_F0 + LANES


def _cd_proj_kernel(x_ref, g_ref, w1_ref, qn_ref, wuq_ref, kvn_ref, wk_ref, wv_ref, bf_ref, cos_ref, sin_ref,
                    selq_ref, selk_ref, oneq_ref, onek_ref, hsel_ref,
                    qc_ref, kc_ref, vct_ref, qd_ref, kd_ref, vdt_ref, knorm_ref, fend_ref, carry_ref, kcarry_ref):
    t = pl.program_id(1)
    tm = x_ref.shape[1]
    h = _rms(x_ref[0], g_ref[...]).astype(BF16)
    p = jnp.dot(h, w1_ref[...], preferred_element_type=F32)

    cq = _rms(p[:, _C_Q0:_C_KV0], qn_ref[...]).astype(BF16)
    ckv = _rms(p[:, _C_KV0:_C_KR0], kvn_ref[...]).astype(BF16)

    cosb = cos_ref[...]
    sinb = sin_ref[...]
    lane = lax.broadcasted_iota(jnp.int32, (tm, LANES), 1)

    def rope(xb):
        partner = jnp.where(lane < QK_NOPE + QK_ROPE // 2, pltpu.roll(xb, LANES - QK_ROPE // 2, 1),
                            pltpu.roll(xb, QK_ROPE // 2, 1))
        return xb * cosb + partner * sinb

    qc = jnp.dot(cq, wuq_ref[...], preferred_element_type=F32) * ((QK_NOPE + QK_ROPE) ** -0.5 * LOG2E)
    kc = jnp.dot(ckv, wk_ref[...], preferred_element_type=F32)
    kr = rope(p[:, _C_KR0:_C_QD0])
    for hh in range(N_HEADS):
        sl = slice(hh * LANES, (hh + 1) * LANES)
        qc_ref[0, :, sl] = rope(qc[:, sl]).astype(BF16)
        kc_ref[0, :, sl] = (kc[:, sl] + kr).astype(BF16)
    _store_pairs_transposed(vct_ref, jnp.dot(ckv, wv_ref[...], preferred_element_type=F32))
    _store_pairs_transposed(vdt_ref, p[:, _C_VD0:_C_F0])

    fl = p[:, _C_F0:_C_END] + bf_ref[...]
    y = jnp.minimum(fl, 0.0) - jnp.log(1.0 + jnp.exp(-jnp.abs(fl)))
    row = lax.broadcasted_iota(jnp.int32, (tm, LANES), 0)
    sh = 1
    while sh < tm:
        y = y + jnp.where(row >= sh, pltpu.roll(y, sh, 0), 0.0)
        sh *= 2

    @pl.when(t == 0)
    def _():
        carry_ref[...] = jnp.zeros_like(carry_ref)
        kcarry_ref[...] = jnp.zeros_like(kcarry_ref)

    y = y + carry_ref[0:1, :]
    carry_ref[...] = jnp.broadcast_to(y[tm - 1:tm, :], carry_ref.shape)
    f2 = y * LOG2E

    hi = f2.astype(BF16).astype(F32)
    r1 = f2 - hi
    mid = r1.astype(BF16).astype(F32)
    lo = r1 - mid
    fp = jnp.where(lane < N_HEADS, hi, jnp.where(lane < 2 * N_HEADS, pltpu.roll(mid, N_HEADS, 1),
                                                 pltpu.roll(lo, 2 * N_HEADS, 1))).astype(BF16)
    def one_head_per_block(x2):
        blocks = []
        for hh in range(N_HEADS):
            src = x2[:, (hh // 2) * LANES:(hh // 2 + 1) * LANES]
            if hh % 2:
                src = pltpu.roll(src, HEAD_DIM, 1)
            blocks.append(jnp.where(lane < HEAD_DIM, src, 0.0))
        return jnp.concatenate(blocks, axis=1)

    qd = one_head_per_block(p[:, _C_QD0:_C_KD0]) * (HEAD_DIM ** -0.5 * LOG2E)
    kd = one_head_per_block(p[:, _C_KD0:_C_VD0])
    qd_ref[0] = (qd +jnp.dot(fp, selq_ref[...], preferred_element_type=F32) + oneq_ref[...]).astype(BF16)
    kd_ref[0] = (kd + jnp.dot(fp, selk_ref[...], preferred_element_type=F32) + onek_ref[...]).astype(BF16)

    n2 = jnp.dot((kd * kd).astype(BF16), hsel_ref[...], preferred_element_type=F32)
    kmax = jnp.maximum(kcarry_ref[0:1, :], jnp.sqrt(jnp.max(n2, axis=0, keepdims=True)) * NORM_MARGIN)
    kcarry_ref[...] = jnp.broadcast_to(kmax, kcarry_ref.shape)
    knorm_ref[0, 0] = jnp.broadcast_to(kmax, knorm_ref.shape[2:])
    fend_ref[0, 0] = jnp.broadcast_to(f2[tm - 1:tm, :], fend_ref.shape[2:])


def _cd_proj(x, g, w1, qn, wuq, kvn, wk, wv, bf, cos_t, sin_t, selq, selk, oneq, onek, hsel):
    b, s, d = x.shape
    tm = TM_PROJ
    assert tm == TK
    const = lambda a: pl.BlockSpec(a.shape, lambda bi, ti: (0,) * a.ndim)
    tok = lambda nc: pl.BlockSpec((1, tm, nc), lambda bi, ti: (bi, ti, 0))
    vt_spec = pl.BlockSpec((1, N_PAIRS, LANES, tm), lambda bi, ti: (bi, 0, 0, ti))
    tab_spec = pl.BlockSpec((1, 1, SUBLANES, LANES), lambda bi, ti: (bi, ti, 0, 0))
    act = jax.ShapeDtypeStruct((b, s, _NPAD), BF16)
    vt = jax.ShapeDtypeStruct((b, N_PAIRS, LANES, s), BF16)
    tab = jax.ShapeDtypeStruct((b, s // tm, SUBLANES, LANES), F32)
    return pl.pallas_call(
        _cd_proj_kernel,
        grid=(b, s // tm),
        in_specs=[
            tok(d), const(g), const(w1), const(qn), const(wuq), const(kvn), const(wk), const(wv), const(bf),
            pl.BlockSpec((tm, LANES), lambda bi, ti: (ti, 0)),
            pl.BlockSpec((tm, LANES), lambda bi, ti: (ti, 0)),
            const(selq), const(selk), const(oneq), const(onek), const(hsel),
        ],
        out_specs=[tok(_NPAD), tok(_NPAD), vt_spec, tok(_NPAD), tok(_NPAD), vt_spec, tab_spec, tab_spec],
        out_shape=[act, act, vt, act, act, vt, tab, tab],
        scratch_shapes=[pltpu.VMEM((SUBLANES, LANES), F32), pltpu.VMEM((SUBLANES, LANES), F32)],
        compiler_params=pltpu.CompilerParams(
            dimension_semantics=("arbitrary", "arbitrary"), vmem_limit_bytes=VMEM_LIMIT),
        name="cd_norm_proj",
    )(x, g, w1, qn, wuq, kvn, wk, wv, bf, cos_t, sin_t, selq, selk, oneq, onek, hsel)


def _mask_pair_heads(q_ref, q2_ref):
    tq = q_ref.shape[1]
    lane = lax.broadcasted_iota(jnp.int32, (tq, LANES), 1)
    for h in range(q2_ref.shape[0]):
        q = q_ref[0, :, (h // 2) * LANES:(h // 2 + 1) * LANES]
        q2_ref[h] = jnp.where((lane >= HEAD_DIM) == bool(h % 2), q, jnp.zeros_like(q))


def _store_pair_heads(o_ref, outs):
    tq = outs[0].shape[1]
    lane = lax.broadcasted_iota(jnp.int32, (tq, LANES), 1)
    for pr in range(len(outs) // 2):
        o_ref[0, :, pr * LANES:(pr + 1) * LANES] = jnp.where(
            lane < HEAD_DIM, outs[2 * pr].T, outs[2 * pr + 1].T).astype(o_ref.dtype)


def _softmax_tile_update(ss, vts, m_ref, acc_ref):
    heads = range(len(ss))
    m_old = [m_ref[h] for h in heads]
    acc_old = [acc_ref[h] for h in heads]
    m_new = [functools.reduce(jnp.maximum, [jnp.max(s, axis=0, keepdims=True) for s in ss[h]], m_old[h])
             for h in heads]
    alpha = [jnp.exp2(m_old[h] - m_new[h]) for h in heads]
    ps = [[jnp.exp2(s - m_new[h]).astype(BF16) for s in ss[h]] for h in heads]
    ones = jnp.ones((SUM_ROWS, ss[0][0].shape[0]), BF16)
    pvs = [sum(jnp.dot(jnp.concatenate([vt, ones], axis=0), p, preferred_element_type=F32)
               for vt, p in zip(vts[h], ps[h])) for h in heads]
    for h in heads:
        m_ref[h] = m_new[h]
        acc_ref[h] = alpha[h] * acc_old[h] + pvs[h]


def _normalised(acc_ref, h):
    return acc_ref[h, 0:LANES, :] / acc_ref[h, LANES:LANES + 1, :]


def _chunk_attn_kernel(q_ref, k_ref, vt_ref, bias_ref, o_ref, q2_ref, m_ref, acc_ref):
    qi = pl.program_id(2)
    tq = q_ref.shape[1]
    heads = range(acc_ref.shape[0])
    _mask_pair_heads(q_ref, q2_ref)
    m_ref[...] = jnp.full_like(m_ref, NEG_BIG)
    acc_ref[...] = jnp.zeros_like(acc_ref)

    def run(js):
        kstarts = [pl.multiple_of((qi - (N_WIN_A - 1) + j) * tq, tq) for j in js]
        ss = [[lax.dot_general(k_ref[0, pl.ds(ks, tq), (h // 2) * LANES:(h // 2 + 1) * LANES], q2_ref[h],
                               NT_DIMS, preferred_element_type=F32) + bias_ref[h, j] for j, ks in zip(js, kstarts)]
              for h in heads]
        vts = [[vt_ref[0, h // 2, :, pl.ds(ks, tq)] for ks in kstarts] for h in heads]
        _softmax_tile_update(ss, vts, m_ref, acc_ref)
        _store_pair_heads(o_ref, [_normalised(acc_ref, h) for h in heads])

    first = jnp.maximum(N_WIN_A - 1 - qi, 0)
    for f in range(N_WIN_A):
        pl.when(first == f)(functools.partial(run, list(range(f, N_WIN_A))))


def _chunk_attn(proj, vt, bias):
    b, s, _ = proj.shape
    nh = HEADS_PER_STEP
    width = nh * HEAD_DIM
    k0 = N_HEADS * HEAD_DIM // width
    resident = dict(pipeline_mode=pl.Buffered(1))
    return pl.pallas_call(
        _chunk_attn_kernel,
        grid=(b, N_HEADS // nh, s // TK),
        in_specs=[
            pl.BlockSpec((1, TK, width), lambda bi, hg, qi: (bi, qi, hg)),
            pl.BlockSpec((1, s, width), lambda bi, hg, qi: (bi, 0, k0 + hg), **resident),
            pl.BlockSpec((1, nh // 2, LANES, s), lambda bi, hg, qi: (bi, hg, 0, 0), **resident),
            pl.BlockSpec((nh, N_WIN_A, TK, TK), lambda bi, hg, qi: (hg, 0, 0, 0), **resident),
        ],
        out_specs=pl.BlockSpec((1, TK, width), lambda bi, hg, qi: (bi, qi, hg)),
        out_shape=jax.ShapeDtypeStruct((b, s, N_HEADS * HEAD_DIM), BF16),
        scratch_shapes=[pltpu.VMEM((nh, TK, LANES), BF16), pltpu.VMEM((nh, 1, TK), F32),
                        pltpu.VMEM((nh, LANES + SUM_ROWS, TK), F32)],
        compiler_params=pltpu.CompilerParams(
            dimension_semantics=("arbitrary", "arbitrary", "arbitrary"), vmem_limit_bytes=VMEM_LIMIT),
        name="chunk_attn",
    )(proj, proj, vt, bias)


def _chunk_bias_tiles(rel_bias):
    h = rel_bias.shape[0]
    nq = TK
    nk = N_WIN_A * TK
    period = 1 << (nq + nk - 1).bit_length()
    u = np.arange(period)
    signed = np.where(u < nk, u, u - period)
    idx = np.clip(LEFT_CHUNKS * CHUNK - signed, -MAX_REL, MAX_REL) + MAX_REL
    v = rel_bias.astype(F32)[:, idx] * LOG2E
    toep = jnp.tile(v, (1, nq))[:, :nq * (period - 1)].reshape(h, nq, period - 1)[:, :, :nk]
    r = np.arange(nq)[:, None]
    off = np.arange(nk)[None, :] - CHUNK * (r // CHUNK)
    in_band = (off >= 0) & (off < BAND)
    bias = jnp.where(in_band[None], toep, NEG_BIG)
    return bias.reshape(h, nq, N_WIN_A, TK).transpose(0, 2, 3, 1)


def _stick_kernel(q_ref, k_ref, vt_ref, tri_ref, o_ref, q2_ref, c_ref, acc_ref):
    qi = pl.program_id(2)
    tq = q_ref.shape[1]
    tk = tri_ref.shape[0]
    heads = range(acc_ref.shape[0])
    pair = lambda h: slice((h // 2) * LANES, (h // 2 + 1) * LANES)
    _mask_pair_heads(q_ref, q2_ref)
    c_ref[...] = jnp.zeros_like(c_ref)
    acc_ref[...] = jnp.zeros_like(acc_ref)
    sign_bit = jnp.uint32(0x80000000)

    def tiles(js, masked):
        nt = range(len(js))
        kstarts = [pl.multiple_of(j * tk, tk) for j in js]
        zs = [[lax.dot_general(k_ref[0, pl.ds(kstarts[t], tk), pair(h)], q2_ref[h], NT_DIMS,
                               preferred_element_type=F32) for t in nt] for h in heads]
        c_old = [c_ref[h] for h in heads]
        acc_old = [acc_ref[h] for h in heads]

        def neg_log_keep(z, t):
            neg_abs = lax.bitcast_convert_type(lax.bitcast_convert_type(z, jnp.uint32) | sign_bit, F32)
            nlk = jnp.maximum(z, 0.0) + jnp.log(1.0 + jnp.exp2(neg_abs)) * LOG2E
            return jnp.where(masks[t], nlk, 0.0) if masked[t] else nlk

        masks = [None] * len(js)
        for t in nt:
            if masked[t]:
                key = kstarts[t] + lax.broadcasted_iota(jnp.int32, (tk, tq), 0)
                qry = qi * tq + lax.broadcasted_iota(jnp.int32, (tk, tq), 1)
                masks[t] = key < qry
        nlk = [[neg_log_keep(zs[h][t], t) for t in nt] for h in heads]
        rs = [[jnp.dot(tri_ref[...], nlk[h][t].astype(BF16), preferred_element_type=F32) for t in nt] for h in heads]
        pvs = []
        c_new = []
        for h in heads:
            c = c_old[h]
            pv = None
            for t in nt:
                w = jnp.exp2((zs[h][t] - nlk[h][t]) - rs[h][t] - c)
                if masked[t]:
                    w = jnp.where(masks[t], w, 0.0)
                d = jnp.dot(vt_ref[0, h // 2, :, pl.ds(kstarts[t], tk)], w.astype(BF16),
                            preferred_element_type=F32)
                pv = d if pv is None else pv + d
                c = c + jnp.sum(nlk[h][t], axis=0, keepdims=True)
            pvs.append(pv)
            c_new.append(c)
        for h in heads:
            c_ref[h] = c_new[h]
            acc_ref[h] = acc_old[h] + pvs[h]

    assert tq == tk
    pl.when(qi > 0)(lambda: tiles([qi, qi - 1], [True, False]))
    pl.when(qi == 0)(lambda: tiles([qi], [True]))

    def more(j):
        return jnp.logical_and(j >= 0, jnp.min(c_ref[...]) < STICK_UNDERFLOW_LOG2)

    def body(j):
        tiles([j], [False])
        return j - 1

    lax.while_loop(more, body, qi - 2)
    _store_pair_heads(o_ref, [acc_ref[h] for h in heads])


def _stick_attn(proj, vt, tri):
    b, s, _ = proj.shape
    nh = HEADS_PER_STEP
    width = nh * HEAD_DIM
    q0 = 3 * N_HEADS * HEAD_DIM // width
    k0 = 4 * N_HEADS * HEAD_DIM // width
    resident = dict(pipeline_mode=pl.Buffered(1))
    return pl.pallas_call(
        _stick_kernel,
        grid=(b, N_HEADS // nh, s // TQ),
        in_specs=[
            pl.BlockSpec((1, TQ, width), lambda bi, hg, qi: (bi, qi, q0 + hg)),
            pl.BlockSpec((1, s, width), lambda bi, hg, qi: (bi, 0, k0 + hg), **resident),
            pl.BlockSpec((1, nh // 2, LANES, s), lambda bi, hg, qi: (bi, hg, 0, 0), **resident),
            pl.BlockSpec((TK, TK), lambda bi, hg, qi: (0, 0)),
        ],
        out_specs=pl.BlockSpec((1, TQ, width), lambda bi, hg, qi: (bi, qi, hg)),
        out_shape=jax.ShapeDtypeStruct((b, s, N_HEADS * HEAD_DIM), BF16),
        scratch_shapes=[pltpu.VMEM((nh, TQ, LANES), BF16), pltpu.VMEM((nh, 1, TQ), F32),
                        pltpu.VMEM((nh, LANES, TQ), F32)],
        compiler_params=pltpu.CompilerParams(
            dimension_semantics=("arbitrary", "arbitrary", "arbitrary"), vmem_limit_bytes=VMEM_LIMIT),
        name="stick_attn",
    )(proj, proj, vt, tri)


def _softmax_attn_kernel(*refs, chunk_mask, decay_skip):
    if decay_skip:
        q_ref, k_ref, vt_ref, knorm_ref, fend_ref, o_ref, m_ref, acc_ref = refs
    else:
        q_ref, k_ref, vt_ref, o_ref, m_ref, acc_ref = refs
    bi = pl.program_id(0)
    qi = pl.program_id(2)
    tq = q_ref.shape[1]
    tk = TK
    heads = range(q_ref.shape[2] // LANES)
    sl = lambda h: slice(h * LANES, (h + 1) * LANES)
    m_ref[...] = jnp.full_like(m_ref, NEG_BIG)
    acc_ref[...] = jnp.zeros_like(acc_ref)

    def tiles(js, masked):
        kstarts = [pl.multiple_of(j * tk, tk) for j in js]

        def scores(h, t):
            s = lax.dot_general(k_ref[0, pl.ds(kstarts[t], tk), sl(h)], q_ref[0, :, sl(h)], NT_DIMS,
                                preferred_element_type=F32)
            if masked[t]:
                key = kstarts[t] + lax.broadcasted_iota(jnp.int32, (tk, tq), 0)
                qry = qi * tq + lax.broadcasted_iota(jnp.int32, (tk, tq), 1)
                s = jnp.where((key // CHUNK <= qry // CHUNK) if chunk_mask else (key <= qry), s, NEG_BIG)
            return s

        ss = [[scores(h, t) for t in range(len(js))] for h in heads]
        vts = [[vt_ref[0, h // 2, :, pl.ds(ks, tk)] for ks in kstarts] for h in heads]
        _softmax_tile_update(ss, vts, m_ref, acc_ref)

    assert tq == tk
    g = TILES_PER_STEP
    rem = qi % g

    def diagonal_step():
        for r in range(g):
            pl.when(rem == r)(functools.partial(tiles, [qi - r + t for t in range(r + 1)], [False] * r + [True]))

    if not decay_skip:
        def body(i, carry):
            tiles([g * i + t for t in range(g)], [False] * g)
            return carry

        lax.fori_loop(0, qi // g, body, 0)
        diagonal_step()
    else:
        diagonal_step()
        top = qi - 1 - rem
        lane = lax.broadcasted_iota(jnp.int32, (SUBLANES, LANES), 1)
        dims = jnp.where(lane < HEAD_DIM, 1.0, 0.0).astype(BF16)
        pieces = jnp.where((lane >= F_LANE0) & (lane < F_LANE0 + N_PIECES), 1.0, 0.0).astype(BF16)
        qnorm, fq = [], []
        for h in heads:
            q = q_ref[0, :, sl(h)]
            q32 = q.astype(F32)
            n2 = lax.dot_general(dims, (q32 * q32).astype(BF16), NT_DIMS, preferred_element_type=F32)[0:1]
            qnorm.append(jnp.sqrt(n2) * NORM_MARGIN)
            fq.append(lax.dot_general(pieces, q, NT_DIMS, preferred_element_type=F32)[0:1])

        def more(it):
            j = jnp.maximum(top - g * it, 0)
            gap = [qnorm[h] * knorm_ref[bi, j, h] + fq[h] - fend_ref[bi, j, h] - m_ref[h] for h in heads]
            reach = jnp.max(functools.reduce(jnp.maximum, gap))
            return jnp.logical_and(it < qi // g, reach > -SOFTMAX_UNDERFLOW_LOG2)

        def body(it):
            tiles([top - g * it - t for t in range(g)], [False] * g)
            return it + 1

        lax.while_loop(more, body, 0)
    _store_pair_heads(o_ref, [_normalised(acc_ref, h) for h in heads])


def _softmax_attn(q, k, vt, chunk_mask, name, skip_tables=None):
    b, s, _ = q.shape
    nh = HEADS_PER_STEP
    resident = dict(pipeline_mode=pl.Buffered(1))
    in_specs = [
        pl.BlockSpec((1, TQ, nh * LANES), lambda bi, hg, qi: (bi, qi, hg)),
        pl.BlockSpec((1, s, nh * LANES), lambda bi, hg, qi: (bi, 0, hg), **resident),
        pl.BlockSpec((1, nh // 2, LANES, s), lambda bi, hg, qi: (bi, hg, 0, 0), **resident),
    ]
    args = [q, k, vt]
    if skip_tables is not None:
        assert nh == N_HEADS
        in_specs += [pl.BlockSpec(memory_space=pltpu.SMEM)] * 2
        args += list(skip_tables)
    return pl.pallas_call(
        functools.partial(_softmax_attn_kernel, chunk_mask=chunk_mask, decay_skip=skip_tables is not None),
        grid=(b, N_HEADS // nh, s // TQ),
        in_specs=in_specs,
        out_specs=pl.BlockSpec((1, TQ, nh * HEAD_DIM), lambda bi, hg, qi: (bi, qi, hg)),
        out_shape=jax.ShapeDtypeStruct((b, s, N_HEADS * HEAD_DIM), BF16),
        scratch_shapes=[pltpu.VMEM((nh, 1, TQ), F32), pltpu.VMEM((nh, LANES + SUM_ROWS, TQ), F32)],
        compiler_params=pltpu.CompilerParams(
            dimension_semantics=("arbitrary", "arbitrary", "arbitrary"), vmem_limit_bytes=VMEM_LIMIT),
        name=name,
    )(*args)


def _out_ffn_kernel(*refs, tiles_per_seq, final_norm):
    if final_norm:
        (x_ref, o1_ref, o2_ref, wo_ref, g_ref, wg_ref, wu_ref, cw_ref, cb_ref, wd_ref, fg_ref,
         out_ref, x1_ref, h_ref, acc_ref, gbuf_ref, tail_ref) = refs
    else:
        (x_ref, o1_ref, o2_ref, wo_ref, g_ref, wg_ref, wu_ref, cw_ref, cb_ref, wd_ref,
         out_ref, x1_ref, h_ref, acc_ref, gbuf_ref, tail_ref) = refs
    i = pl.program_id(0)
    f = pl.program_id(1)
    nf = pl.num_programs(1)
    tm = x_ref.shape[0]
    half = o1_ref.shape[1]

    @pl.when(f == 0)
    def _():
        x1 = (x_ref[...]
              + jnp.dot(o1_ref[...], wo_ref[0:half, :], preferred_element_type=F32)
              + jnp.dot(o2_ref[...], wo_ref[half:2 * half, :], preferred_element_type=F32))
        x1_ref[...] = x1
        h_ref[...] = _rms(x1, g_ref[...]).astype(BF16)
        acc_ref[...] = jnp.zeros_like(acc_ref)

    h = h_ref[...]
    g = jnp.dot(h, wg_ref[...], preferred_element_type=F32)
    u = jnp.dot(h, wu_ref[...], preferred_element_type=F32)

    prev = jnp.where(i % tiles_per_seq == 0, 0.0, tail_ref[f])
    halo = SUBLANES
    gbuf_ref[0:halo, :] = prev
    gbuf_ref[halo:halo + tm, :] = g
    tail_ref[f] = g[tm - halo:tm, :]
    gm1 = gbuf_ref[halo - 1:halo - 1 + tm, :]
    gm2 = gbuf_ref[halo - 2:halo - 2 + tm, :]
    cw = cw_ref[...]
    gc = cw[0:1, :] * gm2 + cw[1:2, :] * gm1 + cw[2:3, :] * g + cb_ref[...]
    y = (gc / (1.0 + jnp.exp(-gc)) * u).astype(BF16)
    acc_ref[...] += jnp.dot(y, wd_ref[...], preferred_element_type=F32)

    @pl.when(f == nf - 1)
    def _():
        res = x1_ref[...] + acc_ref[...]
        if final_norm:
            res = _rms(res, fg_ref[...])
        out_ref[...] = res


def _out_ffn(x2d, o1, o2, wo, g, wg, wu, cw, cb, wd, final_g, seq_len):
    n, d = x2d.shape
    half = o1.shape[1]
    dff = wg.shape[1]
    tm, tf = TM_FFN, TF_FFN
    nf = dff // tf
    final_norm = final_g is not None
    once = dict(pipeline_mode=pl.Buffered(1))
    per_f = once if nf == 1 else {}
    in_specs = [
        pl.BlockSpec((tm, d), lambda i, f: (i, 0)),
        pl.BlockSpec((tm, half), lambda i, f: (i, 0)),
        pl.BlockSpec((tm, half), lambda i, f: (i, 0)),
        pl.BlockSpec((d, d), lambda i, f: (0, 0), **once),
        pl.BlockSpec((1, d), lambda i, f: (0, 0)),
        pl.BlockSpec((d, tf), lambda i, f: (0, f), **per_f),
        pl.BlockSpec((d, tf), lambda i, f: (0, f), **per_f),
        pl.BlockSpec((3, tf), lambda i, f: (0, f)),
        pl.BlockSpec((1, tf), lambda i, f: (0, f)),
        pl.BlockSpec((tf, d), lambda i, f: (f, 0), **per_f),
    ]
    args = [x2d, o1, o2, wo, g, wg, wu, cw, cb, wd]
    if final_norm:
        in_specs.append(pl.BlockSpec((1, d), lambda i, f: (0, 0)))
        args.append(final_g)
    return pl.pallas_call(
        functools.partial(_out_ffn_kernel, tiles_per_seq=seq_len // tm, final_norm=final_norm),
        grid=(n // tm, nf),
        in_specs=in_specs,
        out_specs=pl.BlockSpec((tm, d), lambda i, f: (i, 0)),
        out_shape=jax.ShapeDtypeStruct((n, d), F32),
        scratch_shapes=[
            pltpu.VMEM((tm, d), F32),
            pltpu.VMEM((tm, d), BF16),
            pltpu.VMEM((tm, d), F32),
            pltpu.VMEM((tm + SUBLANES, tf), F32),
            pltpu.VMEM((nf, SUBLANES, tf), F32),
        ],
        compiler_params=pltpu.CompilerParams(
            dimension_semantics=("arbitrary", "arbitrary"), vmem_limit_bytes=VMEM_LIMIT),
        name="out_ffn_final" if final_norm else "out_ffn",
    )(*args)


def _rope_tables(seq_len):
    half = QK_ROPE // 2
    inv = ROPE_THETA ** (-jnp.arange(half, dtype=F32) / half)
    ang = jnp.arange(seq_len, dtype=F32)[:, None] * inv[None, :]
    cos, sin = jnp.cos(ang), jnp.sin(ang)
    ones = jnp.ones((seq_len, QK_NOPE), F32)
    zeros = jnp.zeros((seq_len, QK_NOPE), F32)
    pad1 = jnp.ones((seq_len, LANES - QK_NOPE - QK_ROPE), F32)
    pad0 = jnp.zeros((seq_len, LANES - QK_NOPE - QK_ROPE), F32)
    return (jnp.concatenate([ones, cos, cos, pad1], axis=1),
            jnp.concatenate([zeros, -sin, sin, pad0], axis=1))


def _pad_heads(w, width):
    rows = w.shape[0]
    return jnp.zeros((rows, N_HEADS, LANES), w.dtype).at[:, :, :width].set(
        w.reshape(rows, N_HEADS, width)).reshape(rows, N_HEADS * LANES)


def _cd_weights(w_in, w_uq, w_ukv, b_f):
    d = w_in.shape[0]
    nd = N_HEADS * HEAD_DIM
    o = Q_RANK + KV_RANK + QK_ROPE
    c_q, c_kv, k_rope = w_in[:, :Q_RANK], w_in[:, Q_RANK:Q_RANK + KV_RANK], w_in[:, Q_RANK + KV_RANK:o]
    q_d, k_d, v_d, f_logit = (w_in[:, o:o + nd], w_in[:, o + nd:o + 2 * nd], w_in[:, o + 2 * nd:o + 3 * nd],
                              w_in[:, o + 3 * nd:])
    kr_blk = jnp.zeros((d, LANES), w_in.dtype).at[:, QK_NOPE:QK_NOPE + QK_ROPE].set(k_rope)
    f_blk = jnp.zeros((d, LANES), w_in.dtype).at[:, :N_HEADS].set(f_logit)
    w1 = jnp.concatenate([c_q, c_kv, kr_blk, q_d, k_d, v_d, f_blk], axis=1).astype(BF16)
    wuq = _pad_heads(w_uq, QK_NOPE + QK_ROPE).astype(BF16)
    ukv = w_ukv.reshape(KV_RANK, N_HEADS, QK_NOPE + HEAD_DIM)
    wk = _pad_heads(ukv[:, :, :QK_NOPE].reshape(KV_RANK, N_HEADS * QK_NOPE), QK_NOPE).astype(BF16)
    wv = ukv[:, :, QK_NOPE:].reshape(KV_RANK, N_HEADS * HEAD_DIM).astype(BF16)
    bf = jnp.zeros((1, LANES), F32).at[0, :N_HEADS].set(b_f.astype(F32))
    return w1, wuq, wk, wv, bf


def _forget_selectors():
    selq = np.zeros((LANES, _NPAD), np.float32)
    selk = np.zeros((LANES, _NPAD), np.float32)
    oneq = np.zeros((1, _NPAD), np.float32)
    onek = np.zeros((1, _NPAD), np.float32)
    hsel = np.zeros((_NPAD, LANES), np.float32)
    for h in range(N_HEADS):
        hsel[h * LANES:(h + 1) * LANES, h] = 1.0
        for j in range(N_PIECES):
            selq[j * N_HEADS + h, h * LANES + F_LANE0 + j] = 1.0
            onek[0, h * LANES + F_LANE0 + j] = 1.0
            selk[j * N_HEADS + h, h * LANES + ONE_LANE0 + j] = -1.0
            oneq[0, h * LANES + ONE_LANE0 + j] = 1.0
    return (jnp.asarray(selq, BF16), jnp.asarray(selk, BF16), jnp.asarray(oneq), jnp.asarray(onek),
            jnp.asarray(hsel, BF16))


def kernel(x, ab_norm, ab_w_in, ab_rel_bias, ab_w_o, cd_norm, cd_w_in, cd_q_norm, cd_w_uq, cd_kv_norm, cd_w_ukv,
           cd_b_f, cd_w_o, ffn_norm, ffn_w_gate, ffn_w_up, ffn_conv_w, ffn_conv_b, ffn_w_down, final_norm):
    b, s, d = x.shape
    n = b * s

    qscale = jnp.full((N_HEADS * HEAD_DIM,), HEAD_DIM ** -0.5 * LOG2E, F32)
    one = jnp.ones((2 * N_HEADS * HEAD_DIM,), F32)
    colscale = jnp.concatenate([qscale, one, qscale, one])[None, :]
    proj, vat, vbt = _ab_proj(x, ab_norm[0][None, :], ab_w_in[0].astype(BF16), colscale)
    oa = _chunk_attn(proj, vat, _chunk_bias_tiles(ab_rel_bias[0]))
    tri = jnp.asarray(np.triu(np.ones((TK, TK), np.float32), k=1), BF16)
    ob = _stick_attn(proj, vbt, tri)
    x2d = _out_ffn(x.reshape(n, d), oa.reshape(n, -1), ob.reshape(n, -1), ab_w_o[0].astype(BF16),
                   ffn_norm[0][None, :], ffn_w_gate[0].astype(BF16), ffn_w_up[0].astype(BF16), ffn_conv_w[0],
                   ffn_conv_b[0][None, :], ffn_w_down[0].astype(BF16), None, s)

    w1, wuq, wk, wv, bf = _cd_weights(cd_w_in[0], cd_w_uq[0], cd_w_ukv[0], cd_b_f[0])
    cos_t, sin_t = _rope_tables(s)
    qc, kc, vct, qd, kd, vdt, knorm, fend = _cd_proj(
        x2d.reshape(b, s, d), cd_norm[0][None, :], w1, cd_q_norm[0][None, :], wuq, cd_kv_norm[0][None, :], wk, wv, bf,
        cos_t, sin_t, *_forget_selectors())
    oc = _softmax_attn(qc, kc, vct, True, "mla_attn")
    od = _softmax_attn(qd, kd, vdt, False, "fox_attn", (knorm[:, :, 0, :N_HEADS], fend[:, :, 0, :N_HEADS]))
    out = _out_ffn(x2d, oc.reshape(n, -1), od.reshape(n, -1), cd_w_o[0].astype(BF16), ffn_norm[1][None, :],
                   ffn_w_gate[1].astype(BF16), ffn_w_up[1].astype(BF16), ffn_conv_w[1], ffn_conv_b[1][None, :],
                   ffn_w_down[1].astype(BF16), final_norm[None, :], s)
    return out.reshape(b, s, d)
```

```python
import functools
import math

import numpy as np
import jax
import jax.numpy as jnp
from jax import lax
from jax.experimental import pallas as pl
from jax.experimental.pallas import tpu as pltpu

F32 = jnp.float32
BF16 = jnp.bfloat16

D_MODEL = 1024
HEAD_DIM = 64
CHUNK = 64
LEFT_CHUNKS = 8
BAND = (LEFT_CHUNKS + 1) * CHUNK
MAX_REL = 128
N_HEADS = 8
N_PAIRS = N_HEADS // 2
QK_NOPE = 64
QK_ROPE = 32
Q_RANK = 384
KV_RANK = 256
ROPE_THETA = 10000.0
D_FF = 2816
RMS_EPS = 1e-6

LANES = 128
SUBLANES = 8
LOG2E = math.log2(math.e)
NEG_BIG = -1e30
STICK_UNDERFLOW_LOG2 = 200.0
SOFTMAX_UNDERFLOW_LOG2 = 160.0
NORM_MARGIN = 1.01
SAFE_DENOM_LOG2 = 60.0
VMEM_LIMIT = 52 * 1024 * 1024

TM_PROJ = 256
TM_FFN = 256
TF_FFN = 2816
TK = 256
TQ = 256
N_WIN_A = LEFT_CHUNKS * CHUNK // TK + 1
HEADS_PER_STEP = 8
TILES_PER_STEP = 2
SUM_ROWS = 16

N_PIECES = 3
F_LANE0 = HEAD_DIM
ONE_LANE0 = HEAD_DIM + N_PIECES

NT_DIMS = (((1,), (1,)), ((), ()))


def _rms(x, g):
    ms = jnp.mean(x * x, axis=-1, keepdims=True)
    return x * lax.rsqrt(ms + RMS_EPS) * g


def _store_pairs_transposed(vt_ref, v):
    for p in range(N_PAIRS):
        vt_ref[0, p] = v[:, p * LANES:(p + 1) * LANES].T.astype(vt_ref.dtype)


def _ab_proj_kernel(x_ref, g_ref, w_ref, cs_ref, o_ref, vat_ref, vbt_ref):
    h = _rms(x_ref[0], g_ref[...]).astype(BF16)
    p = jnp.dot(h, w_ref[...], preferred_element_type=F32) * cs_ref[...]
    o_ref[0] = p.astype(o_ref.dtype)
    nv = N_HEADS * HEAD_DIM
    _store_pairs_transposed(vat_ref, p[:, 2 * nv:3 * nv])
    _store_pairs_transposed(vbt_ref, p[:, 5 * nv:6 * nv])


def _ab_proj(x, g, w, colscale):
    b, s, d = x.shape
    nc = w.shape[1]
    tm = TM_PROJ
    vt_spec = pl.BlockSpec((1, N_PAIRS, LANES, tm), lambda bi, ti: (bi, 0, 0, ti))
    vt_shape = jax.ShapeDtypeStruct((b, N_PAIRS, LANES, s), BF16)
    return pl.pallas_call(
        _ab_proj_kernel,
        grid=(b, s // tm),
        in_specs=[
            pl.BlockSpec((1, tm, d), lambda bi, ti: (bi, ti, 0)),
            pl.BlockSpec((1, d), lambda bi, ti: (0, 0)),
            pl.BlockSpec((d, nc), lambda bi, ti: (0, 0)),
            pl.BlockSpec((1, nc), lambda bi, ti: (0, 0)),
        ],
        out_specs=[pl.BlockSpec((1, tm, nc), lambda bi, ti: (bi, ti, 0)), vt_spec, vt_spec],
        out_shape=[jax.ShapeDtypeStruct((b, s, nc), BF16), vt_shape, vt_shape],
        compiler_params=pltpu.CompilerParams(
            dimension_semantics=("arbitrary", "arbitrary"), vmem_limit_bytes=VMEM_LIMIT),
        name="ab_norm_proj",
    )(x, g, w, colscale)


_NPAD = N_HEADS * LANES
_C_Q0 = 0
_C_KV0 = _C_Q0 + Q_RANK
_C_KR0 = _C_KV0 + KV_RANK
_C_QD0 = _C_KR0 + LANES
_C_KD0 = _C_QD0 + N_HEADS * HEAD_DIM
_C_VD0 = _C_KD0 + N_HEADS * HEAD_DIM
_C_F0 = _C_VD0 + N_HEADS * HEAD_DIM
_C_END = _C_F0 + LANES


def _cd_proj_kernel(x_ref, g_ref, w1_ref, qn_ref, wuq_ref, kvn_ref, wk_ref, wv_ref, bf_ref, cos_ref, sin_ref,
                    selq_ref, selk_ref, oneq_ref, onek_ref, hsel_ref,
                    qc_ref, kc_ref, vct_ref, qd_ref, kd_ref, vdt_ref, kcnorm_ref, knorm_ref, fend_ref,
                    carry_ref, kcarry_ref, kccarry_ref):
    t = pl.program_id(1)
    tm = x_ref.shape[1]
    h = _rms(x_ref[0], g_ref[...]).astype(BF16)
    p = jnp.dot(h, w1_ref[...], preferred_element_type=F32)

    cq = _rms(p[:, _C_Q0:_C_KV0], qn_ref[...]).astype(BF16)
    ckv = _rms(p[:, _C_KV0:_C_KR0], kvn_ref[...]).astype(BF16)

    cosb = cos_ref[...]
    sinb = sin_ref[...]
    lane = lax.broadcasted_iota(jnp.int32, (tm, LANES), 1)

    def rope(xb):
        partner = jnp.where(lane < QK_NOPE + QK_ROPE // 2, pltpu.roll(xb, LANES - QK_ROPE // 2, 1),
                            pltpu.roll(xb, QK_ROPE // 2, 1))
        return xb * cosb + partner * sinb

    qc = jnp.dot(cq, wuq_ref[...], preferred_element_type=F32) * ((QK_NOPE + QK_ROPE) ** -0.5 * LOG2E)
    kc = jnp.dot(ckv, wk_ref[...], preferred_element_type=F32)
    kr = rope(p[:, _C_KR0:_C_QD0])
    kc_heads = []
    for hh in range(N_HEADS):
        sl = slice(hh * LANES, (hh + 1) * LANES)
        qc_ref[0, :, sl] = rope(qc[:, sl]).astype(BF16)
        kc_heads.append(kc[:, sl] + kr)
        kc_ref[0, :, sl] = kc_heads[hh].astype(BF16)
    kc = jnp.concatenate(kc_heads, axis=1)
    _store_pairs_transposed(vct_ref, jnp.dot(ckv, wv_ref[...], preferred_element_type=F32))
    _store_pairs_transposed(vdt_ref, p[:, _C_VD0:_C_F0])

    fl = p[:, _C_F0:_C_END] + bf_ref[...]
    y = jnp.minimum(fl, 0.0) - jnp.log(1.0 + jnp.exp(-jnp.abs(fl)))
    row = lax.broadcasted_iota(jnp.int32, (tm, LANES), 0)
    sh = 1
    while sh < tm:
        y = y + jnp.where(row >= sh, pltpu.roll(y, sh, 0), 0.0)
        sh *= 2

    @pl.when(t == 0)
    def _():
        carry_ref[...] = jnp.zeros_like(carry_ref)
        kcarry_ref[...] = jnp.zeros_like(kcarry_ref)
        kccarry_ref[...] = jnp.zeros_like(kccarry_ref)

    y = y + carry_ref[0:1, :]
    carry_ref[...] = jnp.broadcast_to(y[tm - 1:tm, :], carry_ref.shape)
    f2 = y * LOG2E

    hi = f2.astype(BF16).astype(F32)
    r1 = f2 - hi
    mid = r1.astype(BF16).astype(F32)
    lo = r1 - mid
    fp = jnp.where(lane < N_HEADS, hi, jnp.where(lane < 2 * N_HEADS, pltpu.roll(mid, N_HEADS, 1),
                                                 pltpu.roll(lo, 2 * N_HEADS, 1))).astype(BF16)
    def one_head_per_block(x2):
        blocks = []
        for hh in range(N_HEADS):
            src = x2[:, (hh // 2) * LANES:(hh // 2 + 1) * LANES]
            if hh % 2:
                src = pltpu.roll(src, HEAD_DIM, 1)
            blocks.append(jnp.where(lane < HEAD_DIM, src, 0.0))
        return jnp.concatenate(blocks, axis=1)

    qd = one_head_per_block(p[:, _C_QD0:_C_KD0]) * (HEAD_DIM ** -0.5 * LOG2E)
    kd = one_head_per_block(p[:, _C_KD0:_C_VD0])
    qd_ref[0] = (qd +jnp.dot(fp, selq_ref[...], preferred_element_type=F32) + oneq_ref[...]).astype(BF16)
    kd_ref[0] = (kd + jnp.dot(fp, selk_ref[...], preferred_element_type=F32) + onek_ref[...]).astype(BF16)

    def running_max_norm(k, carry, out_ref):
        n2 = jnp.dot((k * k).astype(BF16), hsel_ref[...], preferred_element_type=F32)
        kmax = jnp.maximum(carry[0:1, :], jnp.sqrt(jnp.max(n2, axis=0, keepdims=True)) * NORM_MARGIN)
        carry[...] = jnp.broadcast_to(kmax, carry.shape)
        out_ref[0, 0] = jnp.broadcast_to(kmax, out_ref.shape[2:])

    running_max_norm(kc, kccarry_ref, kcnorm_ref)
    running_max_norm(kd, kcarry_ref, knorm_ref)
    fend_ref[0, 0] = jnp.broadcast_to(f2[tm - 1:tm, :], fend_ref.shape[2:])


def _cd_proj(x, g, w1, qn, wuq, kvn, wk, wv, bf, cos_t, sin_t, selq, selk, oneq, onek, hsel):
    b, s, d = x.shape
    tm = TM_PROJ
    assert tm == TK
    const = lambda a: pl.BlockSpec(a.shape, lambda bi, ti: (0,) * a.ndim)
    tok = lambda nc: pl.BlockSpec((1, tm, nc), lambda bi, ti: (bi, ti, 0))
    vt_spec = pl.BlockSpec((1, N_PAIRS, LANES, tm), lambda bi, ti: (bi, 0, 0, ti))
    tab_spec = pl.BlockSpec((1, 1, SUBLANES, LANES), lambda bi, ti: (bi, ti, 0, 0))
    act = jax.ShapeDtypeStruct((b, s, _NPAD), BF16)
    vt = jax.ShapeDtypeStruct((b, N_PAIRS, LANES, s), BF16)
    tab = jax.ShapeDtypeStruct((b, s // tm, SUBLANES, LANES), F32)
    return pl.pallas_call(
        _cd_proj_kernel,
        grid=(b, s // tm),
        in_specs=[
            tok(d), const(g), const(w1), const(qn), const(wuq), const(kvn), const(wk), const(wv), const(bf),
            pl.BlockSpec((tm, LANES), lambda bi, ti: (ti, 0)),
            pl.BlockSpec((tm, LANES), lambda bi, ti: (ti, 0)),
            const(selq), const(selk), const(oneq), const(onek), const(hsel),
        ],
        out_specs=[tok(_NPAD), tok(_NPAD), vt_spec, tok(_NPAD), tok(_NPAD), vt_spec, tab_spec, tab_spec, tab_spec],
        out_shape=[act, act, vt, act, act, vt, tab, tab, tab],
        scratch_shapes=[pltpu.VMEM((SUBLANES, LANES), F32)] * 3,
        compiler_params=pltpu.CompilerParams(
            dimension_semantics=("arbitrary", "arbitrary"), vmem_limit_bytes=VMEM_LIMIT),
        name="cd_norm_proj",
    )(x, g, w1, qn, wuq, kvn, wk, wv, bf, cos_t, sin_t, selq, selk, oneq, onek, hsel)


def _mask_pair_heads(q_ref, q2_ref):
    tq = q_ref.shape[1]
    lane = lax.broadcasted_iota(jnp.int32, (tq, LANES), 1)
    for h in range(q2_ref.shape[0]):
        q = q_ref[0, :, (h // 2) * LANES:(h // 2 + 1) * LANES]
        q2_ref[h] = jnp.where((lane >= HEAD_DIM) == bool(h % 2), q, jnp.zeros_like(q))


def _store_pair_heads(o_ref, outs):
    tq = outs[0].shape[1]
    lane = lax.broadcasted_iota(jnp.int32, (tq, LANES), 1)
    for pr in range(len(outs) // 2):
        o_ref[0, :, pr * LANES:(pr + 1) * LANES] = jnp.where(
            lane < HEAD_DIM, outs[2 * pr].T, outs[2 * pr + 1].T).astype(o_ref.dtype)


def _softmax_tile_update(ss, vts, m_ref, acc_ref):
    heads = range(len(ss))
    m_old = [m_ref[h] for h in heads]
    acc_old = [acc_ref[h] for h in heads]
    m_new = [functools.reduce(jnp.maximum, [jnp.max(s, axis=0, keepdims=True) for s in ss[h]], m_old[h])
             for h in heads]
    alpha = [jnp.exp2(m_old[h] - m_new[h]) for h in heads]
    ps = [[jnp.exp2(s - m_new[h]).astype(BF16) for s in ss[h]] for h in heads]
    ones = jnp.ones((SUM_ROWS, ss[0][0].shape[0]), BF16)
    pvs = [sum(jnp.dot(jnp.concatenate([vt, ones], axis=0), p, preferred_element_type=F32)
               for vt, p in zip(vts[h], ps[h])) for h in heads]
    for h in heads:
        m_ref[h] = m_new[h]
        acc_ref[h] = alpha[h] * acc_old[h] + pvs[h]


def _softmax_tile_update_fixed(ss, vts, m_ref, acc_ref):
    heads = range(len(ss))
    acc_old = [acc_ref[h] for h in heads]
    ps = [[jnp.exp2(s - m_ref[h]).astype(BF16) for s in ss[h]] for h in heads]
    ones = jnp.ones((SUM_ROWS, ss[0][0].shape[0]), BF16)
    pvs = [sum(jnp.dot(jnp.concatenate([vt, ones], axis=0), p, preferred_element_type=F32)
               for vt, p in zip(vts[h], ps[h])) for h in heads]
    for h in heads:
        acc_ref[h] = acc_old[h] + pvs[h]


def _normalised(acc_ref, h):
    return acc_ref[h, 0:LANES, :] / acc_ref[h, LANES:LANES + 1, :]


def _chunk_attn_kernel(q_ref, k_ref, vt_ref, bias_ref, o_ref, q2_ref, m_ref, acc_ref):
    qi = pl.program_id(2)
    tq = q_ref.shape[1]
    heads = range(acc_ref.shape[0])
    _mask_pair_heads(q_ref, q2_ref)
    m_ref[...] = jnp.full_like(m_ref, NEG_BIG)
    acc_ref[...] = jnp.zeros_like(acc_ref)

    def run(js):
        kstarts = [pl.multiple_of((qi - (N_WIN_A - 1) + j) * tq, tq) for j in js]
        ss = [[lax.dot_general(k_ref[0, pl.ds(ks, tq), (h // 2) * LANES:(h // 2 + 1) * LANES], q2_ref[h],
                               NT_DIMS, preferred_element_type=F32) + bias_ref[h, j] for j, ks in zip(js, kstarts)]
              for h in heads]
        vts = [[vt_ref[0, h // 2, :, pl.ds(ks, tq)] for ks in kstarts] for h in heads]
        _softmax_tile_update(ss, vts, m_ref, acc_ref)
        _store_pair_heads(o_ref, [_normalised(acc_ref, h) for h in heads])

    first = jnp.maximum(N_WIN_A - 1 - qi, 0)
    for f in range(N_WIN_A):
        pl.when(first == f)(functools.partial(run, list(range(f, N_WIN_A))))


def _chunk_attn(proj, vt, bias):
    b, s, _ = proj.shape
    nh = HEADS_PER_STEP
    width = nh * HEAD_DIM
    k0 = N_HEADS * HEAD_DIM // width
    resident = dict(pipeline_mode=pl.Buffered(1))
    return pl.pallas_call(
        _chunk_attn_kernel,
        grid=(b, N_HEADS // nh, s // TK),
        in_specs=[
            pl.BlockSpec((1, TK, width), lambda bi, hg, qi: (bi, qi, hg)),
            pl.BlockSpec((1, s, width), lambda bi, hg, qi: (bi, 0, k0 + hg), **resident),
            pl.BlockSpec((1, nh // 2, LANES, s), lambda bi, hg, qi: (bi, hg, 0, 0), **resident),
            pl.BlockSpec((nh, N_WIN_A, TK, TK), lambda bi, hg, qi: (hg, 0, 0, 0), **resident),
        ],
        out_specs=pl.BlockSpec((1, TK, width), lambda bi, hg, qi: (bi, qi, hg)),
        out_shape=jax.ShapeDtypeStruct((b, s, N_HEADS * HEAD_DIM), BF16),
        scratch_shapes=[pltpu.VMEM((nh, TK, LANES), BF16), pltpu.VMEM((nh, 1, TK), F32),
                        pltpu.VMEM((nh, LANES + SUM_ROWS, TK), F32)],
        compiler_params=pltpu.CompilerParams(
            dimension_semantics=("arbitrary", "arbitrary", "arbitrary"), vmem_limit_bytes=VMEM_LIMIT),
        name="chunk_attn",
    )(proj, proj, vt, bias)


def _chunk_bias_tiles(rel_bias):
    h = rel_bias.shape[0]
    nq = TK
    nk = N_WIN_A * TK
    period = 1 << (nq + nk - 1).bit_length()
    u = np.arange(period)
    signed = np.where(u < nk, u, u - period)
    idx = np.clip(LEFT_CHUNKS * CHUNK - signed, -MAX_REL, MAX_REL) + MAX_REL
    v = rel_bias.astype(F32)[:, idx] * LOG2E
    toep = jnp.tile(v, (1, nq))[:, :nq * (period - 1)].reshape(h, nq, period - 1)[:, :, :nk]
    r = np.arange(nq)[:, None]
    off = np.arange(nk)[None, :] - CHUNK * (r // CHUNK)
    in_band = (off >= 0) & (off < BAND)
    bias = jnp.where(in_band[None], toep, NEG_BIG)
    return bias.reshape(h, nq, N_WIN_A, TK).transpose(0, 2, 3, 1)


def _stick_kernel(q_ref, k_ref, vt_ref, tri_ref, o_ref, q2_ref, c_ref, acc_ref):
    qi = pl.program_id(2)
    tq = q_ref.shape[1]
    tk = tri_ref.shape[0]
    heads = range(acc_ref.shape[0])
    pair = lambda h: slice((h // 2) * LANES, (h // 2 + 1) * LANES)
    _mask_pair_heads(q_ref, q2_ref)
    c_ref[...] = jnp.zeros_like(c_ref)
    acc_ref[...] = jnp.zeros_like(acc_ref)
    sign_bit = jnp.uint32(0x80000000)

    def tiles(js, masked):
        nt = range(len(js))
        kstarts = [pl.multiple_of(j * tk, tk) for j in js]
        zs = [[lax.dot_general(k_ref[0, pl.ds(kstarts[t], tk), pair(h)], q2_ref[h], NT_DIMS,
                               preferred_element_type=F32) for t in nt] for h in heads]
        c_old = [c_ref[h] for h in heads]
        acc_old = [acc_ref[h] for h in heads]

        def neg_log_keep(z, t):
            neg_abs = lax.bitcast_convert_type(lax.bitcast_convert_type(z, jnp.uint32) | sign_bit, F32)
            nlk = jnp.maximum(z, 0.0) + jnp.log(1.0 + jnp.exp2(neg_abs)) * LOG2E
            return jnp.where(masks[t], nlk, 0.0) if masked[t] else nlk

        masks = [None] * len(js)
        for t in nt:
            if masked[t]:
                key = kstarts[t] + lax.broadcasted_iota(jnp.int32, (tk, tq), 0)
                qry = qi * tq + lax.broadcasted_iota(jnp.int32, (tk, tq), 1)
                masks[t] = key < qry
        nlk = [[neg_log_keep(zs[h][t], t) for t in nt] for h in heads]
        rs = [[jnp.dot(tri_ref[...], nlk[h][t].astype(BF16), preferred_element_type=F32) for t in nt] for h in heads]
        pvs = []
        c_new = []
        for h in heads:
            c = c_old[h]
            pv = None
            for t in nt:
                w = jnp.exp2((zs[h][t] - nlk[h][t]) - rs[h][t] - c)
                if masked[t]:
                    w = jnp.where(masks[t], w, 0.0)
                d = jnp.dot(vt_ref[0, h // 2, :, pl.ds(kstarts[t], tk)], w.astype(BF16),
                            preferred_element_type=F32)
                pv = d if pv is None else pv + d
                c = c + jnp.sum(nlk[h][t], axis=0, keepdims=True)
            pvs.append(pv)
            c_new.append(c)
        for h in heads:
            c_ref[h] = c_new[h]
            acc_ref[h] = acc_old[h] + pvs[h]

    assert tq == tk
    pl.when(qi > 0)(lambda: tiles([qi, qi - 1], [True, False]))
    pl.when(qi == 0)(lambda: tiles([qi], [True]))

    def more(j):
        return jnp.logical_and(j >= 0, jnp.min(c_ref[...]) < STICK_UNDERFLOW_LOG2)

    def body(j):
        tiles([j], [False])
        return j - 1

    lax.while_loop(more, body, qi - 2)
    _store_pair_heads(o_ref, [acc_ref[h] for h in heads])


def _stick_attn(proj, vt, tri):
    b, s, _ = proj.shape
    nh = HEADS_PER_STEP
    width = nh * HEAD_DIM
    q0 = 3 * N_HEADS * HEAD_DIM // width
    k0 = 4 * N_HEADS * HEAD_DIM // width
    resident = dict(pipeline_mode=pl.Buffered(1))
    return pl.pallas_call(
        _stick_kernel,
        grid=(b, N_HEADS // nh, s // TQ),
        in_specs=[
            pl.BlockSpec((1, TQ, width), lambda bi, hg, qi: (bi, qi, q0 + hg)),
            pl.BlockSpec((1, s, width), lambda bi, hg, qi: (bi, 0, k0 + hg), **resident),
            pl.BlockSpec((1, nh // 2, LANES, s), lambda bi, hg, qi: (bi, hg, 0, 0), **resident),
            pl.BlockSpec((TK, TK), lambda bi, hg, qi: (0, 0)),
        ],
        out_specs=pl.BlockSpec((1, TQ, width), lambda bi, hg, qi: (bi, qi, hg)),
        out_shape=jax.ShapeDtypeStruct((b, s, N_HEADS * HEAD_DIM), BF16),
        scratch_shapes=[pltpu.VMEM((nh, TQ, LANES), BF16), pltpu.VMEM((nh, 1, TQ), F32),
                        pltpu.VMEM((nh, LANES, TQ), F32)],
        compiler_params=pltpu.CompilerParams(
            dimension_semantics=("arbitrary", "arbitrary", "arbitrary"), vmem_limit_bytes=VMEM_LIMIT),
        name="stick_attn",
    )(proj, proj, vt, tri)


def _softmax_attn_kernel(*refs, chunk_mask, decay_skip):
    if decay_skip:
        q_ref, k_ref, vt_ref, knorm_ref, fend_ref, o_ref, m_ref, acc_ref = refs
    else:
        q_ref, k_ref, vt_ref, knorm_ref, o_ref, m_ref, acc_ref = refs
    bi = pl.program_id(0)
    qi = pl.program_id(2)
    tq = q_ref.shape[1]
    tk = TK
    heads = range(q_ref.shape[2] // LANES)
    sl = lambda h: slice(h * LANES, (h + 1) * LANES)
    assert tq == tk
    g = TILES_PER_STEP
    rem = qi % g

    lane = lax.broadcasted_iota(jnp.int32, (SUBLANES, LANES), 1)
    dims = jnp.where(lane < (HEAD_DIM if decay_skip else LANES), 1.0, 0.0).astype(BF16)
    pieces = jnp.where((lane >= F_LANE0) & (lane < F_LANE0 + N_PIECES), 1.0, 0.0).astype(BF16)
    qnorm, fq = [], []
    for h in heads:
        q = q_ref[0, :, sl(h)]
        q32 = q.astype(F32)
        n2 = lax.dot_general(dims, (q32 * q32).astype(BF16), NT_DIMS, preferred_element_type=F32)[0:1]
        qnorm.append(jnp.sqrt(n2) * NORM_MARGIN)
        if decay_skip:
            fq.append(lax.dot_general(pieces, q, NT_DIMS, preferred_element_type=F32)[0:1])

    def run(fixed):
        update = _softmax_tile_update_fixed if fixed else _softmax_tile_update
        acc_ref[...] = jnp.zeros_like(acc_ref)
        for h in heads:
            m_ref[h] = (qnorm[h] * knorm_ref[bi, qi, h] + 1.0) if fixed else jnp.full((1, tq), NEG_BIG, F32)

        def tiles(js, masked):
            kstarts = [pl.multiple_of(j * tk, tk) for j in js]

            def scores(h, t):
                s = lax.dot_general(k_ref[0, pl.ds(kstarts[t], tk), sl(h)], q_ref[0, :, sl(h)], NT_DIMS,
                                    preferred_element_type=F32)
                if masked[t]:
                    key = kstarts[t] + lax.broadcasted_iota(jnp.int32, (tk, tq), 0)
                    qry = qi * tq + lax.broadcasted_iota(jnp.int32, (tk, tq), 1)
                    s = jnp.where((key // CHUNK <= qry // CHUNK) if chunk_mask else (key <= qry), s, NEG_BIG)
                return s

            ss = [[scores(h, t) for t in range(len(js))] for h in heads]
            vts = [[vt_ref[0, h // 2, :, pl.ds(ks, tk)] for ks in kstarts] for h in heads]
            update(ss, vts, m_ref, acc_ref)

        def diagonal_step():
            for r in range(g):
                pl.when(rem == r)(functools.partial(tiles, [qi - r + t for t in range(r + 1)], [False] * r + [True]))

        if not decay_skip:
            def body(i, carry):
                tiles([g * i + t for t in range(g)], [False] * g)
                return carry

            lax.fori_loop(0, qi // g, body, 0)
            diagonal_step()
        else:
            diagonal_step()
            top = qi - 1 - rem

            def more(it):
                j = jnp.maximum(top - g * it, 0)
                if fixed:
                    level = [m_ref[h] + jnp.log2(acc_ref[h, LANES:LANES + 1, :]) for h in heads]
                else:
                    level = [m_ref[h] for h in heads]
                gap = [qnorm[h] * knorm_ref[bi, j, h] + fq[h] - fend_ref[bi, j, h] - level[h] for h in heads]
                reach = jnp.max(functools.reduce(jnp.maximum, gap))
                return jnp.logical_and(it < qi // g, reach > -SOFTMAX_UNDERFLOW_LOG2)

            def body(it):
                tiles([top - g * it - t for t in range(g)], [False] * g)
                return it + 1

            lax.while_loop(more, body, 0)
        _store_pair_heads(o_ref, [_normalised(acc_ref, h) for h in heads])

    run(True)
    smallest = jnp.min(functools.reduce(jnp.minimum, [acc_ref[h, LANES:LANES + 1, :] for h in heads]))
    pl.when(jnp.logical_not(smallest >= 2.0 ** -SAFE_DENOM_LOG2))(lambda: run(False))


def _softmax_attn(q, k, vt, knorm, chunk_mask, name, fend=None):
    b, s, _ = q.shape
    nh = HEADS_PER_STEP
    assert nh == N_HEADS
    resident = dict(pipeline_mode=pl.Buffered(1))
    tables = [knorm] + ([] if fend is None else [fend])
    return pl.pallas_call(
        functools.partial(_softmax_attn_kernel, chunk_mask=chunk_mask, decay_skip=fend is not None),
        grid=(b, N_HEADS // nh, s // TQ),
        in_specs=[
            pl.BlockSpec((1, TQ, nh * LANES), lambda bi, hg, qi: (bi, qi, hg)),
            pl.BlockSpec((1, s, nh * LANES), lambda bi, hg, qi: (bi, 0, hg), **resident),
            pl.BlockSpec((1, nh // 2, LANES, s), lambda bi, hg, qi: (bi, hg, 0, 0), **resident),
        ] + [pl.BlockSpec(memory_space=pltpu.SMEM)] * len(tables),
        out_specs=pl.BlockSpec((1, TQ, nh * HEAD_DIM), lambda bi, hg, qi: (bi, qi, hg)),
        out_shape=jax.ShapeDtypeStruct((b, s, N_HEADS * HEAD_DIM), BF16),
        scratch_shapes=[pltpu.VMEM((nh, 1, TQ), F32), pltpu.VMEM((nh, LANES + SUM_ROWS, TQ), F32)],
        compiler_params=pltpu.CompilerParams(
            dimension_semantics=("arbitrary", "arbitrary", "arbitrary"), vmem_limit_bytes=VMEM_LIMIT),
        name=name,
    )(q, k, vt, *tables)


def _out_ffn_kernel(*refs, tiles_per_seq, final_norm):
    if final_norm:
        (x_ref, o1_ref, o2_ref, wo_ref, g_ref, wg_ref, wu_ref, cw_ref, cb_ref, wd_ref, fg_ref,
         out_ref, x1_ref, h_ref, acc_ref, gbuf_ref, tail_ref) = refs
    else:
        (x_ref, o1_ref, o2_ref, wo_ref, g_ref, wg_ref, wu_ref, cw_ref, cb_ref, wd_ref,
         out_ref, x1_ref, h_ref, acc_ref, gbuf_ref, tail_ref) = refs
    i = pl.program_id(0)
    f = pl.program_id(1)
    nf = pl.num_programs(1)
    tm = x_ref.shape[0]
    half = o1_ref.shape[1]

    @pl.when(f == 0)
    def _():
        x1 = (x_ref[...]
              + jnp.dot(o1_ref[...], wo_ref[0:half, :], preferred_element_type=F32)
              + jnp.dot(o2_ref[...], wo_ref[half:2 * half, :], preferred_element_type=F32))
        x1_ref[...] = x1
        h_ref[...] = _rms(x1, g_ref[...]).astype(BF16)
        acc_ref[...] = jnp.zeros_like(acc_ref)

    h = h_ref[...]
    g = jnp.dot(h, wg_ref[...], preferred_element_type=F32)
    u = jnp.dot(h, wu_ref[...], preferred_element_type=F32)

    prev = jnp.where(i % tiles_per_seq == 0, 0.0, tail_ref[f])
    halo = SUBLANES
    gbuf_ref[0:halo, :] = prev
    gbuf_ref[halo:halo + tm, :] = g
    tail_ref[f] = g[tm - halo:tm, :]
    gm1 = gbuf_ref[halo - 1:halo - 1 + tm, :]
    gm2 = gbuf_ref[halo - 2:halo - 2 + tm, :]
    cw = cw_ref[...]
    gc = cw[0:1, :] * gm2 + cw[1:2, :] * gm1 + cw[2:3, :] * g + cb_ref[...]
    y = (gc / (1.0 + jnp.exp(-gc)) * u).astype(BF16)
    acc_ref[...] += jnp.dot(y, wd_ref[...], preferred_element_type=F32)

    @pl.when(f == nf - 1)
    def _():
        res = x1_ref[...] + acc_ref[...]
        if final_norm:
            res = _rms(res, fg_ref[...])
        out_ref[...] = res


def _out_ffn(x2d, o1, o2, wo, g, wg, wu, cw, cb, wd, final_g, seq_len):
    n, d = x2d.shape
    half = o1.shape[1]
    dff = wg.shape[1]
    tm, tf = TM_FFN, TF_FFN
    nf = dff // tf
    final_norm = final_g is not None
    once = dict(pipeline_mode=pl.Buffered(1))
    per_f = once if nf == 1 else {}
    in_specs = [
        pl.BlockSpec((tm, d), lambda i, f: (i, 0)),
        pl.BlockSpec((tm, half), lambda i, f: (i, 0)),
        pl.BlockSpec((tm, half), lambda i, f: (i, 0)),
        pl.BlockSpec((d, d), lambda i, f: (0, 0), **once),
        pl.BlockSpec((1, d), lambda i, f: (0, 0)),
        pl.BlockSpec((d, tf), lambda i, f: (0, f), **per_f),
        pl.BlockSpec((d, tf), lambda i, f: (0, f), **per_f),
        pl.BlockSpec((3, tf), lambda i, f: (0, f)),
        pl.BlockSpec((1, tf), lambda i, f: (0, f)),
        pl.BlockSpec((tf, d), lambda i, f: (f, 0), **per_f),
    ]
    args = [x2d, o1, o2, wo, g, wg, wu, cw, cb, wd]
    if final_norm:
        in_specs.append(pl.BlockSpec((1, d), lambda i, f: (0, 0)))
        args.append(final_g)
    return pl.pallas_call(
        functools.partial(_out_ffn_kernel, tiles_per_seq=seq_len // tm, final_norm=final_norm),
        grid=(n // tm, nf),
        in_specs=in_specs,
        out_specs=pl.BlockSpec((tm, d), lambda i, f: (i, 0)),
        out_shape=jax.ShapeDtypeStruct((n, d), F32),
        scratch_shapes=[
            pltpu.VMEM((tm, d), F32),
            pltpu.VMEM((tm, d), BF16),
            pltpu.VMEM((tm, d), F32),
            pltpu.VMEM((tm + SUBLANES, tf), F32),
            pltpu.VMEM((nf, SUBLANES, tf), F32),
        ],
        compiler_params=pltpu.CompilerParams(
            dimension_semantics=("arbitrary", "arbitrary"), vmem_limit_bytes=VMEM_LIMIT),
        name="out_ffn_final" if final_norm else "out_ffn",
    )(*args)


def _rope_tables(seq_len):
    half = QK_ROPE // 2
    inv = ROPE_THETA ** (-jnp.arange(half, dtype=F32) / half)
    ang = jnp.arange(seq_len, dtype=F32)[:, None] * inv[None, :]
    cos, sin = jnp.cos(ang), jnp.sin(ang)
    ones = jnp.ones((seq_len, QK_NOPE), F32)
    zeros = jnp.zeros((seq_len, QK_NOPE), F32)
    pad1 = jnp.ones((seq_len, LANES - QK_NOPE - QK_ROPE), F32)
    pad0 = jnp.zeros((seq_len, LANES - QK_NOPE - QK_ROPE), F32)
    return (jnp.concatenate([ones, cos, cos, pad1], axis=1),
            jnp.concatenate([zeros, -sin, sin, pad0], axis=1))


def _pad_heads(w, width):
    rows = w.shape[0]
    return jnp.zeros((rows, N_HEADS, LANES), w.dtype).at[:, :, :width].set(
        w.reshape(rows, N_HEADS, width)).reshape(rows, N_HEADS * LANES)


def _cd_weights(w_in, w_uq, w_ukv, b_f):
    d = w_in.shape[0]
    nd = N_HEADS * HEAD_DIM
    o = Q_RANK + KV_RANK + QK_ROPE
    c_q, c_kv, k_rope = w_in[:, :Q_RANK], w_in[:, Q_RANK:Q_RANK + KV_RANK], w_in[:, Q_RANK + KV_RANK:o]
    q_d, k_d, v_d, f_logit = (w_in[:, o:o + nd], w_in[:, o + nd:o + 2 * nd], w_in[:, o + 2 * nd:o + 3 * nd],
                              w_in[:, o + 3 * nd:])
    kr_blk = jnp.zeros((d, LANES), w_in.dtype).at[:, QK_NOPE:QK_NOPE + QK_ROPE].set(k_rope)
    f_blk = jnp.zeros((d, LANES), w_in.dtype).at[:, :N_HEADS].set(f_logit)
    w1 = jnp.concatenate([c_q, c_kv, kr_blk, q_d, k_d, v_d, f_blk], axis=1).astype(BF16)
    wuq = _pad_heads(w_uq, QK_NOPE + QK_ROPE).astype(BF16)
    ukv = w_ukv.reshape(KV_RANK, N_HEADS, QK_NOPE + HEAD_DIM)
    wk = _pad_heads(ukv[:, :, :QK_NOPE].reshape(KV_RANK, N_HEADS * QK_NOPE), QK_NOPE).astype(BF16)
    wv = ukv[:, :, QK_NOPE:].reshape(KV_RANK, N_HEADS * HEAD_DIM).astype(BF16)
    bf = jnp.zeros((1, LANES), F32).at[0, :N_HEADS].set(b_f.astype(F32))
    return w1, wuq, wk, wv, bf


def _forget_selectors():
    selq = np.zeros((LANES, _NPAD), np.float32)
    selk = np.zeros((LANES, _NPAD), np.float32)
    oneq = np.zeros((1, _NPAD), np.float32)
    onek = np.zeros((1, _NPAD), np.float32)
    hsel = np.zeros((_NPAD, LANES), np.float32)
    for h in range(N_HEADS):
        hsel[h * LANES:(h + 1) * LANES, h] = 1.0
        for j in range(N_PIECES):
            selq[j * N_HEADS + h, h * LANES + F_LANE0 + j] = 1.0
            onek[0, h * LANES + F_LANE0 + j] = 1.0
            selk[j * N_HEADS + h, h * LANES + ONE_LANE0 + j] = -1.0
            oneq[0, h * LANES + ONE_LANE0 + j] = 1.0
    return (jnp.asarray(selq, BF16), jnp.asarray(selk, BF16), jnp.asarray(oneq), jnp.asarray(onek),
            jnp.asarray(hsel, BF16))


def kernel(x, ab_norm, ab_w_in, ab_rel_bias, ab_w_o, cd_norm, cd_w_in, cd_q_norm, cd_w_uq, cd_kv_norm, cd_w_ukv,
           cd_b_f, cd_w_o, ffn_norm, ffn_w_gate, ffn_w_up, ffn_conv_w, ffn_conv_b, ffn_w_down, final_norm):
    b, s, d = x.shape
    n = b * s

    qscale = jnp.full((N_HEADS * HEAD_DIM,), HEAD_DIM ** -0.5 * LOG2E, F32)
    one = jnp.ones((2 * N_HEADS * HEAD_DIM,), F32)
    colscale = jnp.concatenate([qscale, one, qscale, one])[None, :]
    proj, vat, vbt = _ab_proj(x, ab_norm[0][None, :], ab_w_in[0].astype(BF16), colscale)
    oa = _chunk_attn(proj, vat, _chunk_bias_tiles(ab_rel_bias[0]))
    tri = jnp.asarray(np.triu(np.ones((TK, TK), np.float32), k=1), BF16)
    ob = _stick_attn(proj, vbt, tri)
    x2d = _out_ffn(x.reshape(n, d), oa.reshape(n, -1), ob.reshape(n, -1), ab_w_o[0].astype(BF16),
                   ffn_norm[0][None, :], ffn_w_gate[0].astype(BF16), ffn_w_up[0].astype(BF16), ffn_conv_w[0],
                   ffn_conv_b[0][None, :], ffn_w_down[0].astype(BF16), None, s)

    w1, wuq, wk, wv, bf = _cd_weights(cd_w_in[0], cd_w_uq[0], cd_w_ukv[0], cd_b_f[0])
    cos_t, sin_t = _rope_tables(s)
    qc, kc, vct, qd, kd, vdt, kcnorm, kdnorm, fend = _cd_proj(
        x2d.reshape(b, s, d), cd_norm[0][None, :], w1, cd_q_norm[0][None, :], wuq, cd_kv_norm[0][None, :], wk, wv, bf,
        cos_t, sin_t, *_forget_selectors())
    per_head = lambda tab: tab[:, :, 0, :N_HEADS]
    oc = _softmax_attn(qc, kc, vct, per_head(kcnorm), True, "mla_attn")
    od = _softmax_attn(qd, kd, vdt, per_head(kdnorm), False, "fox_attn", per_head(fend))
    out = _out_ffn(x2d, oc.reshape(n, -1), od.reshape(n, -1), cd_w_o[0].astype(BF16), ffn_norm[1][None, :],
                   ffn_w_gate[1].astype(BF16), ffn_w_up[1].astype(BF16), ffn_conv_w[1], ffn_conv_b[1][None, :],
                   ffn_w_down[1].astype(BF16), final_norm[None, :], s)
    return out.reshape(b, s, d)
```

```python
import functools
import math

import numpy as np
import jax
import jax.numpy as jnp
from jax import lax
from jax.experimental import pallas as pl
from jax.experimental.pallas import tpu as pltpu

F32 = jnp.float32
BF16 = jnp.bfloat16

D_MODEL = 1024
HEAD_DIM = 64
CHUNK = 64
LEFT_CHUNKS = 8
BAND = (LEFT_CHUNKS + 1) * CHUNK
MAX_REL = 128
N_HEADS = 8
N_PAIRS = N_HEADS // 2
QK_NOPE = 64
QK_ROPE = 32
Q_RANK = 384
KV_RANK = 256
ROPE_THETA = 10000.0
D_FF = 2816
RMS_EPS = 1e-6

LANES = 128
SUBLANES = 8
LOG2E = math.log2(math.e)
NEG_BIG = -1e30
STICK_UNDERFLOW_LOG2 = 200.0
SOFTMAX_UNDERFLOW_LOG2 = 160.0
NORM_MARGIN = 1.01
SAFE_DENOM_LOG2 = 60.0
VMEM_LIMIT = 52 * 1024 * 1024

TM_PROJ = 256
TM_FFN = 256
TF_FFN = 2816
TK = 256
TQ = 256
N_WIN_A = LEFT_CHUNKS * CHUNK // TK + 1
HEADS_PER_STEP = 8
TILES_PER_STEP = 2
SUM_ROWS = 16

N_PIECES = 3
F_LANE0 = HEAD_DIM
ONE_LANE0 = HEAD_DIM + N_PIECES

NT_DIMS = (((1,), (1,)), ((), ()))


def _rms(x, g):
    ms = jnp.mean(x * x, axis=-1, keepdims=True)
    return x * lax.rsqrt(ms + RMS_EPS) * g


def _store_pairs_transposed(vt_ref, v):
    for p in range(N_PAIRS):
        vt_ref[0, p] = v[:, p * LANES:(p + 1) * LANES].T.astype(vt_ref.dtype)


def _running_max_norm(k, hsel_ref, carry_ref, out_ref):
    n2 = jnp.dot((k * k).astype(BF16), hsel_ref[...], preferred_element_type=F32)
    kmax = jnp.maximum(carry_ref[0:1, :], jnp.sqrt(jnp.max(n2, axis=0, keepdims=True)) * NORM_MARGIN)
    carry_ref[...] = jnp.broadcast_to(kmax, carry_ref.shape)
    out_ref[0, 0] = jnp.broadcast_to(kmax, out_ref.shape[2:])


def _ab_proj_kernel(x_ref, g_ref, w_ref, cs_ref, hsel_ref, o_ref, vat_ref, vbt_ref, kanorm_ref, kcarry_ref):
    h = _rms(x_ref[0], g_ref[...]).astype(BF16)
    p = jnp.dot(h, w_ref[...], preferred_element_type=F32) * cs_ref[...]
    o_ref[0] = p.astype(o_ref.dtype)
    nv = N_HEADS * HEAD_DIM
    _store_pairs_transposed(vat_ref, p[:, 2 * nv:3 * nv])
    _store_pairs_transposed(vbt_ref, p[:, 5 * nv:6 * nv])

    @pl.when(pl.program_id(1) == 0)
    def _():
        kcarry_ref[...] = jnp.zeros_like(kcarry_ref)

    _running_max_norm(p[:, nv:2 * nv], hsel_ref, kcarry_ref, kanorm_ref)


def _ab_proj(x, g, w, colscale, hsel):
    b, s, d = x.shape
    nc = w.shape[1]
    tm = TM_PROJ
    assert tm == TK
    vt_spec = pl.BlockSpec((1, N_PAIRS, LANES, tm), lambda bi, ti: (bi, 0, 0, ti))
    vt_shape = jax.ShapeDtypeStruct((b, N_PAIRS, LANES, s), BF16)
    return pl.pallas_call(
        _ab_proj_kernel,
        grid=(b, s // tm),
        in_specs=[
            pl.BlockSpec((1, tm, d), lambda bi, ti: (bi, ti, 0)),
            pl.BlockSpec((1, d), lambda bi, ti: (0, 0)),
            pl.BlockSpec((d, nc), lambda bi, ti: (0, 0)),
            pl.BlockSpec((1, nc), lambda bi, ti: (0, 0)),
            pl.BlockSpec(hsel.shape, lambda bi, ti: (0, 0)),
        ],
        out_specs=[pl.BlockSpec((1, tm, nc), lambda bi, ti: (bi, ti, 0)), vt_spec, vt_spec,
                   pl.BlockSpec((1, 1, SUBLANES, LANES), lambda bi, ti: (bi, ti, 0, 0))],
        out_shape=[jax.ShapeDtypeStruct((b, s, nc), BF16), vt_shape, vt_shape,
                   jax.ShapeDtypeStruct((b, s // tm, SUBLANES, LANES), F32)],
        scratch_shapes=[pltpu.VMEM((SUBLANES, LANES), F32)],
        compiler_params=pltpu.CompilerParams(
            dimension_semantics=("arbitrary", "arbitrary"), vmem_limit_bytes=VMEM_LIMIT),
        name="ab_norm_proj",
    )(x, g, w, colscale, hsel)


_NPAD = N_HEADS * LANES
_C_Q0 = 0
_C_KV0 = _C_Q0 + Q_RANK
_C_KR0 = _C_KV0 + KV_RANK
_C_QD0 = _C_KR0 + LANES
_C_KD0 = _C_QD0 + N_HEADS * HEAD_DIM
_C_VD0 = _C_KD0 + N_HEADS * HEAD_DIM
_C_F0 = _C_VD0 + N_HEADS * HEAD_DIM
_C_END = _C_F0 + LANES


def _cd_proj_kernel(x_ref, g_ref, w1_ref, qn_ref, wuq_ref, kvn_ref, wk_ref, wv_ref, bf_ref, cos_ref, sin_ref,
                    selq_ref, selk_ref, oneq_ref, onek_ref, hsel_ref,
                    qc_ref, kc_ref, vct_ref, qd_ref, kd_ref, vdt_ref, kcnorm_ref, knorm_ref, fend_ref,
                    carry_ref, kcarry_ref, kccarry_ref):
    t = pl.program_id(1)
    tm = x_ref.shape[1]
    h = _rms(x_ref[0], g_ref[...]).astype(BF16)
    p = jnp.dot(h, w1_ref[...], preferred_element_type=F32)

    cq = _rms(p[:, _C_Q0:_C_KV0], qn_ref[...]).astype(BF16)
    ckv = _rms(p[:, _C_KV0:_C_KR0], kvn_ref[...]).astype(BF16)

    cosb = cos_ref[...]
    sinb = sin_ref[...]
    lane = lax.broadcasted_iota(jnp.int32, (tm, LANES), 1)

    def rope(xb):
        partner = jnp.where(lane < QK_NOPE + QK_ROPE // 2, pltpu.roll(xb, LANES - QK_ROPE // 2, 1),
                            pltpu.roll(xb, QK_ROPE // 2, 1))
        return xb * cosb + partner * sinb

    qc = jnp.dot(cq, wuq_ref[...], preferred_element_type=F32) * ((QK_NOPE + QK_ROPE) ** -0.5 * LOG2E)
    kc = jnp.dot(ckv, wk_ref[...], preferred_element_type=F32)
    kr = rope(p[:, _C_KR0:_C_QD0])
    kc_heads = []
    for hh in range(N_HEADS):
        sl = slice(hh * LANES, (hh + 1) * LANES)
        qc_ref[0, :, sl] = rope(qc[:, sl]).astype(BF16)
        kc_heads.append(kc[:, sl] + kr)
        kc_ref[0, :, sl] = kc_heads[hh].astype(BF16)
    kc = jnp.concatenate(kc_heads, axis=1)
    _store_pairs_transposed(vct_ref, jnp.dot(ckv, wv_ref[...], preferred_element_type=F32))
    _store_pairs_transposed(vdt_ref, p[:, _C_VD0:_C_F0])

    fl = p[:, _C_F0:_C_END] + bf_ref[...]
    y = jnp.minimum(fl, 0.0) - jnp.log(1.0 + jnp.exp(-jnp.abs(fl)))
    row = lax.broadcasted_iota(jnp.int32, (tm, LANES), 0)
    sh = 1
    while sh < tm:
        y = y + jnp.where(row >= sh, pltpu.roll(y, sh, 0), 0.0)
        sh *= 2

    @pl.when(t == 0)
    def _():
        carry_ref[...] = jnp.zeros_like(carry_ref)
        kcarry_ref[...] = jnp.zeros_like(kcarry_ref)
        kccarry_ref[...] = jnp.zeros_like(kccarry_ref)

    y = y + carry_ref[0:1, :]
    carry_ref[...] = jnp.broadcast_to(y[tm - 1:tm, :], carry_ref.shape)
    f2 = y * LOG2E

    hi = f2.astype(BF16).astype(F32)
    r1 = f2 - hi
    mid = r1.astype(BF16).astype(F32)
    lo = r1 - mid
    fp = jnp.where(lane < N_HEADS, hi, jnp.where(lane < 2 * N_HEADS, pltpu.roll(mid, N_HEADS, 1),
                                                 pltpu.roll(lo, 2 * N_HEADS, 1))).astype(BF16)
    def one_head_per_block(x2):
        blocks = []
        for hh in range(N_HEADS):
            src = x2[:, (hh // 2) * LANES:(hh // 2 + 1) * LANES]
            if hh % 2:
                src = pltpu.roll(src, HEAD_DIM, 1)
            blocks.append(jnp.where(lane < HEAD_DIM, src, 0.0))
        return jnp.concatenate(blocks, axis=1)

    qd = one_head_per_block(p[:, _C_QD0:_C_KD0]) * (HEAD_DIM ** -0.5 * LOG2E)
    kd = one_head_per_block(p[:, _C_KD0:_C_VD0])
    qd_ref[0] = (qd +jnp.dot(fp, selq_ref[...], preferred_element_type=F32) + oneq_ref[...]).astype(BF16)
    kd_ref[0] = (kd + jnp.dot(fp, selk_ref[...], preferred_element_type=F32) + onek_ref[...]).astype(BF16)

    _running_max_norm(kc, hsel_ref, kccarry_ref, kcnorm_ref)
    _running_max_norm(kd, hsel_ref, kcarry_ref, knorm_ref)
    fend_ref[0, 0] = jnp.broadcast_to(f2[tm - 1:tm, :], fend_ref.shape[2:])


def _cd_proj(x, g, w1, qn, wuq, kvn, wk, wv, bf, cos_t, sin_t, selq, selk, oneq, onek, hsel):
    b, s, d = x.shape
    tm = TM_PROJ
    assert tm == TK
    const = lambda a: pl.BlockSpec(a.shape, lambda bi, ti: (0,) * a.ndim)
    tok = lambda nc: pl.BlockSpec((1, tm, nc), lambda bi, ti: (bi, ti, 0))
    vt_spec = pl.BlockSpec((1, N_PAIRS, LANES, tm), lambda bi, ti: (bi, 0, 0, ti))
    tab_spec = pl.BlockSpec((1, 1, SUBLANES, LANES), lambda bi, ti: (bi, ti, 0, 0))
    act = jax.ShapeDtypeStruct((b, s, _NPAD), BF16)
    vt = jax.ShapeDtypeStruct((b, N_PAIRS, LANES, s), BF16)
    tab = jax.ShapeDtypeStruct((b, s // tm, SUBLANES, LANES), F32)
    return pl.pallas_call(
        _cd_proj_kernel,
        grid=(b, s // tm),
        in_specs=[
            tok(d), const(g), const(w1), const(qn), const(wuq), const(kvn), const(wk), const(wv), const(bf),
            pl.BlockSpec((tm, LANES), lambda bi, ti: (ti, 0)),
            pl.BlockSpec((tm, LANES), lambda bi, ti: (ti, 0)),
            const(selq), const(selk), const(oneq), const(onek), const(hsel),
        ],
        out_specs=[tok(_NPAD), tok(_NPAD), vt_spec, tok(_NPAD), tok(_NPAD), vt_spec, tab_spec, tab_spec, tab_spec],
        out_shape=[act, act, vt, act, act, vt, tab, tab, tab],
        scratch_shapes=[pltpu.VMEM((SUBLANES, LANES), F32)] * 3,
        compiler_params=pltpu.CompilerParams(
            dimension_semantics=("arbitrary", "arbitrary"), vmem_limit_bytes=VMEM_LIMIT),
        name="cd_norm_proj",
    )(x, g, w1, qn, wuq, kvn, wk, wv, bf, cos_t, sin_t, selq, selk, oneq, onek, hsel)


def _mask_pair_heads(q_ref, q2_ref):
    tq = q_ref.shape[1]
    lane = lax.broadcasted_iota(jnp.int32, (tq, LANES), 1)
    for h in range(q2_ref.shape[0]):
        q = q_ref[0, :, (h // 2) * LANES:(h // 2 + 1) * LANES]
        q2_ref[h] = jnp.where((lane >= HEAD_DIM) == bool(h % 2), q, jnp.zeros_like(q))


def _store_pair_heads(o_ref, outs):
    tq = outs[0].shape[1]
    lane = lax.broadcasted_iota(jnp.int32, (tq, LANES), 1)
    for pr in range(len(outs) // 2):
        o_ref[0, :, pr * LANES:(pr + 1) * LANES] = jnp.where(
            lane < HEAD_DIM, outs[2 * pr].T, outs[2 * pr + 1].T).astype(o_ref.dtype)


def _softmax_tile_update(ss, vts, m_ref, acc_ref):
    heads = range(len(ss))
    m_old = [m_ref[h] for h in heads]
    acc_old = [acc_ref[h] for h in heads]
    m_new = [functools.reduce(jnp.maximum, [jnp.max(s, axis=0, keepdims=True) for s in ss[h]], m_old[h])
             for h in heads]
    alpha = [jnp.exp2(m_old[h] - m_new[h]) for h in heads]
    ps = [[jnp.exp2(s - m_new[h]).astype(BF16) for s in ss[h]] for h in heads]
    ones = jnp.ones((SUM_ROWS, ss[0][0].shape[0]), BF16)
    pvs = [sum(jnp.dot(jnp.concatenate([vt, ones], axis=0), p, preferred_element_type=F32)
               for vt, p in zip(vts[h], ps[h])) for h in heads]
    for h in heads:
        m_ref[h] = m_new[h]
        acc_ref[h] = alpha[h] * acc_old[h] + pvs[h]


def _softmax_tile_update_fixed(ss, vts, m_ref, acc_ref):
    heads = range(len(ss))
    acc_old = [acc_ref[h] for h in heads]
    ps = [[jnp.exp2(s - m_ref[h]).astype(BF16) for s in ss[h]] for h in heads]
    ones = jnp.ones((SUM_ROWS, ss[0][0].shape[0]), BF16)
    pvs = [sum(jnp.dot(jnp.concatenate([vt, ones], axis=0), p, preferred_element_type=F32)
               for vt, p in zip(vts[h], ps[h])) for h in heads]
    for h in heads:
        acc_ref[h] = acc_old[h] + pvs[h]


def _normalised(acc_ref, h):
    return acc_ref[h, 0:LANES, :] / acc_ref[h, LANES:LANES + 1, :]


def _chunk_attn_kernel(q_ref, k_ref, vt_ref, bias_ref, knorm_ref, bmax_ref, o_ref, q2_ref, m_ref, acc_ref):
    bi = pl.program_id(0)
    qi = pl.program_id(2)
    tq = q_ref.shape[1]
    heads = range(acc_ref.shape[0])
    _mask_pair_heads(q_ref, q2_ref)
    ones = jnp.ones((SUBLANES, LANES), BF16)
    qnorm = []
    for h in heads:
        q32 = q2_ref[h].astype(F32)
        n2 = lax.dot_general(ones, (q32 * q32).astype(BF16), NT_DIMS, preferred_element_type=F32)[0:1]
        qnorm.append(jnp.sqrt(n2) * NORM_MARGIN)

    def run(fixed, js):
        update = _softmax_tile_update_fixed if fixed else _softmax_tile_update
        acc_ref[...] = jnp.zeros_like(acc_ref)
        for h in heads:
            m_ref[h] = ((qnorm[h] * knorm_ref[bi, qi, h] + bmax_ref[h]) if fixed
                        else jnp.full((1, tq), NEG_BIG, F32))
        kstarts = [pl.multiple_of((qi - (N_WIN_A - 1) + j) * tq, tq) for j in js]
        ss = [[lax.dot_general(k_ref[0, pl.ds(ks, tq), (h // 2) * LANES:(h // 2 + 1) * LANES], q2_ref[h],
                               NT_DIMS, preferred_element_type=F32) + bias_ref[h, j] for j, ks in zip(js, kstarts)]
              for h in heads]
        vts = [[vt_ref[0, h // 2, :, pl.ds(ks, tq)] for ks in kstarts] for h in heads]
        update(ss, vts, m_ref, acc_ref)
        _store_pair_heads(o_ref, [_normalised(acc_ref, h) for h in heads])

    def run_window(fixed):
        first = jnp.maximum(N_WIN_A - 1 - qi, 0)
        for f in range(N_WIN_A):
            pl.when(first == f)(functools.partial(run, fixed, list(range(f, N_WIN_A))))

    run_window(True)
    smallest = jnp.min(functools.reduce(jnp.minimum, [acc_ref[h, LANES:LANES + 1, :] for h in heads]))
    pl.when(jnp.logical_not(smallest >= 2.0 ** -SAFE_DENOM_LOG2))(lambda: run_window(False))


def _chunk_attn(proj, vt, bias, knorm, bmax):
    b, s, _ = proj.shape
    nh = HEADS_PER_STEP
    assert nh == N_HEADS
    width = nh * HEAD_DIM
    k0 = N_HEADS * HEAD_DIM // width
    resident = dict(pipeline_mode=pl.Buffered(1))
    return pl.pallas_call(
        _chunk_attn_kernel,
        grid=(b, N_HEADS // nh, s // TK),
        in_specs=[
            pl.BlockSpec((1, TK, width), lambda bi, hg, qi: (bi, qi, hg)),
            pl.BlockSpec((1, s, width), lambda bi, hg, qi: (bi, 0, k0 + hg), **resident),
            pl.BlockSpec((1, nh // 2, LANES, s), lambda bi, hg, qi: (bi, hg, 0, 0), **resident),
            pl.BlockSpec((nh, N_WIN_A, TK, TK), lambda bi, hg, qi: (hg, 0, 0, 0), **resident),
            pl.BlockSpec(memory_space=pltpu.SMEM),
            pl.BlockSpec(memory_space=pltpu.SMEM),
        ],
        out_specs=pl.BlockSpec((1, TK, width), lambda bi, hg, qi: (bi, qi, hg)),
        out_shape=jax.ShapeDtypeStruct((b, s, N_HEADS * HEAD_DIM), BF16),
        scratch_shapes=[pltpu.VMEM((nh, TK, LANES), BF16), pltpu.VMEM((nh, 1, TK), F32),
                        pltpu.VMEM((nh, LANES + SUM_ROWS, TK), F32)],
        compiler_params=pltpu.CompilerParams(
            dimension_semantics=("arbitrary", "arbitrary", "arbitrary"), vmem_limit_bytes=VMEM_LIMIT),
        name="chunk_attn",
    )(proj, proj, vt, bias, knorm, bmax)


def _chunk_bias_tiles(rel_bias):
    h = rel_bias.shape[0]
    nq = TK
    nk = N_WIN_A * TK
    period = 1 << (nq + nk - 1).bit_length()
    u = np.arange(period)
    signed = np.where(u < nk, u, u - period)
    idx = np.clip(LEFT_CHUNKS * CHUNK - signed, -MAX_REL, MAX_REL) + MAX_REL
    v = rel_bias.astype(F32)[:, idx] * LOG2E
    toep = jnp.tile(v, (1, nq))[:, :nq * (period - 1)].reshape(h, nq, period - 1)[:, :, :nk]
    r = np.arange(nq)[:, None]
    off = np.arange(nk)[None, :] - CHUNK * (r // CHUNK)
    in_band = (off >= 0) & (off < BAND)
    bias = jnp.where(in_band[None], toep, NEG_BIG)
    return bias.reshape(h, nq, N_WIN_A, TK).transpose(0, 2, 3, 1)


def _stick_kernel(q_ref, k_ref, vt_ref, tri_ref, o_ref, q2_ref, c_ref, acc_ref):
    qi = pl.program_id(2)
    tq = q_ref.shape[1]
    tk = tri_ref.shape[0]
    heads = range(acc_ref.shape[0])
    pair = lambda h: slice((h // 2) * LANES, (h // 2 + 1) * LANES)
    _mask_pair_heads(q_ref, q2_ref)
    c_ref[...] = jnp.zeros_like(c_ref)
    acc_ref[...] = jnp.zeros_like(acc_ref)
    sign_bit = jnp.uint32(0x80000000)

    def tiles(js, masked):
        nt = range(len(js))
        kstarts = [pl.multiple_of(j * tk, tk) for j in js]
        zs = [[lax.dot_general(k_ref[0, pl.ds(kstarts[t], tk), pair(h)], q2_ref[h], NT_DIMS,
                               preferred_element_type=F32) for t in nt] for h in heads]
        c_old = [c_ref[h] for h in heads]
        acc_old = [acc_ref[h] for h in heads]

        def neg_log_keep(z, t):
            neg_abs = lax.bitcast_convert_type(lax.bitcast_convert_type(z, jnp.uint32) | sign_bit, F32)
            nlk = jnp.maximum(z, 0.0) + jnp.log(1.0 + jnp.exp2(neg_abs)) * LOG2E
            return jnp.where(masks[t], nlk, 0.0) if masked[t] else nlk

        masks = [None] * len(js)
        for t in nt:
            if masked[t]:
                key = kstarts[t] + lax.broadcasted_iota(jnp.int32, (tk, tq), 0)
                qry = qi * tq + lax.broadcasted_iota(jnp.int32, (tk, tq), 1)
                masks[t] = key < qry
        nlk = [[neg_log_keep(zs[h][t], t) for t in nt] for h in heads]
        rs = [[jnp.dot(tri_ref[...], nlk[h][t].astype(BF16), preferred_element_type=F32) for t in nt] for h in heads]
        pvs = []
        c_new = []
        for h in heads:
            c = c_old[h]
            pv = None
            for t in nt:
                w = jnp.exp2((zs[h][t] - nlk[h][t]) - rs[h][t] - c)
                if masked[t]:
                    w = jnp.where(masks[t], w, 0.0)
                d = jnp.dot(vt_ref[0, h // 2, :, pl.ds(kstarts[t], tk)], w.astype(BF16),
                            preferred_element_type=F32)
                pv = d if pv is None else pv + d
                c = c + jnp.sum(nlk[h][t], axis=0, keepdims=True)
            pvs.append(pv)
            c_new.append(c)
        for h in heads:
            c_ref[h] = c_new[h]
            acc_ref[h] = acc_old[h] + pvs[h]

    assert tq == tk
    pl.when(qi > 0)(lambda: tiles([qi, qi - 1], [True, False]))
    pl.when(qi == 0)(lambda: tiles([qi], [True]))

    def more(j):
        return jnp.logical_and(j >= 0, jnp.min(c_ref[...]) < STICK_UNDERFLOW_LOG2)

    def body(j):
        tiles([j], [False])
        return j - 1

    lax.while_loop(more, body, qi - 2)
    _store_pair_heads(o_ref, [acc_ref[h] for h in heads])


def _stick_attn(proj, vt, tri):
    b, s, _ = proj.shape
    nh = HEADS_PER_STEP
    width = nh * HEAD_DIM
    q0 = 3 * N_HEADS * HEAD_DIM // width
    k0 = 4 * N_HEADS * HEAD_DIM // width
    resident = dict(pipeline_mode=pl.Buffered(1))
    return pl.pallas_call(
        _stick_kernel,
        grid=(b, N_HEADS // nh, s // TQ),
        in_specs=[
            pl.BlockSpec((1, TQ, width), lambda bi, hg, qi: (bi, qi, q0 + hg)),
            pl.BlockSpec((1, s, width), lambda bi, hg, qi: (bi, 0, k0 + hg), **resident),
            pl.BlockSpec((1, nh // 2, LANES, s), lambda bi, hg, qi: (bi, hg, 0, 0), **resident),
            pl.BlockSpec((TK, TK), lambda bi, hg, qi: (0, 0)),
        ],
        out_specs=pl.BlockSpec((1, TQ, width), lambda bi, hg, qi: (bi, qi, hg)),
        out_shape=jax.ShapeDtypeStruct((b, s, N_HEADS * HEAD_DIM), BF16),
        scratch_shapes=[pltpu.VMEM((nh, TQ, LANES), BF16), pltpu.VMEM((nh, 1, TQ), F32),
                        pltpu.VMEM((nh, LANES, TQ), F32)],
        compiler_params=pltpu.CompilerParams(
            dimension_semantics=("arbitrary", "arbitrary", "arbitrary"), vmem_limit_bytes=VMEM_LIMIT),
        name="stick_attn",
    )(proj, proj, vt, tri)


def _softmax_attn_kernel(*refs, chunk_mask, decay_skip):
    if decay_skip:
        q_ref, k_ref, vt_ref, knorm_ref, fend_ref, o_ref, m_ref, acc_ref = refs
    else:
        q_ref, k_ref, vt_ref, knorm_ref, o_ref, m_ref, acc_ref = refs
    bi = pl.program_id(0)
    qi = pl.program_id(2)
    tq = q_ref.shape[1]
    tk = TK
    heads = range(q_ref.shape[2] // LANES)
    sl = lambda h: slice(h * LANES, (h + 1) * LANES)
    assert tq == tk
    g = TILES_PER_STEP
    rem = qi % g

    lane = lax.broadcasted_iota(jnp.int32, (SUBLANES, LANES), 1)
    dims = jnp.where(lane < (HEAD_DIM if decay_skip else LANES), 1.0, 0.0).astype(BF16)
    pieces = jnp.where((lane >= F_LANE0) & (lane < F_LANE0 + N_PIECES), 1.0, 0.0).astype(BF16)
    qnorm, fq = [], []
    for h in heads:
        q = q_ref[0, :, sl(h)]
        q32 = q.astype(F32)
        n2 = lax.dot_general(dims, (q32 * q32).astype(BF16), NT_DIMS, preferred_element_type=F32)[0:1]
        qnorm.append(jnp.sqrt(n2) * NORM_MARGIN)
        if decay_skip:
            fq.append(lax.dot_general(pieces, q, NT_DIMS, preferred_element_type=F32)[0:1])

    def run(fixed):
        update = _softmax_tile_update_fixed if fixed else _softmax_tile_update
        acc_ref[...] = jnp.zeros_like(acc_ref)
        for h in heads:
            m_ref[h] = (qnorm[h] * knorm_ref[bi, qi, h] + 1.0) if fixed else jnp.full((1, tq), NEG_BIG, F32)

        def tiles(js, masked):
            kstarts = [pl.multiple_of(j * tk, tk) for j in js]

            def scores(h, t):
                s = lax.dot_general(k_ref[0, pl.ds(kstarts[t], tk), sl(h)], q_ref[0, :, sl(h)], NT_DIMS,
                                    preferred_element_type=F32)
                if masked[t]:
                    key = kstarts[t] + lax.broadcasted_iota(jnp.int32, (tk, tq), 0)
                    qry = qi * tq + lax.broadcasted_iota(jnp.int32, (tk, tq), 1)
                    s = jnp.where((key // CHUNK <= qry // CHUNK) if chunk_mask else (key <= qry), s, NEG_BIG)
                return s

            ss = [[scores(h, t) for t in range(len(js))] for h in heads]
            vts = [[vt_ref[0, h // 2, :, pl.ds(ks, tk)] for ks in kstarts] for h in heads]
            update(ss, vts, m_ref, acc_ref)

        def diagonal_step():
            for r in range(g):
                pl.when(rem == r)(functools.partial(tiles, [qi - r + t for t in range(r + 1)], [False] * r + [True]))

        if not decay_skip:
            def body(i, carry):
                tiles([g * i + t for t in range(g)], [False] * g)
                return carry

            lax.fori_loop(0, qi // g, body, 0)
            diagonal_step()
        else:
            diagonal_step()
            top = qi - 1 - rem

            def more(it):
                j = jnp.maximum(top - g * it, 0)
                if fixed:
                    level = [m_ref[h] + jnp.log2(acc_ref[h, LANES:LANES + 1, :]) for h in heads]
                else:
                    level = [m_ref[h] for h in heads]
                gap = [qnorm[h] * knorm_ref[bi, j, h] + fq[h] - fend_ref[bi, j, h] - level[h] for h in heads]
                reach = jnp.max(functools.reduce(jnp.maximum, gap))
                return jnp.logical_and(it < qi // g, reach > -SOFTMAX_UNDERFLOW_LOG2)

            def body(it):
                tiles([top - g * it - t for t in range(g)], [False] * g)
                return it + 1

            lax.while_loop(more, body, 0)
        _store_pair_heads(o_ref, [_normalised(acc_ref, h) for h in heads])

    run(True)
    smallest = jnp.min(functools.reduce(jnp.minimum, [acc_ref[h, LANES:LANES + 1, :] for h in heads]))
    pl.when(jnp.logical_not(smallest >= 2.0 ** -SAFE_DENOM_LOG2))(lambda: run(False))


def _softmax_attn(q, k, vt, knorm, chunk_mask, name, fend=None):
    b, s, _ = q.shape
    nh = HEADS_PER_STEP
    assert nh == N_HEADS
    resident = dict(pipeline_mode=pl.Buffered(1))
    tables = [knorm] + ([] if fend is None else [fend])
    return pl.pallas_call(
        functools.partial(_softmax_attn_kernel, chunk_mask=chunk_mask, decay_skip=fend is not None),
        grid=(b, N_HEADS // nh, s // TQ),
        in_specs=[
            pl.BlockSpec((1, TQ, nh * LANES), lambda bi, hg, qi: (bi, qi, hg)),
            pl.BlockSpec((1, s, nh * LANES), lambda bi, hg, qi: (bi, 0, hg), **resident),
            pl.BlockSpec((1, nh // 2, LANES, s), lambda bi, hg, qi: (bi, hg, 0, 0), **resident),
        ] + [pl.BlockSpec(memory_space=pltpu.SMEM)] * len(tables),
        out_specs=pl.BlockSpec((1, TQ, nh * HEAD_DIM), lambda bi, hg, qi: (bi, qi, hg)),
        out_shape=jax.ShapeDtypeStruct((b, s, N_HEADS * HEAD_DIM), BF16),
        scratch_shapes=[pltpu.VMEM((nh, 1, TQ), F32), pltpu.VMEM((nh, LANES + SUM_ROWS, TQ), F32)],
        compiler_params=pltpu.CompilerParams(
            dimension_semantics=("arbitrary", "arbitrary", "arbitrary"), vmem_limit_bytes=VMEM_LIMIT),
        name=name,
    )(q, k, vt, *tables)


def _out_ffn_kernel(*refs, tiles_per_seq, final_norm):
    if final_norm:
        (x_ref, o1_ref, o2_ref, wo_ref, g_ref, wg_ref, wu_ref, cw_ref, cb_ref, wd_ref, fg_ref,
         out_ref, x1_ref, h_ref, acc_ref, gbuf_ref, tail_ref) = refs
    else:
        (x_ref, o1_ref, o2_ref, wo_ref, g_ref, wg_ref, wu_ref, cw_ref, cb_ref, wd_ref,
         out_ref, x1_ref, h_ref, acc_ref, gbuf_ref, tail_ref) = refs
    i = pl.program_id(0)
    f = pl.program_id(1)
    nf = pl.num_programs(1)
    tm = x_ref.shape[0]
    half = o1_ref.shape[1]

    @pl.when(f == 0)
    def _():
        x1 = (x_ref[...]
              + jnp.dot(o1_ref[...], wo_ref[0:half, :], preferred_element_type=F32)
              + jnp.dot(o2_ref[...], wo_ref[half:2 * half, :], preferred_element_type=F32))
        x1_ref[...] = x1
        h_ref[...] = _rms(x1, g_ref[...]).astype(BF16)
        acc_ref[...] = jnp.zeros_like(acc_ref)

    h = h_ref[...]
    g = jnp.dot(h, wg_ref[...], preferred_element_type=F32)
    u = jnp.dot(h, wu_ref[...], preferred_element_type=F32)

    prev = jnp.where(i % tiles_per_seq == 0, 0.0, tail_ref[f])
    halo = SUBLANES
    gbuf_ref[0:halo, :] = prev
    gbuf_ref[halo:halo + tm, :] = g
    tail_ref[f] = g[tm - halo:tm, :]
    gm1 = gbuf_ref[halo - 1:halo - 1 + tm, :]
    gm2 = gbuf_ref[halo - 2:halo - 2 + tm, :]
    cw = cw_ref[...]
    gc = cw[0:1, :] * gm2 + cw[1:2, :] * gm1 + cw[2:3, :] * g + cb_ref[...]
    y = (gc / (1.0 + jnp.exp(-gc)) * u).astype(BF16)
    acc_ref[...] += jnp.dot(y, wd_ref[...], preferred_element_type=F32)

    @pl.when(f == nf - 1)
    def _():
        res = x1_ref[...] + acc_ref[...]
        if final_norm:
            res = _rms(res, fg_ref[...])
        out_ref[...] = res


def _out_ffn(x2d, o1, o2, wo, g, wg, wu, cw, cb, wd, final_g, seq_len):
    n, d = x2d.shape
    half = o1.shape[1]
    dff = wg.shape[1]
    tm, tf = TM_FFN, TF_FFN
    nf = dff // tf
    final_norm = final_g is not None
    once = dict(pipeline_mode=pl.Buffered(1))
    per_f = once if nf == 1 else {}
    in_specs = [
        pl.BlockSpec((tm, d), lambda i, f: (i, 0)),
        pl.BlockSpec((tm, half), lambda i, f: (i, 0)),
        pl.BlockSpec((tm, half), lambda i, f: (i, 0)),
        pl.BlockSpec((d, d), lambda i, f: (0, 0), **once),
        pl.BlockSpec((1, d), lambda i, f: (0, 0)),
        pl.BlockSpec((d, tf), lambda i, f: (0, f), **per_f),
        pl.BlockSpec((d, tf), lambda i, f: (0, f), **per_f),
        pl.BlockSpec((3, tf), lambda i, f: (0, f)),
        pl.BlockSpec((1, tf), lambda i, f: (0, f)),
        pl.BlockSpec((tf, d), lambda i, f: (f, 0), **per_f),
    ]
    args = [x2d, o1, o2, wo, g, wg, wu, cw, cb, wd]
    if final_norm:
        in_specs.append(pl.BlockSpec((1, d), lambda i, f: (0, 0)))
        args.append(final_g)
    return pl.pallas_call(
        functools.partial(_out_ffn_kernel, tiles_per_seq=seq_len // tm, final_norm=final_norm),
        grid=(n // tm, nf),
        in_specs=in_specs,
        out_specs=pl.BlockSpec((tm, d), lambda i, f: (i, 0)),
        out_shape=jax.ShapeDtypeStruct((n, d), F32),
        scratch_shapes=[
            pltpu.VMEM((tm, d), F32),
            pltpu.VMEM((tm, d), BF16),
            pltpu.VMEM((tm, d), F32),
            pltpu.VMEM((tm + SUBLANES, tf), F32),
            pltpu.VMEM((nf, SUBLANES, tf), F32),
        ],
        compiler_params=pltpu.CompilerParams(
            dimension_semantics=("arbitrary", "arbitrary"), vmem_limit_bytes=VMEM_LIMIT),
        name="out_ffn_final" if final_norm else "out_ffn",
    )(*args)


def _rope_tables(seq_len):
    half = QK_ROPE // 2
    inv = ROPE_THETA ** (-jnp.arange(half, dtype=F32) / half)
    ang = jnp.arange(seq_len, dtype=F32)[:, None] * inv[None, :]
    cos, sin = jnp.cos(ang), jnp.sin(ang)
    ones = jnp.ones((seq_len, QK_NOPE), F32)
    zeros = jnp.zeros((seq_len, QK_NOPE), F32)
    pad1 = jnp.ones((seq_len, LANES - QK_NOPE - QK_ROPE), F32)
    pad0 = jnp.zeros((seq_len, LANES - QK_NOPE - QK_ROPE), F32)
    return (jnp.concatenate([ones, cos, cos, pad1], axis=1),
            jnp.concatenate([zeros, -sin, sin, pad0], axis=1))


def _pad_heads(w, width):
    rows = w.shape[0]
    return jnp.zeros((rows, N_HEADS, LANES), w.dtype).at[:, :, :width].set(
        w.reshape(rows, N_HEADS, width)).reshape(rows, N_HEADS * LANES)


def _cd_weights(w_in, w_uq, w_ukv, b_f):
    d = w_in.shape[0]
    nd = N_HEADS * HEAD_DIM
    o = Q_RANK + KV_RANK + QK_ROPE
    c_q, c_kv, k_rope = w_in[:, :Q_RANK], w_in[:, Q_RANK:Q_RANK + KV_RANK], w_in[:, Q_RANK + KV_RANK:o]
    q_d, k_d, v_d, f_logit = (w_in[:, o:o + nd], w_in[:, o + nd:o + 2 * nd], w_in[:, o + 2 * nd:o + 3 * nd],
                              w_in[:, o + 3 * nd:])
    kr_blk = jnp.zeros((d, LANES), w_in.dtype).at[:, QK_NOPE:QK_NOPE + QK_ROPE].set(k_rope)
    f_blk = jnp.zeros((d, LANES), w_in.dtype).at[:, :N_HEADS].set(f_logit)
    w1 = jnp.concatenate([c_q, c_kv, kr_blk, q_d, k_d, v_d, f_blk], axis=1).astype(BF16)
    wuq = _pad_heads(w_uq, QK_NOPE + QK_ROPE).astype(BF16)
    ukv = w_ukv.reshape(KV_RANK, N_HEADS, QK_NOPE + HEAD_DIM)
    wk = _pad_heads(ukv[:, :, :QK_NOPE].reshape(KV_RANK, N_HEADS * QK_NOPE), QK_NOPE).astype(BF16)
    wv = ukv[:, :, QK_NOPE:].reshape(KV_RANK, N_HEADS * HEAD_DIM).astype(BF16)
    bf = jnp.zeros((1, LANES), F32).at[0, :N_HEADS].set(b_f.astype(F32))
    return w1, wuq, wk, wv, bf


def _forget_selectors():
    selq = np.zeros((LANES, _NPAD), np.float32)
    selk = np.zeros((LANES, _NPAD), np.float32)
    oneq = np.zeros((1, _NPAD), np.float32)
    onek = np.zeros((1, _NPAD), np.float32)
    hsel = np.zeros((_NPAD, LANES), np.float32)
    for h in range(N_HEADS):
        hsel[h * LANES:(h + 1) * LANES, h] = 1.0
        for j in range(N_PIECES):
            selq[j * N_HEADS + h, h * LANES + F_LANE0 + j] = 1.0
            onek[0, h * LANES + F_LANE0 + j] = 1.0
            selk[j * N_HEADS + h, h * LANES + ONE_LANE0 + j] = -1.0
            oneq[0, h * LANES + ONE_LANE0 + j] = 1.0
    return (jnp.asarray(selq, BF16), jnp.asarray(selk, BF16), jnp.asarray(oneq), jnp.asarray(onek),
            jnp.asarray(hsel, BF16))


def kernel(x, ab_norm, ab_w_in, ab_rel_bias, ab_w_o, cd_norm, cd_w_in, cd_q_norm, cd_w_uq, cd_kv_norm, cd_w_ukv,
           cd_b_f, cd_w_o, ffn_norm, ffn_w_gate, ffn_w_up, ffn_conv_w, ffn_conv_b, ffn_w_down, final_norm):
    b, s, d = x.shape
    n = b * s

    qscale = jnp.full((N_HEADS * HEAD_DIM,), HEAD_DIM ** -0.5 * LOG2E, F32)
    one = jnp.ones((2 * N_HEADS * HEAD_DIM,), F32)
    colscale = jnp.concatenate([qscale, one, qscale, one])[None, :]
    per_head = lambda tab: tab[:, :, 0, :N_HEADS]
    hsel_a = jnp.asarray(np.repeat(np.eye(N_HEADS, LANES, dtype=np.float32), HEAD_DIM, axis=0), BF16)
    proj, vat, vbt, kanorm = _ab_proj(x, ab_norm[0][None, :], ab_w_in[0].astype(BF16), colscale, hsel_a)
    bias_max = jnp.max(ab_rel_bias[0].astype(F32), axis=1) * LOG2E + 1.0
    oa = _chunk_attn(proj, vat, _chunk_bias_tiles(ab_rel_bias[0]), per_head(kanorm), bias_max)
    tri = jnp.asarray(np.triu(np.ones((TK, TK), np.float32), k=1), BF16)
    ob = _stick_attn(proj, vbt, tri)
    x2d = _out_ffn(x.reshape(n, d), oa.reshape(n, -1), ob.reshape(n, -1), ab_w_o[0].astype(BF16),
                   ffn_norm[0][None, :], ffn_w_gate[0].astype(BF16), ffn_w_up[0].astype(BF16), ffn_conv_w[0],
                   ffn_conv_b[0][None, :], ffn_w_down[0].astype(BF16), None, s)

    w1, wuq, wk, wv, bf = _cd_weights(cd_w_in[0], cd_w_uq[0], cd_w_ukv[0], cd_b_f[0])
    cos_t, sin_t = _rope_tables(s)
    qc, kc, vct, qd, kd, vdt, kcnorm, kdnorm, fend = _cd_proj(
        x2d.reshape(b, s, d), cd_norm[0][None, :], w1, cd_q_norm[0][None, :], wuq, cd_kv_norm[0][None, :], wk, wv, bf,
        cos_t, sin_t, *_forget_selectors())
    oc = _softmax_attn(qc, kc, vct, per_head(kcnorm), True, "mla_attn")
    od = _softmax_attn(qd, kd, vdt, per_head(kdnorm), False, "fox_attn", per_head(fend))
    out = _out_ffn(x2d, oc.reshape(n, -1), od.reshape(n, -1), cd_w_o[0].astype(BF16), ffn_norm[1][None, :],
                   ffn_w_gate[1].astype(BF16), ffn_w_up[1].astype(BF16), ffn_conv_w[1], ffn_conv_b[1][None, :],
                   ffn_w_down[1].astype(BF16), final_norm[None, :], s)
    return out.reshape(b, s, d)
```

```python
import functools
import math

import numpy as np
import jax
import jax.numpy as jnp
from jax import lax
from jax.experimental import pallas as pl
from jax.experimental.pallas import tpu as pltpu

F32 = jnp.float32
BF16 = jnp.bfloat16

D_MODEL = 1024
HEAD_DIM = 64
CHUNK = 64
LEFT_CHUNKS = 8
BAND = (LEFT_CHUNKS + 1) * CHUNK
MAX_REL = 128
N_HEADS = 8
N_PAIRS = N_HEADS // 2
QK_NOPE = 64
QK_ROPE = 32
Q_RANK = 384
KV_RANK = 256
ROPE_THETA = 10000.0
D_FF = 2816
RMS_EPS = 1e-6

LANES = 128
SUBLANES = 8
LOG2E = math.log2(math.e)
NEG_BIG = -1e30
STICK_UNDERFLOW_LOG2 = float("inf")
SOFTMAX_UNDERFLOW_LOG2 = float("inf")
NORM_MARGIN = 1.01
SAFE_DENOM_LOG2 = 60.0
VMEM_LIMIT = 52 * 1024 * 1024

TM_PROJ = 256
TM_FFN = 256
TF_FFN = 2816
TK = 256
TQ = 256
N_WIN_A = LEFT_CHUNKS * CHUNK // TK + 1
HEADS_PER_STEP = 8
TILES_PER_STEP = 2
SUM_ROWS = 16

N_PIECES = 3
F_LANE0 = HEAD_DIM
ONE_LANE0 = HEAD_DIM + N_PIECES

NT_DIMS = (((1,), (1,)), ((), ()))


def _rms(x, g):
    ms = jnp.mean(x * x, axis=-1, keepdims=True)
    return x * lax.rsqrt(ms + RMS_EPS) * g


def _store_pairs_transposed(vt_ref, v):
    for p in range(N_PAIRS):
        vt_ref[0, p] = v[:, p * LANES:(p + 1) * LANES].T.astype(vt_ref.dtype)


def _running_max_norm(k, hsel_ref, carry_ref, out_ref):
    n2 = jnp.dot((k * k).astype(BF16), hsel_ref[...], preferred_element_type=F32)
    kmax = jnp.maximum(carry_ref[0:1, :], jnp.sqrt(jnp.max(n2, axis=0, keepdims=True)) * NORM_MARGIN)
    carry_ref[...] = jnp.broadcast_to(kmax, carry_ref.shape)
    out_ref[0, 0] = jnp.broadcast_to(kmax, out_ref.shape[2:])


def _ab_proj_kernel(x_ref, g_ref, w_ref, cs_ref, hsel_ref, o_ref, vat_ref, vbt_ref, kanorm_ref, kcarry_ref):
    h = _rms(x_ref[0], g_ref[...]).astype(BF16)
    p = jnp.dot(h, w_ref[...], preferred_element_type=F32) * cs_ref[...]
    o_ref[0] = p.astype(o_ref.dtype)
    nv = N_HEADS * HEAD_DIM
    _store_pairs_transposed(vat_ref, p[:, 2 * nv:3 * nv])
    _store_pairs_transposed(vbt_ref, p[:, 5 * nv:6 * nv])

    @pl.when(pl.program_id(1) == 0)
    def _():
        kcarry_ref[...] = jnp.zeros_like(kcarry_ref)

    _running_max_norm(p[:, nv:2 * nv], hsel_ref, kcarry_ref, kanorm_ref)


def _ab_proj(x, g, w, colscale, hsel):
    b, s, d = x.shape
    nc = w.shape[1]
    tm = TM_PROJ
    assert tm == TK
    vt_spec = pl.BlockSpec((1, N_PAIRS, LANES, tm), lambda bi, ti: (bi, 0, 0, ti))
    vt_shape = jax.ShapeDtypeStruct((b, N_PAIRS, LANES, s), BF16)
    return pl.pallas_call(
        _ab_proj_kernel,
        grid=(b, s // tm),
        in_specs=[
            pl.BlockSpec((1, tm, d), lambda bi, ti: (bi, ti, 0)),
            pl.BlockSpec((1, d), lambda bi, ti: (0, 0)),
            pl.BlockSpec((d, nc), lambda bi, ti: (0, 0)),
            pl.BlockSpec((1, nc), lambda bi, ti: (0, 0)),
            pl.BlockSpec(hsel.shape, lambda bi, ti: (0, 0)),
        ],
        out_specs=[pl.BlockSpec((1, tm, nc), lambda bi, ti: (bi, ti, 0)), vt_spec, vt_spec,
                   pl.BlockSpec((1, 1, SUBLANES, LANES), lambda bi, ti: (bi, ti, 0, 0))],
        out_shape=[jax.ShapeDtypeStruct((b, s, nc), BF16), vt_shape, vt_shape,
                   jax.ShapeDtypeStruct((b, s // tm, SUBLANES, LANES), F32)],
        scratch_shapes=[pltpu.VMEM((SUBLANES, LANES), F32)],
        compiler_params=pltpu.CompilerParams(
            dimension_semantics=("arbitrary", "arbitrary"), vmem_limit_bytes=VMEM_LIMIT),
        name="ab_norm_proj",
    )(x, g, w, colscale, hsel)


_NPAD = N_HEADS * LANES
_C_Q0 = 0
_C_KV0 = _C_Q0 + Q_RANK
_C_KR0 = _C_KV0 + KV_RANK
_C_QD0 = _C_KR0 + LANES
_C_KD0 = _C_QD0 + N_HEADS * HEAD_DIM
_C_VD0 = _C_KD0 + N_HEADS * HEAD_DIM
_C_F0 = _C_VD0 + N_HEADS * HEAD_DIM
_C_END = _C_F0 + LANES


def _cd_proj_kernel(x_ref, g_ref, w1_ref, qn_ref, wuq_ref, kvn_ref, wk_ref, wv_ref, bf_ref, cos_ref, sin_ref,
                    selq_ref, selk_ref, oneq_ref, onek_ref, hsel_ref,
                    qc_ref, kc_ref, vct_ref, qd_ref, kd_ref, vdt_ref, kcnorm_ref, knorm_ref, fend_ref,
                    carry_ref, kcarry_ref, kccarry_ref):
    t = pl.program_id(1)
    tm = x_ref.shape[1]
    h = _rms(x_ref[0], g_ref[...]).astype(BF16)
    p = jnp.dot(h, w1_ref[...], preferred_element_type=F32)

    cq = _rms(p[:, _C_Q0:_C_KV0], qn_ref[...]).astype(BF16)
    ckv = _rms(p[:, _C_KV0:_C_KR0], kvn_ref[...]).astype(BF16)

    cosb = cos_ref[...]
    sinb = sin_ref[...]
    lane = lax.broadcasted_iota(jnp.int32, (tm, LANES), 1)

    def rope(xb):
        partner = jnp.where(lane < QK_NOPE + QK_ROPE // 2, pltpu.roll(xb, LANES - QK_ROPE // 2, 1),
                            pltpu.roll(xb, QK_ROPE // 2, 1))
        return xb * cosb + partner * sinb

    qc = jnp.dot(cq, wuq_ref[...], preferred_element_type=F32) * ((QK_NOPE + QK_ROPE) ** -0.5 * LOG2E)
    kc = jnp.dot(ckv, wk_ref[...], preferred_element_type=F32)
    kr = rope(p[:, _C_KR0:_C_QD0])
    kc_heads = []
    for hh in range(N_HEADS):
        sl = slice(hh * LANES, (hh + 1) * LANES)
        qc_ref[0, :, sl] = rope(qc[:, sl]).astype(BF16)
        kc_heads.append(kc[:, sl] + kr)
        kc_ref[0, :, sl] = kc_heads[hh].astype(BF16)
    kc = jnp.concatenate(kc_heads, axis=1)
    _store_pairs_transposed(vct_ref, jnp.dot(ckv, wv_ref[...], preferred_element_type=F32))
    _store_pairs_transposed(vdt_ref, p[:, _C_VD0:_C_F0])

    fl = p[:, _C_F0:_C_END] + bf_ref[...]
    y = jnp.minimum(fl, 0.0) - jnp.log(1.0 + jnp.exp(-jnp.abs(fl)))
    row = lax.broadcasted_iota(jnp.int32, (tm, LANES), 0)
    sh = 1
    while sh < tm:
        y = y + jnp.where(row >= sh, pltpu.roll(y, sh, 0), 0.0)
        sh *= 2

    @pl.when(t == 0)
    def _():
        carry_ref[...] = jnp.zeros_like(carry_ref)
        kcarry_ref[...] = jnp.zeros_like(kcarry_ref)
        kccarry_ref[...] = jnp.zeros_like(kccarry_ref)

    y = y + carry_ref[0:1, :]
    carry_ref[...] = jnp.broadcast_to(y[tm - 1:tm, :], carry_ref.shape)
    f2 = y * LOG2E

    hi = f2.astype(BF16).astype(F32)
    r1 = f2 - hi
    mid = r1.astype(BF16).astype(F32)
    lo = r1 - mid
    fp = jnp.where(lane < N_HEADS, hi, jnp.where(lane < 2 * N_HEADS, pltpu.roll(mid, N_HEADS, 1),
                                                 pltpu.roll(lo, 2 * N_HEADS, 1))).astype(BF16)
    def one_head_per_block(x2):
        blocks = []
        for hh in range(N_HEADS):
            src = x2[:, (hh // 2) * LANES:(hh // 2 + 1) * LANES]
            if hh % 2:
                src = pltpu.roll(src, HEAD_DIM, 1)
            blocks.append(jnp.where(lane < HEAD_DIM, src, 0.0))
        return jnp.concatenate(blocks, axis=1)

    qd = one_head_per_block(p[:, _C_QD0:_C_KD0]) * (HEAD_DIM ** -0.5 * LOG2E)
    kd = one_head_per_block(p[:, _C_KD0:_C_VD0])
    qd_ref[0] = (qd +jnp.dot(fp, selq_ref[...], preferred_element_type=F32) + oneq_ref[...]).astype(BF16)
    kd_ref[0] = (kd + jnp.dot(fp, selk_ref[...], preferred_element_type=F32) + onek_ref[...]).astype(BF16)

    _running_max_norm(kc, hsel_ref, kccarry_ref, kcnorm_ref)
    _running_max_norm(kd, hsel_ref, kcarry_ref, knorm_ref)
    fend_ref[0, 0] = jnp.broadcast_to(f2[tm - 1:tm, :], fend_ref.shape[2:])


def _cd_proj(x, g, w1, qn, wuq, kvn, wk, wv, bf, cos_t, sin_t, selq, selk, oneq, onek, hsel):
    b, s, d = x.shape
    tm = TM_PROJ
    assert tm == TK
    const = lambda a: pl.BlockSpec(a.shape, lambda bi, ti: (0,) * a.ndim)
    tok = lambda nc: pl.BlockSpec((1, tm, nc), lambda bi, ti: (bi, ti, 0))
    vt_spec = pl.BlockSpec((1, N_PAIRS, LANES, tm), lambda bi, ti: (bi, 0, 0, ti))
    tab_spec = pl.BlockSpec((1, 1, SUBLANES, LANES), lambda bi, ti: (bi, ti, 0, 0))
    act = jax.ShapeDtypeStruct((b, s, _NPAD), BF16)
    vt = jax.ShapeDtypeStruct((b, N_PAIRS, LANES, s), BF16)
    tab = jax.ShapeDtypeStruct((b, s // tm, SUBLANES, LANES), F32)
    return pl.pallas_call(
        _cd_proj_kernel,
        grid=(b, s // tm),
        in_specs=[
            tok(d), const(g), const(w1), const(qn), const(wuq), const(kvn), const(wk), const(wv), const(bf),
            pl.BlockSpec((tm, LANES), lambda bi, ti: (ti, 0)),
            pl.BlockSpec((tm, LANES), lambda bi, ti: (ti, 0)),
            const(selq), const(selk), const(oneq), const(onek), const(hsel),
        ],
        out_specs=[tok(_NPAD), tok(_NPAD), vt_spec, tok(_NPAD), tok(_NPAD), vt_spec, tab_spec, tab_spec, tab_spec],
        out_shape=[act, act, vt, act, act, vt, tab, tab, tab],
        scratch_shapes=[pltpu.VMEM((SUBLANES, LANES), F32)] * 3,
        compiler_params=pltpu.CompilerParams(
            dimension_semantics=("arbitrary", "arbitrary"), vmem_limit_bytes=VMEM_LIMIT),
        name="cd_norm_proj",
    )(x, g, w1, qn, wuq, kvn, wk, wv, bf, cos_t, sin_t, selq, selk, oneq, onek, hsel)


def _mask_pair_heads(q_ref, q2_ref):
    tq = q_ref.shape[1]
    lane = lax.broadcasted_iota(jnp.int32, (tq, LANES), 1)
    for h in range(q2_ref.shape[0]):
        q = q_ref[0, :, (h // 2) * LANES:(h // 2 + 1) * LANES]
        q2_ref[h] = jnp.where((lane >= HEAD_DIM) == bool(h % 2), q, jnp.zeros_like(q))


def _store_pair_heads(o_ref, outs):
    tq = outs[0].shape[1]
    lane = lax.broadcasted_iota(jnp.int32, (tq, LANES), 1)
    for pr in range(len(outs) // 2):
        o_ref[0, :, pr * LANES:(pr + 1) * LANES] = jnp.where(
            lane < HEAD_DIM, outs[2 * pr].T, outs[2 * pr + 1].T).astype(o_ref.dtype)


def _softmax_tile_update(ss, vts, m_ref, acc_ref):
    heads = range(len(ss))
    m_old = [m_ref[h] for h in heads]
    acc_old = [acc_ref[h] for h in heads]
    m_new = [functools.reduce(jnp.maximum, [jnp.max(s, axis=0, keepdims=True) for s in ss[h]], m_old[h])
             for h in heads]
    alpha = [jnp.exp2(m_old[h] - m_new[h]) for h in heads]
    ps = [[jnp.exp2(s - m_new[h]).astype(BF16) for s in ss[h]] for h in heads]
    ones = jnp.ones((SUM_ROWS, ss[0][0].shape[0]), BF16)
    pvs = [sum(jnp.dot(jnp.concatenate([vt, ones], axis=0), p, preferred_element_type=F32)
               for vt, p in zip(vts[h], ps[h])) for h in heads]
    for h in heads:
        m_ref[h] = m_new[h]
        acc_ref[h] = alpha[h] * acc_old[h] + pvs[h]


def _softmax_tile_update_fixed(ss, vts, m_ref, acc_ref):
    heads = range(len(ss))
    acc_old = [acc_ref[h] for h in heads]
    ps = [[jnp.exp2(s - m_ref[h]).astype(BF16) for s in ss[h]] for h in heads]
    ones = jnp.ones((SUM_ROWS, ss[0][0].shape[0]), BF16)
    pvs = [sum(jnp.dot(jnp.concatenate([vt, ones], axis=0), p, preferred_element_type=F32)
               for vt, p in zip(vts[h], ps[h])) for h in heads]
    for h in heads:
        acc_ref[h] = acc_old[h] + pvs[h]


def _normalised(acc_ref, h):
    return acc_ref[h, 0:LANES, :] / acc_ref[h, LANES:LANES + 1, :]


def _chunk_attn_kernel(q_ref, k_ref, vt_ref, bias_ref, knorm_ref, bmax_ref, o_ref, q2_ref, m_ref, acc_ref):
    bi = pl.program_id(0)
    qi = pl.program_id(2)
    tq = q_ref.shape[1]
    heads = range(acc_ref.shape[0])
    _mask_pair_heads(q_ref, q2_ref)
    ones = jnp.ones((SUBLANES, LANES), BF16)
    qnorm = []
    for h in heads:
        q32 = q2_ref[h].astype(F32)
        n2 = lax.dot_general(ones, (q32 * q32).astype(BF16), NT_DIMS, preferred_element_type=F32)[0:1]
        qnorm.append(jnp.sqrt(n2) * NORM_MARGIN)

    def run(fixed, js):
        update = _softmax_tile_update_fixed if fixed else _softmax_tile_update
        acc_ref[...] = jnp.zeros_like(acc_ref)
        for h in heads:
            m_ref[h] = ((qnorm[h] * knorm_ref[bi, qi, h] + bmax_ref[h]) if fixed
                        else jnp.full((1, tq), NEG_BIG, F32))
        kstarts = [pl.multiple_of((qi - (N_WIN_A - 1) + j) * tq, tq) for j in js]
        ss = [[lax.dot_general(k_ref[0, pl.ds(ks, tq), (h // 2) * LANES:(h // 2 + 1) * LANES], q2_ref[h],
                               NT_DIMS, preferred_element_type=F32) + bias_ref[h, j] for j, ks in zip(js, kstarts)]
              for h in heads]
        vts = [[vt_ref[0, h // 2, :, pl.ds(ks, tq)] for ks in kstarts] for h in heads]
        update(ss, vts, m_ref, acc_ref)
        _store_pair_heads(o_ref, [_normalised(acc_ref, h) for h in heads])

    def run_window(fixed):
        first = jnp.maximum(N_WIN_A - 1 - qi, 0)
        for f in range(N_WIN_A):
            pl.when(first == f)(functools.partial(run, fixed, list(range(f, N_WIN_A))))

    run_window(True)
    smallest = jnp.min(functools.reduce(jnp.minimum, [acc_ref[h, LANES:LANES + 1, :] for h in heads]))
    pl.when(jnp.logical_not(smallest >= 2.0 ** -SAFE_DENOM_LOG2))(lambda: run_window(False))


def _chunk_attn(proj, vt, bias, knorm, bmax):
    b, s, _ = proj.shape
    nh = HEADS_PER_STEP
    assert nh == N_HEADS
    width = nh * HEAD_DIM
    k0 = N_HEADS * HEAD_DIM // width
    resident = dict(pipeline_mode=pl.Buffered(1))
    return pl.pallas_call(
        _chunk_attn_kernel,
        grid=(b, N_HEADS // nh, s // TK),
        in_specs=[
            pl.BlockSpec((1, TK, width), lambda bi, hg, qi: (bi, qi, hg)),
            pl.BlockSpec((1, s, width), lambda bi, hg, qi: (bi, 0, k0 + hg), **resident),
            pl.BlockSpec((1, nh // 2, LANES, s), lambda bi, hg, qi: (bi, hg, 0, 0), **resident),
            pl.BlockSpec((nh, N_WIN_A, TK, TK), lambda bi, hg, qi: (hg, 0, 0, 0), **resident),
            pl.BlockSpec(memory_space=pltpu.SMEM),
            pl.BlockSpec(memory_space=pltpu.SMEM),
        ],
        out_specs=pl.BlockSpec((1, TK, width), lambda bi, hg, qi: (bi, qi, hg)),
        out_shape=jax.ShapeDtypeStruct((b, s, N_HEADS * HEAD_DIM), BF16),
        scratch_shapes=[pltpu.VMEM((nh, TK, LANES), BF16), pltpu.VMEM((nh, 1, TK), F32),
                        pltpu.VMEM((nh, LANES + SUM_ROWS, TK), F32)],
        compiler_params=pltpu.CompilerParams(
            dimension_semantics=("arbitrary", "arbitrary", "arbitrary"), vmem_limit_bytes=VMEM_LIMIT),
        name="chunk_attn",
    )(proj, proj, vt, bias, knorm, bmax)


def _chunk_bias_tiles(rel_bias):
    h = rel_bias.shape[0]
    nq = TK
    nk = N_WIN_A * TK
    period = 1 << (nq + nk - 1).bit_length()
    u = np.arange(period)
    signed = np.where(u < nk, u, u - period)
    idx = np.clip(LEFT_CHUNKS * CHUNK - signed, -MAX_REL, MAX_REL) + MAX_REL
    v = rel_bias.astype(F32)[:, idx] * LOG2E
    toep = jnp.tile(v, (1, nq))[:, :nq * (period - 1)].reshape(h, nq, period - 1)[:, :, :nk]
    r = np.arange(nq)[:, None]
    off = np.arange(nk)[None, :] - CHUNK * (r // CHUNK)
    in_band = (off >= 0) & (off < BAND)
    bias = jnp.where(in_band[None], toep, NEG_BIG)
    return bias.reshape(h, nq, N_WIN_A, TK).transpose(0, 2, 3, 1)


def _stick_kernel(q_ref, k_ref, vt_ref, tri_ref, o_ref, q2_ref, c_ref, acc_ref):
    qi = pl.program_id(2)
    tq = q_ref.shape[1]
    tk = tri_ref.shape[0]
    heads = range(acc_ref.shape[0])
    pair = lambda h: slice((h // 2) * LANES, (h // 2 + 1) * LANES)
    _mask_pair_heads(q_ref, q2_ref)
    c_ref[...] = jnp.zeros_like(c_ref)
    acc_ref[...] = jnp.zeros_like(acc_ref)
    sign_bit = jnp.uint32(0x80000000)

    def tiles(js, masked):
        nt = range(len(js))
        kstarts = [pl.multiple_of(j * tk, tk) for j in js]
        zs = [[lax.dot_general(k_ref[0, pl.ds(kstarts[t], tk), pair(h)], q2_ref[h], NT_DIMS,
                               preferred_element_type=F32) for t in nt] for h in heads]
        c_old = [c_ref[h] for h in heads]
        acc_old = [acc_ref[h] for h in heads]

        def neg_log_keep(z, t):
            neg_abs = lax.bitcast_convert_type(lax.bitcast_convert_type(z, jnp.uint32) | sign_bit, F32)
            nlk = jnp.maximum(z, 0.0) + jnp.log(1.0 + jnp.exp2(neg_abs)) * LOG2E
            return jnp.where(masks[t], nlk, 0.0) if masked[t] else nlk

        masks = [None] * len(js)
        for t in nt:
            if masked[t]:
                key = kstarts[t] + lax.broadcasted_iota(jnp.int32, (tk, tq), 0)
                qry = qi * tq + lax.broadcasted_iota(jnp.int32, (tk, tq), 1)
                masks[t] = key < qry
        nlk = [[neg_log_keep(zs[h][t], t) for t in nt] for h in heads]
        rs = [[jnp.dot(tri_ref[...], nlk[h][t].astype(BF16), preferred_element_type=F32) for t in nt] for h in heads]
        pvs = []
        c_new = []
        for h in heads:
            c = c_old[h]
            pv = None
            for t in nt:
                w = jnp.exp2((zs[h][t] - nlk[h][t]) - rs[h][t] - c)
                if masked[t]:
                    w = jnp.where(masks[t], w, 0.0)
                d = jnp.dot(vt_ref[0, h // 2, :, pl.ds(kstarts[t], tk)], w.astype(BF16),
                            preferred_element_type=F32)
                pv = d if pv is None else pv + d
                c = c + jnp.sum(nlk[h][t], axis=0, keepdims=True)
            pvs.append(pv)
            c_new.append(c)
        for h in heads:
            c_ref[h] = c_new[h]
            acc_ref[h] = acc_old[h] + pvs[h]

    assert tq == tk
    pl.when(qi > 0)(lambda: tiles([qi, qi - 1], [True, False]))
    pl.when(qi == 0)(lambda: tiles([qi], [True]))

    def more(j):
        return jnp.logical_and(j >= 0, jnp.min(c_ref[...]) < STICK_UNDERFLOW_LOG2)

    def body(j):
        tiles([j], [False])
        return j - 1

    lax.while_loop(more, body, qi - 2)
    _store_pair_heads(o_ref, [acc_ref[h] for h in heads])


def _stick_attn(proj, vt, tri):
    b, s, _ = proj.shape
    nh = HEADS_PER_STEP
    width = nh * HEAD_DIM
    q0 = 3 * N_HEADS * HEAD_DIM // width
    k0 = 4 * N_HEADS * HEAD_DIM // width
    resident = dict(pipeline_mode=pl.Buffered(1))
    return pl.pallas_call(
        _stick_kernel,
        grid=(b, N_HEADS // nh, s // TQ),
        in_specs=[
            pl.BlockSpec((1, TQ, width), lambda bi, hg, qi: (bi, qi, q0 + hg)),
            pl.BlockSpec((1, s, width), lambda bi, hg, qi: (bi, 0, k0 + hg), **resident),
            pl.BlockSpec((1, nh // 2, LANES, s), lambda bi, hg, qi: (bi, hg, 0, 0), **resident),
            pl.BlockSpec((TK, TK), lambda bi, hg, qi: (0, 0)),
        ],
        out_specs=pl.BlockSpec((1, TQ, width), lambda bi, hg, qi: (bi, qi, hg)),
        out_shape=jax.ShapeDtypeStruct((b, s, N_HEADS * HEAD_DIM), BF16),
        scratch_shapes=[pltpu.VMEM((nh, TQ, LANES), BF16), pltpu.VMEM((nh, 1, TQ), F32),
                        pltpu.VMEM((nh, LANES, TQ), F32)],
        compiler_params=pltpu.CompilerParams(
            dimension_semantics=("arbitrary", "arbitrary", "arbitrary"), vmem_limit_bytes=VMEM_LIMIT),
        name="stick_attn",
    )(proj, proj, vt, tri)


def _softmax_attn_kernel(*refs, chunk_mask, decay_skip):
    if decay_skip:
        q_ref, k_ref, vt_ref, knorm_ref, fend_ref, o_ref, m_ref, acc_ref = refs
    else:
        q_ref, k_ref, vt_ref, knorm_ref, o_ref, m_ref, acc_ref = refs
    bi = pl.program_id(0)
    qi = pl.program_id(2)
    tq = q_ref.shape[1]
    tk = TK
    heads = range(q_ref.shape[2] // LANES)
    sl = lambda h: slice(h * LANES, (h + 1) * LANES)
    assert tq == tk
    g = TILES_PER_STEP
    rem = qi % g

    lane = lax.broadcasted_iota(jnp.int32, (SUBLANES, LANES), 1)
    dims = jnp.where(lane < (HEAD_DIM if decay_skip else LANES), 1.0, 0.0).astype(BF16)
    pieces = jnp.where((lane >= F_LANE0) & (lane < F_LANE0 + N_PIECES), 1.0, 0.0).astype(BF16)
    qnorm, fq = [], []
    for h in heads:
        q = q_ref[0, :, sl(h)]
        q32 = q.astype(F32)
        n2 = lax.dot_general(dims, (q32 * q32).astype(BF16), NT_DIMS, preferred_element_type=F32)[0:1]
        qnorm.append(jnp.sqrt(n2) * NORM_MARGIN)
        if decay_skip:
            fq.append(lax.dot_general(pieces, q, NT_DIMS, preferred_element_type=F32)[0:1])

    def run(fixed):
        update = _softmax_tile_update_fixed if fixed else _softmax_tile_update
        acc_ref[...] = jnp.zeros_like(acc_ref)
        for h in heads:
            m_ref[h] = (qnorm[h] * knorm_ref[bi, qi, h] + 1.0) if fixed else jnp.full((1, tq), NEG_BIG, F32)

        def tiles(js, masked):
            kstarts = [pl.multiple_of(j * tk, tk) for j in js]

            def scores(h, t):
                s = lax.dot_general(k_ref[0, pl.ds(kstarts[t], tk), sl(h)], q_ref[0, :, sl(h)], NT_DIMS,
                                    preferred_element_type=F32)
                if masked[t]:
                    key = kstarts[t] + lax.broadcasted_iota(jnp.int32, (tk, tq), 0)
                    qry = qi * tq + lax.broadcasted_iota(jnp.int32, (tk, tq), 1)
                    s = jnp.where((key // CHUNK <= qry // CHUNK) if chunk_mask else (key <= qry), s, NEG_BIG)
                return s

            ss = [[scores(h, t) for t in range(len(js))] for h in heads]
            vts = [[vt_ref[0, h // 2, :, pl.ds(ks, tk)] for ks in kstarts] for h in heads]
            update(ss, vts, m_ref, acc_ref)

        def diagonal_step():
            for r in range(g):
                pl.when(rem == r)(functools.partial(tiles, [qi - r + t for t in range(r + 1)], [False] * r + [True]))

        if not decay_skip:
            def body(i, carry):
                tiles([g * i + t for t in range(g)], [False] * g)
                return carry

            lax.fori_loop(0, qi // g, body, 0)
            diagonal_step()
        else:
            diagonal_step()
            top = qi - 1 - rem

            def more(it):
                j = jnp.maximum(top - g * it, 0)
                if fixed:
                    level = [m_ref[h] + jnp.log2(acc_ref[h, LANES:LANES + 1, :]) for h in heads]
                else:
                    level = [m_ref[h] for h in heads]
                gap = [qnorm[h] * knorm_ref[bi, j, h] + fq[h] - fend_ref[bi, j, h] - level[h] for h in heads]
                reach = jnp.max(functools.reduce(jnp.maximum, gap))
                return jnp.logical_and(it < qi // g, reach > -SOFTMAX_UNDERFLOW_LOG2)

            def body(it):
                tiles([top - g * it - t for t in range(g)], [False] * g)
                return it + 1

            lax.while_loop(more, body, 0)
        _store_pair_heads(o_ref, [_normalised(acc_ref, h) for h in heads])

    run(True)
    smallest = jnp.min(functools.reduce(jnp.minimum, [acc_ref[h, LANES:LANES + 1, :] for h in heads]))
    pl.when(jnp.logical_not(smallest >= 2.0 ** -SAFE_DENOM_LOG2))(lambda: run(False))


def _softmax_attn(q, k, vt, knorm, chunk_mask, name, fend=None):
    b, s, _ = q.shape
    nh = HEADS_PER_STEP
    assert nh == N_HEADS
    resident = dict(pipeline_mode=pl.Buffered(1))
    tables = [knorm] + ([] if fend is None else [fend])
    return pl.pallas_call(
        functools.partial(_softmax_attn_kernel, chunk_mask=chunk_mask, decay_skip=fend is not None),
        grid=(b, N_HEADS // nh, s // TQ),
        in_specs=[
            pl.BlockSpec((1, TQ, nh * LANES), lambda bi, hg, qi: (bi, qi, hg)),
            pl.BlockSpec((1, s, nh * LANES), lambda bi, hg, qi: (bi, 0, hg), **resident),
            pl.BlockSpec((1, nh // 2, LANES, s), lambda bi, hg, qi: (bi, hg, 0, 0), **resident),
        ] + [pl.BlockSpec(memory_space=pltpu.SMEM)] * len(tables),
        out_specs=pl.BlockSpec((1, TQ, nh * HEAD_DIM), lambda bi, hg, qi: (bi, qi, hg)),
        out_shape=jax.ShapeDtypeStruct((b, s, N_HEADS * HEAD_DIM), BF16),
        scratch_shapes=[pltpu.VMEM((nh, 1, TQ), F32), pltpu.VMEM((nh, LANES + SUM_ROWS, TQ), F32)],
        compiler_params=pltpu.CompilerParams(
            dimension_semantics=("arbitrary", "arbitrary", "arbitrary"), vmem_limit_bytes=VMEM_LIMIT),
        name=name,
    )(q, k, vt, *tables)


def _out_ffn_kernel(*refs, tiles_per_seq, final_norm):
    if final_norm:
        (x_ref, o1_ref, o2_ref, wo_ref, g_ref, wg_ref, wu_ref, cw_ref, cb_ref, wd_ref, fg_ref,
         out_ref, x1_ref, h_ref, acc_ref, gbuf_ref, tail_ref) = refs
    else:
        (x_ref, o1_ref, o2_ref, wo_ref, g_ref, wg_ref, wu_ref, cw_ref, cb_ref, wd_ref,
         out_ref, x1_ref, h_ref, acc_ref, gbuf_ref, tail_ref) = refs
    i = pl.program_id(0)
    f = pl.program_id(1)
    nf = pl.num_programs(1)
    tm = x_ref.shape[0]
    half = o1_ref.shape[1]

    @pl.when(f == 0)
    def _():
        x1 = (x_ref[...]
              + jnp.dot(o1_ref[...], wo_ref[0:half, :], preferred_element_type=F32)
              + jnp.dot(o2_ref[...], wo_ref[half:2 * half, :], preferred_element_type=F32))
        x1_ref[...] = x1
        h_ref[...] = _rms(x1, g_ref[...]).astype(BF16)
        acc_ref[...] = jnp.zeros_like(acc_ref)

    h = h_ref[...]
    g = jnp.dot(h, wg_ref[...], preferred_element_type=F32)
    u = jnp.dot(h, wu_ref[...], preferred_element_type=F32)

    prev = jnp.where(i % tiles_per_seq == 0, 0.0, tail_ref[f])
    halo = SUBLANES
    gbuf_ref[0:halo, :] = prev
    gbuf_ref[halo:halo + tm, :] = g
    tail_ref[f] = g[tm - halo:tm, :]
    gm1 = gbuf_ref[halo - 1:halo - 1 + tm, :]
    gm2 = gbuf_ref[halo - 2:halo - 2 + tm, :]
    cw = cw_ref[...]
    gc = cw[0:1, :] * gm2 + cw[1:2, :] * gm1 + cw[2:3, :] * g + cb_ref[...]
    y = (gc / (1.0 + jnp.exp(-gc)) * u).astype(BF16)
    acc_ref[...] += jnp.dot(y, wd_ref[...], preferred_element_type=F32)

    @pl.when(f == nf - 1)
    def _():
        res = x1_ref[...] + acc_ref[...]
        if final_norm:
            res = _rms(res, fg_ref[...])
        out_ref[...] = res


def _out_ffn(x2d, o1, o2, wo, g, wg, wu, cw, cb, wd, final_g, seq_len):
    n, d = x2d.shape
    half = o1.shape[1]
    dff = wg.shape[1]
    tm, tf = TM_FFN, TF_FFN
    nf = dff // tf
    final_norm = final_g is not None
    once = dict(pipeline_mode=pl.Buffered(1))
    per_f = once if nf == 1 else {}
    in_specs = [
        pl.BlockSpec((tm, d), lambda i, f: (i, 0)),
        pl.BlockSpec((tm, half), lambda i, f: (i, 0)),
        pl.BlockSpec((tm, half), lambda i, f: (i, 0)),
        pl.BlockSpec((d, d), lambda i, f: (0, 0), **once),
        pl.BlockSpec((1, d), lambda i, f: (0, 0)),
        pl.BlockSpec((d, tf), lambda i, f: (0, f), **per_f),
        pl.BlockSpec((d, tf), lambda i, f: (0, f), **per_f),
        pl.BlockSpec((3, tf), lambda i, f: (0, f)),
        pl.BlockSpec((1, tf), lambda i, f: (0, f)),
        pl.BlockSpec((tf, d), lambda i, f: (f, 0), **per_f),
    ]
    args = [x2d, o1, o2, wo, g, wg, wu, cw, cb, wd]
    if final_norm:
        in_specs.append(pl.BlockSpec((1, d), lambda i, f: (0, 0)))
        args.append(final_g)
    return pl.pallas_call(
        functools.partial(_out_ffn_kernel, tiles_per_seq=seq_len // tm, final_norm=final_norm),
        grid=(n // tm, nf),
        in_specs=in_specs,
        out_specs=pl.BlockSpec((tm, d), lambda i, f: (i, 0)),
        out_shape=jax.ShapeDtypeStruct((n, d), F32),
        scratch_shapes=[
            pltpu.VMEM((tm, d), F32),
            pltpu.VMEM((tm, d), BF16),
            pltpu.VMEM((tm, d), F32),
            pltpu.VMEM((tm + SUBLANES, tf), F32),
            pltpu.VMEM((nf, SUBLANES, tf), F32),
        ],
        compiler_params=pltpu.CompilerParams(
            dimension_semantics=("arbitrary", "arbitrary"), vmem_limit_bytes=VMEM_LIMIT),
        name="out_ffn_final" if final_norm else "out_ffn",
    )(*args)


def _rope_tables(seq_len):
    half = QK_ROPE // 2
    inv = ROPE_THETA ** (-jnp.arange(half, dtype=F32) / half)
    ang = jnp.arange(seq_len, dtype=F32)[:, None] * inv[None, :]
    cos, sin = jnp.cos(ang), jnp.sin(ang)
    ones = jnp.ones((seq_len, QK_NOPE), F32)
    zeros = jnp.zeros((seq_len, QK_NOPE), F32)
    pad1 = jnp.ones((seq_len, LANES - QK_NOPE - QK_ROPE), F32)
    pad0 = jnp.zeros((seq_len, LANES - QK_NOPE - QK_ROPE), F32)
    return (jnp.concatenate([ones, cos, cos, pad1], axis=1),
            jnp.concatenate([zeros, -sin, sin, pad0], axis=1))


def _pad_heads(w, width):
    rows = w.shape[0]
    return jnp.zeros((rows, N_HEADS, LANES), w.dtype).at[:, :, :width].set(
        w.reshape(rows, N_HEADS, width)).reshape(rows, N_HEADS * LANES)


def _cd_weights(w_in, w_uq, w_ukv, b_f):
    d = w_in.shape[0]
    nd = N_HEADS * HEAD_DIM
    o = Q_RANK + KV_RANK + QK_ROPE
    c_q, c_kv, k_rope = w_in[:, :Q_RANK], w_in[:, Q_RANK:Q_RANK + KV_RANK], w_in[:, Q_RANK + KV_RANK:o]
    q_d, k_d, v_d, f_logit = (w_in[:, o:o + nd], w_in[:, o + nd:o + 2 * nd], w_in[:, o + 2 * nd:o + 3 * nd],
                              w_in[:, o + 3 * nd:])
    kr_blk = jnp.zeros((d, LANES), w_in.dtype).at[:, QK_NOPE:QK_NOPE + QK_ROPE].set(k_rope)
    f_blk = jnp.zeros((d, LANES), w_in.dtype).at[:, :N_HEADS].set(f_logit)
    w1 = jnp.concatenate([c_q, c_kv, kr_blk, q_d, k_d, v_d, f_blk], axis=1).astype(BF16)
    wuq = _pad_heads(w_uq, QK_NOPE + QK_ROPE).astype(BF16)
    ukv = w_ukv.reshape(KV_RANK, N_HEADS, QK_NOPE + HEAD_DIM)
    wk = _pad_heads(ukv[:, :, :QK_NOPE].reshape(KV_RANK, N_HEADS * QK_NOPE), QK_NOPE).astype(BF16)
    wv = ukv[:, :, QK_NOPE:].reshape(KV_RANK, N_HEADS * HEAD_DIM).astype(BF16)
    bf = jnp.zeros((1, LANES), F32).at[0, :N_HEADS].set(b_f.astype(F32))
    return w1, wuq, wk, wv, bf


def _forget_selectors():
    selq = np.zeros((LANES, _NPAD), np.float32)
    selk = np.zeros((LANES, _NPAD), np.float32)
    oneq = np.zeros((1, _NPAD), np.float32)
    onek = np.zeros((1, _NPAD), np.float32)
    hsel = np.zeros((_NPAD, LANES), np.float32)
    for h in range(N_HEADS):
        hsel[h * LANES:(h + 1) * LANES, h] = 1.0
        for j in range(N_PIECES):
            selq[j * N_HEADS + h, h * LANES + F_LANE0 + j] = 1.0
            onek[0, h * LANES + F_LANE0 + j] = 1.0
            selk[j * N_HEADS + h, h * LANES + ONE_LANE0 + j] = -1.0
            oneq[0, h * LANES + ONE_LANE0 + j] = 1.0
    return (jnp.asarray(selq, BF16), jnp.asarray(selk, BF16), jnp.asarray(oneq), jnp.asarray(onek),
            jnp.asarray(hsel, BF16))


def kernel(x, ab_norm, ab_w_in, ab_rel_bias, ab_w_o, cd_norm, cd_w_in, cd_q_norm, cd_w_uq, cd_kv_norm, cd_w_ukv,
           cd_b_f, cd_w_o, ffn_norm, ffn_w_gate, ffn_w_up, ffn_conv_w, ffn_conv_b, ffn_w_down, final_norm):
    b, s, d = x.shape
    n = b * s

    qscale = jnp.full((N_HEADS * HEAD_DIM,), HEAD_DIM ** -0.5 * LOG2E, F32)
    one = jnp.ones((2 * N_HEADS * HEAD_DIM,), F32)
    colscale = jnp.concatenate([qscale, one, qscale, one])[None, :]
    per_head = lambda tab: tab[:, :, 0, :N_HEADS]
    hsel_a = jnp.asarray(np.repeat(np.eye(N_HEADS, LANES, dtype=np.float32), HEAD_DIM, axis=0), BF16)
    proj, vat, vbt, kanorm = _ab_proj(x, ab_norm[0][None, :], ab_w_in[0].astype(BF16), colscale, hsel_a)
    bias_max = jnp.max(ab_rel_bias[0].astype(F32), axis=1) * LOG2E + 1.0
    oa = _chunk_attn(proj, vat, _chunk_bias_tiles(ab_rel_bias[0]), per_head(kanorm), bias_max)
    tri = jnp.asarray(np.triu(np.ones((TK, TK), np.float32), k=1), BF16)
    ob = _stick_attn(proj, vbt, tri)
    x2d = _out_ffn(x.reshape(n, d), oa.reshape(n, -1), ob.reshape(n, -1), ab_w_o[0].astype(BF16),
                   ffn_norm[0][None, :], ffn_w_gate[0].astype(BF16), ffn_w_up[0].astype(BF16), ffn_conv_w[0],
                   ffn_conv_b[0][None, :], ffn_w_down[0].astype(BF16), None, s)

    w1, wuq, wk, wv, bf = _cd_weights(cd_w_in[0], cd_w_uq[0], cd_w_ukv[0], cd_b_f[0])
    cos_t, sin_t = _rope_tables(s)
    qc, kc, vct, qd, kd, vdt, kcnorm, kdnorm, fend = _cd_proj(
        x2d.reshape(b, s, d), cd_norm[0][None, :], w1, cd_q_norm[0][None, :], wuq, cd_kv_norm[0][None, :], wk, wv, bf,
        cos_t, sin_t, *_forget_selectors())
    oc = _softmax_attn(qc, kc, vct, per_head(kcnorm), True, "mla_attn")
    od = _softmax_attn(qd, kd, vdt, per_head(kdnorm), False, "fox_attn", per_head(fend))
    out = _out_ffn(x2d, oc.reshape(n, -1), od.reshape(n, -1), cd_w_o[0].astype(BF16), ffn_norm[1][None, :],
                   ffn_w_gate[1].astype(BF16), ffn_w_up[1].astype(BF16), ffn_conv_w[1], ffn_conv_b[1][None, :],
                   ffn_w_down[1].astype(BF16), final_norm[None, :], s)
    return out.reshape(b, s, d)
```

```python
import functools
import math

import numpy as np
import jax
import jax.numpy as jnp
from jax import lax
from jax.experimental import pallas as pl
from jax.experimental.pallas import tpu as pltpu

F32 = jnp.float32
BF16 = jnp.bfloat16

D_MODEL = 1024
HEAD_DIM = 64
CHUNK = 64
LEFT_CHUNKS = 8
BAND = (LEFT_CHUNKS + 1) * CHUNK
MAX_REL = 128
N_HEADS = 8
N_PAIRS = N_HEADS // 2
QK_NOPE = 64
QK_ROPE = 32
Q_RANK = 384
KV_RANK = 256
ROPE_THETA = 10000.0
D_FF = 2816
RMS_EPS = 1e-6

LANES = 128
SUBLANES = 8
LOG2E = math.log2(math.e)
NEG_BIG = -1e30
STICK_UNDERFLOW_LOG2 = 200.0
SOFTMAX_UNDERFLOW_LOG2 = 160.0
NORM_MARGIN = 1.01
SAFE_DENOM_LOG2 = 60.0
VMEM_LIMIT = 52 * 1024 * 1024

TM_PROJ = 512
TM_FFN = 256
TF_FFN = 2816
TK = 256
TQ = 256
N_WIN_A = LEFT_CHUNKS * CHUNK // TK + 1
HEADS_PER_STEP = 8
TILES_PER_STEP = 2
SUM_ROWS = 16

N_PIECES = 3
F_LANE0 = HEAD_DIM
ONE_LANE0 = HEAD_DIM + N_PIECES

NT_DIMS = (((1,), (1,)), ((), ()))


def _rms(x, g):
    ms = jnp.mean(x * x, axis=-1, keepdims=True)
    return x * lax.rsqrt(ms + RMS_EPS) * g


def _store_pairs_transposed(vt_ref, v):
    for p in range(N_PAIRS):
        vt_ref[0, p] = v[:, p * LANES:(p + 1) * LANES].T.astype(vt_ref.dtype)


def _running_max_norm(k, hsel_ref, carry_ref, out_ref):
    n2 = jnp.dot((k * k).astype(BF16), hsel_ref[...], preferred_element_type=F32)
    for i in range(k.shape[0] // TK):
        tile_max = jnp.sqrt(jnp.max(n2[i * TK:(i + 1) * TK], axis=0, keepdims=True)) * NORM_MARGIN
        kmax = jnp.maximum(carry_ref[0:1, :], tile_max)
        carry_ref[...] = jnp.broadcast_to(kmax, carry_ref.shape)
        out_ref[0, i] = jnp.broadcast_to(kmax, out_ref.shape[2:])


def _ab_proj_kernel(x_ref, g_ref, w_ref, cs_ref, hsel_ref, o_ref, vat_ref, vbt_ref, kanorm_ref, kcarry_ref):
    h = _rms(x_ref[0], g_ref[...]).astype(BF16)
    p = jnp.dot(h, w_ref[...], preferred_element_type=F32) * cs_ref[...]
    o_ref[0] = p.astype(o_ref.dtype)
    nv = N_HEADS * HEAD_DIM
    _store_pairs_transposed(vat_ref, p[:, 2 * nv:3 * nv])
    _store_pairs_transposed(vbt_ref, p[:, 5 * nv:6 * nv])

    @pl.when(pl.program_id(1) == 0)
    def _():
        kcarry_ref[...] = jnp.zeros_like(kcarry_ref)

    _running_max_norm(p[:, nv:2 * nv], hsel_ref, kcarry_ref, kanorm_ref)


def _ab_proj(x, g, w, colscale, hsel):
    b, s, d = x.shape
    nc = w.shape[1]
    tm = TM_PROJ
    vt_spec = pl.BlockSpec((1, N_PAIRS, LANES, tm), lambda bi, ti: (bi, 0, 0, ti))
    vt_shape = jax.ShapeDtypeStruct((b, N_PAIRS, LANES, s), BF16)
    return pl.pallas_call(
        _ab_proj_kernel,
        grid=(b, s // tm),
        in_specs=[
            pl.BlockSpec((1, tm, d), lambda bi, ti: (bi, ti, 0)),
            pl.BlockSpec((1, d), lambda bi, ti: (0, 0)),
            pl.BlockSpec((d, nc), lambda bi, ti: (0, 0)),
            pl.BlockSpec((1, nc), lambda bi, ti: (0, 0)),
            pl.BlockSpec(hsel.shape, lambda bi, ti: (0, 0)),
        ],
        out_specs=[pl.BlockSpec((1, tm, nc), lambda bi, ti: (bi, ti, 0)), vt_spec, vt_spec,
                   pl.BlockSpec((1, tm // TK, SUBLANES, LANES), lambda bi, ti: (bi, ti, 0, 0))],
        out_shape=[jax.ShapeDtypeStruct((b, s, nc), BF16), vt_shape, vt_shape,
                   jax.ShapeDtypeStruct((b, s // TK, SUBLANES, LANES), F32)],
        scratch_shapes=[pltpu.VMEM((SUBLANES, LANES), F32)],
        compiler_params=pltpu.CompilerParams(
            dimension_semantics=("arbitrary", "arbitrary"), vmem_limit_bytes=VMEM_LIMIT),
        name="ab_norm_proj",
    )(x, g, w, colscale, hsel)


_NPAD = N_HEADS * LANES
_C_Q0 = 0
_C_KV0 = _C_Q0 + Q_RANK
_C_KR0 = _C_KV0 + KV_RANK
_C_QD0 = _C_KR0 + LANES
_C_KD0 = _C_QD0 + N_HEADS * HEAD_DIM
_C_VD0 = _C_KD0 + N_HEADS * HEAD_DIM
_C_F0 = _C_VD0 + N_HEADS * HEAD_DIM
_C_END = _C_F0 + LANES


def _cd_proj_kernel(x_ref, g_ref, w1_ref, qn_ref, wuq_ref, kvn_ref, wk_ref, wv_ref, bf_ref, cos_ref, sin_ref,
                    selq_ref, selk_ref, oneq_ref, onek_ref, hsel_ref,
                    qc_ref, kc_ref, vct_ref, qd_ref, kd_ref, vdt_ref, kcnorm_ref, knorm_ref, fend_ref,
                    carry_ref, kcarry_ref, kccarry_ref):
    t = pl.program_id(1)
    tm = x_ref.shape[1]
    h = _rms(x_ref[0], g_ref[...]).astype(BF16)
    p = jnp.dot(h, w1_ref[...], preferred_element_type=F32)

    cq = _rms(p[:, _C_Q0:_C_KV0], qn_ref[...]).astype(BF16)
    ckv = _rms(p[:, _C_KV0:_C_KR0], kvn_ref[...]).astype(BF16)

    cosb = cos_ref[...]
    sinb = sin_ref[...]
    lane = lax.broadcasted_iota(jnp.int32, (tm, LANES), 1)

    def rope(xb):
        partner = jnp.where(lane < QK_NOPE + QK_ROPE // 2, pltpu.roll(xb, LANES - QK_ROPE // 2, 1),
                            pltpu.roll(xb, QK_ROPE // 2, 1))
        return xb * cosb + partner * sinb

    qc = jnp.dot(cq, wuq_ref[...], preferred_element_type=F32) * ((QK_NOPE + QK_ROPE) ** -0.5 * LOG2E)
    kc = jnp.dot(ckv, wk_ref[...], preferred_element_type=F32)
    kr = rope(p[:, _C_KR0:_C_QD0])
    kc_heads = []
    for hh in range(N_HEADS):
        sl = slice(hh * LANES, (hh + 1) * LANES)
        qc_ref[0, :, sl] = rope(qc[:, sl]).astype(BF16)
        kc_heads.append(kc[:, sl] + kr)
        kc_ref[0, :, sl] = kc_heads[hh].astype(BF16)
    kc = jnp.concatenate(kc_heads, axis=1)
    _store_pairs_transposed(vct_ref, jnp.dot(ckv, wv_ref[...], preferred_element_type=F32))
    _store_pairs_transposed(vdt_ref, p[:, _C_VD0:_C_F0])

    fl = p[:, _C_F0:_C_END] + bf_ref[...]
    y = jnp.minimum(fl, 0.0) - jnp.log(1.0 + jnp.exp(-jnp.abs(fl)))
    row = lax.broadcasted_iota(jnp.int32, (tm, LANES), 0)
    sh = 1
    while sh < tm:
        y = y + jnp.where(row >= sh, pltpu.roll(y, sh, 0), 0.0)
        sh *= 2

    @pl.when(t == 0)
    def _():
        carry_ref[...] = jnp.zeros_like(carry_ref)
        kcarry_ref[...] = jnp.zeros_like(kcarry_ref)
        kccarry_ref[...] = jnp.zeros_like(kccarry_ref)

    y = y + carry_ref[0:1, :]
    carry_ref[...] = jnp.broadcast_to(y[tm - 1:tm, :], carry_ref.shape)
    f2 = y * LOG2E

    hi = f2.astype(BF16).astype(F32)
    r1 = f2 - hi
    mid = r1.astype(BF16).astype(F32)
    lo = r1 - mid
    fp = jnp.where(lane < N_HEADS, hi, jnp.where(lane < 2 * N_HEADS, pltpu.roll(mid, N_HEADS, 1),
                                                 pltpu.roll(lo, 2 * N_HEADS, 1))).astype(BF16)
    def one_head_per_block(x2):
        blocks = []
        for hh in range(N_HEADS):
            src = x2[:, (hh // 2) * LANES:(hh // 2 + 1) * LANES]
            if hh % 2:
                src = pltpu.roll(src, HEAD_DIM, 1)
            blocks.append(jnp.where(lane < HEAD_DIM, src, 0.0))
        return jnp.concatenate(blocks, axis=1)

    qd = one_head_per_block(p[:, _C_QD0:_C_KD0]) * (HEAD_DIM ** -0.5 * LOG2E)
    kd = one_head_per_block(p[:, _C_KD0:_C_VD0])
    qd_ref[0] = (qd +jnp.dot(fp, selq_ref[...], preferred_element_type=F32) + oneq_ref[...]).astype(BF16)
    kd_ref[0] = (kd + jnp.dot(fp, selk_ref[...], preferred_element_type=F32) + onek_ref[...]).astype(BF16)

    _running_max_norm(kc, hsel_ref, kccarry_ref, kcnorm_ref)
    _running_max_norm(kd, hsel_ref, kcarry_ref, knorm_ref)
    for i in range(tm // TK):
        fend_ref[0, i] = jnp.broadcast_to(f2[(i + 1) * TK - 1:(i + 1) * TK, :], fend_ref.shape[2:])


def _cd_proj(x, g, w1, qn, wuq, kvn, wk, wv, bf, cos_t, sin_t, selq, selk, oneq, onek, hsel):
    b, s, d = x.shape
    tm = TM_PROJ
    const = lambda a: pl.BlockSpec(a.shape, lambda bi, ti: (0,) * a.ndim)
    tok = lambda nc: pl.BlockSpec((1, tm, nc), lambda bi, ti: (bi, ti, 0))
    vt_spec = pl.BlockSpec((1, N_PAIRS, LANES, tm), lambda bi, ti: (bi, 0, 0, ti))
    tab_spec = pl.BlockSpec((1, tm // TK, SUBLANES, LANES), lambda bi, ti: (bi, ti, 0, 0))
    act = jax.ShapeDtypeStruct((b, s, _NPAD), BF16)
    vt = jax.ShapeDtypeStruct((b, N_PAIRS, LANES, s), BF16)
    tab = jax.ShapeDtypeStruct((b, s // TK, SUBLANES, LANES), F32)
    return pl.pallas_call(
        _cd_proj_kernel,
        grid=(b, s // tm),
        in_specs=[
            tok(d), const(g), const(w1), const(qn), const(wuq), const(kvn), const(wk), const(wv), const(bf),
            pl.BlockSpec((tm, LANES), lambda bi, ti: (ti, 0)),
            pl.BlockSpec((tm, LANES), lambda bi, ti: (ti, 0)),
            const(selq), const(selk), const(oneq), const(onek), const(hsel),
        ],
        out_specs=[tok(_NPAD), tok(_NPAD), vt_spec, tok(_NPAD), tok(_NPAD), vt_spec, tab_spec, tab_spec, tab_spec],
        out_shape=[act, act, vt, act, act, vt, tab, tab, tab],
        scratch_shapes=[pltpu.VMEM((SUBLANES, LANES), F32)] * 3,
        compiler_params=pltpu.CompilerParams(
            dimension_semantics=("arbitrary", "arbitrary"), vmem_limit_bytes=VMEM_LIMIT),
        name="cd_norm_proj",
    )(x, g, w1, qn, wuq, kvn, wk, wv, bf, cos_t, sin_t, selq, selk, oneq, onek, hsel)


def _mask_pair_heads(q_ref, q2_ref):
    tq = q_ref.shape[1]
    lane = lax.broadcasted_iota(jnp.int32, (tq, LANES), 1)
    for h in range(q2_ref.shape[0]):
        q = q_ref[0, :, (h // 2) * LANES:(h // 2 + 1) * LANES]
        q2_ref[h] = jnp.where((lane >= HEAD_DIM) == bool(h % 2), q, jnp.zeros_like(q))


def _store_pair_heads(o_ref, outs):
    tq = outs[0].shape[1]
    lane = lax.broadcasted_iota(jnp.int32, (tq, LANES), 1)
    for pr in range(len(outs) // 2):
        o_ref[0, :, pr * LANES:(pr + 1) * LANES] = jnp.where(
            lane < HEAD_DIM, outs[2 * pr].T, outs[2 * pr + 1].T).astype(o_ref.dtype)


def _softmax_tile_update(ss, vts, m_ref, acc_ref):
    heads = range(len(ss))
    m_old = [m_ref[h] for h in heads]
    acc_old = [acc_ref[h] for h in heads]
    m_new = [functools.reduce(jnp.maximum, [jnp.max(s, axis=0, keepdims=True) for s in ss[h]], m_old[h])
             for h in heads]
    alpha = [jnp.exp2(m_old[h] - m_new[h]) for h in heads]
    ps = [[jnp.exp2(s - m_new[h]).astype(BF16) for s in ss[h]] for h in heads]
    ones = jnp.ones((SUM_ROWS, ss[0][0].shape[0]), BF16)
    pvs = [sum(jnp.dot(jnp.concatenate([vt, ones], axis=0), p, preferred_element_type=F32)
               for vt, p in zip(vts[h], ps[h])) for h in heads]
    for h in heads:
        m_ref[h] = m_new[h]
        acc_ref[h] = alpha[h] * acc_old[h] + pvs[h]


def _softmax_tile_update_fixed(ss, vts, m_ref, acc_ref):
    heads = range(len(ss))
    acc_old = [acc_ref[h] for h in heads]
    ps = [[jnp.exp2(s - m_ref[h]).astype(BF16) for s in ss[h]] for h in heads]
    ones = jnp.ones((SUM_ROWS, ss[0][0].shape[0]), BF16)
    pvs = [sum(jnp.dot(jnp.concatenate([vt, ones], axis=0), p, preferred_element_type=F32)
               for vt, p in zip(vts[h], ps[h])) for h in heads]
    for h in heads:
        acc_ref[h] = acc_old[h] + pvs[h]


def _normalised(acc_ref, h):
    return acc_ref[h, 0:LANES, :] / acc_ref[h, LANES:LANES + 1, :]


def _chunk_attn_kernel(q_ref, k_ref, vt_ref, bias_ref, knorm_ref, bmax_ref, o_ref, q2_ref, m_ref, acc_ref):
    bi = pl.program_id(0)
    qi = pl.program_id(2)
    tq = q_ref.shape[1]
    heads = range(acc_ref.shape[0])
    _mask_pair_heads(q_ref, q2_ref)
    ones = jnp.ones((SUBLANES, LANES), BF16)
    qnorm = []
    for h in heads:
        q32 = q2_ref[h].astype(F32)
        n2 = lax.dot_general(ones, (q32 * q32).astype(BF16), NT_DIMS, preferred_element_type=F32)[0:1]
        qnorm.append(jnp.sqrt(n2) * NORM_MARGIN)

    def run(fixed, js):
        update = _softmax_tile_update_fixed if fixed else _softmax_tile_update
        acc_ref[...] = jnp.zeros_like(acc_ref)
        for h in heads:
            m_ref[h] = ((qnorm[h] * knorm_ref[bi, qi, h] + bmax_ref[h]) if fixed
                        else jnp.full((1, tq), NEG_BIG, F32))
        kstarts = [pl.multiple_of((qi - (N_WIN_A - 1) + j) * tq, tq) for j in js]
        ss = [[lax.dot_general(k_ref[0, pl.ds(ks, tq), (h // 2) * LANES:(h // 2 + 1) * LANES], q2_ref[h],
                               NT_DIMS, preferred_element_type=F32) + bias_ref[h, j] for j, ks in zip(js, kstarts)]
              for h in heads]
        vts = [[vt_ref[0, h // 2, :, pl.ds(ks, tq)] for ks in kstarts] for h in heads]
        update(ss, vts, m_ref, acc_ref)
        _store_pair_heads(o_ref, [_normalised(acc_ref, h) for h in heads])

    def run_window(fixed):
        first = jnp.maximum(N_WIN_A - 1 - qi, 0)
        for f in range(N_WIN_A):
            pl.when(first == f)(functools.partial(run, fixed, list(range(f, N_WIN_A))))

    run_window(True)
    smallest = jnp.min(functools.reduce(jnp.minimum, [acc_ref[h, LANES:LANES + 1, :] for h in heads]))
    pl.when(jnp.logical_not(smallest >= 2.0 ** -SAFE_DENOM_LOG2))(lambda: run_window(False))


def _chunk_attn(proj, vt, bias, knorm, bmax):
    b, s, _ = proj.shape
    nh = HEADS_PER_STEP
    assert nh == N_HEADS
    width = nh * HEAD_DIM
    k0 = N_HEADS * HEAD_DIM // width
    resident = dict(pipeline_mode=pl.Buffered(1))
    return pl.pallas_call(
        _chunk_attn_kernel,
        grid=(b, N_HEADS // nh, s // TK),
        in_specs=[
            pl.BlockSpec((1, TK, width), lambda bi, hg, qi: (bi, qi, hg)),
            pl.BlockSpec((1, s, width), lambda bi, hg, qi: (bi, 0, k0 + hg), **resident),
            pl.BlockSpec((1, nh // 2, LANES, s), lambda bi, hg, qi: (bi, hg, 0, 0), **resident),
            pl.BlockSpec((nh, N_WIN_A, TK, TK), lambda bi, hg, qi: (hg, 0, 0, 0), **resident),
            pl.BlockSpec(memory_space=pltpu.SMEM),
            pl.BlockSpec(memory_space=pltpu.SMEM),
        ],
        out_specs=pl.BlockSpec((1, TK, width), lambda bi, hg, qi: (bi, qi, hg)),
        out_shape=jax.ShapeDtypeStruct((b, s, N_HEADS * HEAD_DIM), BF16),
        scratch_shapes=[pltpu.VMEM((nh, TK, LANES), BF16), pltpu.VMEM((nh, 1, TK), F32),
                        pltpu.VMEM((nh, LANES + SUM_ROWS, TK), F32)],
        compiler_params=pltpu.CompilerParams(
            dimension_semantics=("arbitrary", "arbitrary", "arbitrary"), vmem_limit_bytes=VMEM_LIMIT),
        name="chunk_attn",
    )(proj, proj, vt, bias, knorm, bmax)


def _chunk_bias_tiles(rel_bias):
    h = rel_bias.shape[0]
    nq = TK
    nk = N_WIN_A * TK
    period = 1 << (nq + nk - 1).bit_length()
    u = np.arange(period)
    signed = np.where(u < nk, u, u - period)
    idx = np.clip(LEFT_CHUNKS * CHUNK - signed, -MAX_REL, MAX_REL) + MAX_REL
    v = rel_bias.astype(F32)[:, idx] * LOG2E
    toep = jnp.tile(v, (1, nq))[:, :nq * (period - 1)].reshape(h, nq, period - 1)[:, :, :nk]
    r = np.arange(nq)[:, None]
    off = np.arange(nk)[None, :] - CHUNK * (r // CHUNK)
    in_band = (off >= 0) & (off < BAND)
    bias = jnp.where(in_band[None], toep, NEG_BIG)
    return bias.reshape(h, nq, N_WIN_A, TK).transpose(0, 2, 3, 1)


def _stick_kernel(q_ref, k_ref, vt_ref, tri_ref, o_ref, q2_ref, c_ref, acc_ref):
    qi = pl.program_id(2)
    tq = q_ref.shape[1]
    tk = tri_ref.shape[0]
    heads = range(acc_ref.shape[0])
    pair = lambda h: slice((h // 2) * LANES, (h // 2 + 1) * LANES)
    _mask_pair_heads(q_ref, q2_ref)
    c_ref[...] = jnp.zeros_like(c_ref)
    acc_ref[...] = jnp.zeros_like(acc_ref)
    sign_bit = jnp.uint32(0x80000000)

    def tiles(js, masked):
        nt = range(len(js))
        kstarts = [pl.multiple_of(j * tk, tk) for j in js]
        zs = [[lax.dot_general(k_ref[0, pl.ds(kstarts[t], tk), pair(h)], q2_ref[h], NT_DIMS,
                               preferred_element_type=F32) for t in nt] for h in heads]
        c_old = [c_ref[h] for h in heads]
        acc_old = [acc_ref[h] for h in heads]

        def neg_log_keep(z, t):
            neg_abs = lax.bitcast_convert_type(lax.bitcast_convert_type(z, jnp.uint32) | sign_bit, F32)
            nlk = jnp.maximum(z, 0.0) + jnp.log(1.0 + jnp.exp2(neg_abs)) * LOG2E
            return jnp.where(masks[t], nlk, 0.0) if masked[t] else nlk

        masks = [None] * len(js)
        for t in nt:
            if masked[t]:
                key = kstarts[t] + lax.broadcasted_iota(jnp.int32, (tk, tq), 0)
                qry = qi * tq + lax.broadcasted_iota(jnp.int32, (tk, tq), 1)
                masks[t] = key < qry
        nlk = [[neg_log_keep(zs[h][t], t) for t in nt] for h in heads]
        rs = [[jnp.dot(tri_ref[...], nlk[h][t].astype(BF16), preferred_element_type=F32) for t in nt] for h in heads]
        pvs = []
        c_new = []
        for h in heads:
            c = c_old[h]
            pv = None
            for t in nt:
                w = jnp.exp2((zs[h][t] - nlk[h][t]) - rs[h][t] - c)
                if masked[t]:
                    w = jnp.where(masks[t], w, 0.0)
                d = jnp.dot(vt_ref[0, h // 2, :, pl.ds(kstarts[t], tk)], w.astype(BF16),
                            preferred_element_type=F32)
                pv = d if pv is None else pv + d
                c = c + jnp.sum(nlk[h][t], axis=0, keepdims=True)
            pvs.append(pv)
            c_new.append(c)
        for h in heads:
            c_ref[h] = c_new[h]
            acc_ref[h] = acc_old[h] + pvs[h]

    assert tq == tk
    pl.when(qi > 0)(lambda: tiles([qi, qi - 1], [True, False]))
    pl.when(qi == 0)(lambda: tiles([qi], [True]))

    def more(j):
        return jnp.logical_and(j >= 0, jnp.min(c_ref[...]) < STICK_UNDERFLOW_LOG2)

    def body(j):
        tiles([j], [False])
        return j - 1

    lax.while_loop(more, body, qi - 2)
    _store_pair_heads(o_ref, [acc_ref[h] for h in heads])


def _stick_attn(proj, vt, tri):
    b, s, _ = proj.shape
    nh = HEADS_PER_STEP
    width = nh * HEAD_DIM
    q0 = 3 * N_HEADS * HEAD_DIM // width
    k0 = 4 * N_HEADS * HEAD_DIM // width
    resident = dict(pipeline_mode=pl.Buffered(1))
    return pl.pallas_call(
        _stick_kernel,
        grid=(b, N_HEADS // nh, s // TQ),
        in_specs=[
            pl.BlockSpec((1, TQ, width), lambda bi, hg, qi: (bi, qi, q0 + hg)),
            pl.BlockSpec((1, s, width), lambda bi, hg, qi: (bi, 0, k0 + hg), **resident),
            pl.BlockSpec((1, nh // 2, LANES, s), lambda bi, hg, qi: (bi, hg, 0, 0), **resident),
            pl.BlockSpec((TK, TK), lambda bi, hg, qi: (0, 0)),
        ],
        out_specs=pl.BlockSpec((1, TQ, width), lambda bi, hg, qi: (bi, qi, hg)),
        out_shape=jax.ShapeDtypeStruct((b, s, N_HEADS * HEAD_DIM), BF16),
        scratch_shapes=[pltpu.VMEM((nh, TQ, LANES), BF16), pltpu.VMEM((nh, 1, TQ), F32),
                        pltpu.VMEM((nh, LANES, TQ), F32)],
        compiler_params=pltpu.CompilerParams(
            dimension_semantics=("arbitrary", "arbitrary", "arbitrary"), vmem_limit_bytes=VMEM_LIMIT),
        name="stick_attn",
    )(proj, proj, vt, tri)


def _softmax_attn_kernel(*refs, chunk_mask, decay_skip):
    if decay_skip:
        q_ref, k_ref, vt_ref, knorm_ref, fend_ref, o_ref, m_ref, acc_ref = refs
    else:
        q_ref, k_ref, vt_ref, knorm_ref, o_ref, m_ref, acc_ref = refs
    bi = pl.program_id(0)
    qi = pl.program_id(2)
    tq = q_ref.shape[1]
    tk = TK
    heads = range(q_ref.shape[2] // LANES)
    sl = lambda h: slice(h * LANES, (h + 1) * LANES)
    assert tq == tk
    g = TILES_PER_STEP
    rem = qi % g

    lane = lax.broadcasted_iota(jnp.int32, (SUBLANES, LANES), 1)
    dims = jnp.where(lane < (HEAD_DIM if decay_skip else LANES), 1.0, 0.0).astype(BF16)
    pieces = jnp.where((lane >= F_LANE0) & (lane < F_LANE0 + N_PIECES), 1.0, 0.0).astype(BF16)
    qnorm, fq = [], []
    for h in heads:
        q = q_ref[0, :, sl(h)]
        q32 = q.astype(F32)
        n2 = lax.dot_general(dims, (q32 * q32).astype(BF16), NT_DIMS, preferred_element_type=F32)[0:1]
        qnorm.append(jnp.sqrt(n2) * NORM_MARGIN)
        if decay_skip:
            fq.append(lax.dot_general(pieces, q, NT_DIMS, preferred_element_type=F32)[0:1])

    def run(fixed):
        update = _softmax_tile_update_fixed if fixed else _softmax_tile_update
        acc_ref[...] = jnp.zeros_like(acc_ref)
        for h in heads:
            m_ref[h] = (qnorm[h] * knorm_ref[bi, qi, h] + 1.0) if fixed else jnp.full((1, tq), NEG_BIG, F32)

        def tiles(js, masked):
            kstarts = [pl.multiple_of(j * tk, tk) for j in js]

            def scores(h, t):
                s = lax.dot_general(k_ref[0, pl.ds(kstarts[t], tk), sl(h)], q_ref[0, :, sl(h)], NT_DIMS,
                                    preferred_element_type=F32)
                if masked[t]:
                    key = kstarts[t] + lax.broadcasted_iota(jnp.int32, (tk, tq), 0)
                    qry = qi * tq + lax.broadcasted_iota(jnp.int32, (tk, tq), 1)
                    s = jnp.where((key // CHUNK <= qry // CHUNK) if chunk_mask else (key <= qry), s, NEG_BIG)
                return s

            ss = [[scores(h, t) for t in range(len(js))] for h in heads]
            vts = [[vt_ref[0, h // 2, :, pl.ds(ks, tk)] for ks in kstarts] for h in heads]
            update(ss, vts, m_ref, acc_ref)

        def diagonal_step():
            for r in range(g):
                pl.when(rem == r)(functools.partial(tiles, [qi - r + t for t in range(r + 1)], [False] * r + [True]))

        if not decay_skip:
            def body(i, carry):
                tiles([g * i + t for t in range(g)], [False] * g)
                return carry

            lax.fori_loop(0, qi // g, body, 0)
            diagonal_step()
        else:
            diagonal_step()
            top = qi - 1 - rem

            def more(it):
                j = jnp.maximum(top - g * it, 0)
                if fixed:
                    level = [m_ref[h] + jnp.log2(acc_ref[h, LANES:LANES + 1, :]) for h in heads]
                else:
                    level = [m_ref[h] for h in heads]
                gap = [qnorm[h] * knorm_ref[bi, j, h] + fq[h] - fend_ref[bi, j, h] - level[h] for h in heads]
                reach = jnp.max(functools.reduce(jnp.maximum, gap))
                return jnp.logical_and(it < qi // g, reach > -SOFTMAX_UNDERFLOW_LOG2)

            def body(it):
                tiles([top - g * it - t for t in range(g)], [False] * g)
                return it + 1

            lax.while_loop(more, body, 0)
        _store_pair_heads(o_ref, [_normalised(acc_ref, h) for h in heads])

    run(True)
    smallest = jnp.min(functools.reduce(jnp.minimum, [acc_ref[h, LANES:LANES + 1, :] for h in heads]))
    pl.when(jnp.logical_not(smallest >= 2.0 ** -SAFE_DENOM_LOG2))(lambda: run(False))


def _softmax_attn(q, k, vt, knorm, chunk_mask, name, fend=None):
    b, s, _ = q.shape
    nh = HEADS_PER_STEP
    assert nh == N_HEADS
    resident = dict(pipeline_mode=pl.Buffered(1))
    tables = [knorm] + ([] if fend is None else [fend])
    return pl.pallas_call(
        functools.partial(_softmax_attn_kernel, chunk_mask=chunk_mask, decay_skip=fend is not None),
        grid=(b, N_HEADS // nh, s // TQ),
        in_specs=[
            pl.BlockSpec((1, TQ, nh * LANES), lambda bi, hg, qi: (bi, qi, hg)),
            pl.BlockSpec((1, s, nh * LANES), lambda bi, hg, qi: (bi, 0, hg), **resident),
            pl.BlockSpec((1, nh // 2, LANES, s), lambda bi, hg, qi: (bi, hg, 0, 0), **resident),
        ] + [pl.BlockSpec(memory_space=pltpu.SMEM)] * len(tables),
        out_specs=pl.BlockSpec((1, TQ, nh * HEAD_DIM), lambda bi, hg, qi: (bi, qi, hg)),
        out_shape=jax.ShapeDtypeStruct((b, s, N_HEADS * HEAD_DIM), BF16),
        scratch_shapes=[pltpu.VMEM((nh, 1, TQ), F32), pltpu.VMEM((nh, LANES + SUM_ROWS, TQ), F32)],
        compiler_params=pltpu.CompilerParams(
            dimension_semantics=("arbitrary", "arbitrary", "arbitrary"), vmem_limit_bytes=VMEM_LIMIT),
        name=name,
    )(q, k, vt, *tables)


def _out_ffn_kernel(*refs, tiles_per_seq, final_norm):
    if final_norm:
        (x_ref, o1_ref, o2_ref, wo_ref, g_ref, wg_ref, wu_ref, cw_ref, cb_ref, wd_ref, fg_ref,
         out_ref, x1_ref, h_ref, acc_ref, gbuf_ref, tail_ref) = refs
    else:
        (x_ref, o1_ref, o2_ref, wo_ref, g_ref, wg_ref, wu_ref, cw_ref, cb_ref, wd_ref,
         out_ref, x1_ref, h_ref, acc_ref, gbuf_ref, tail_ref) = refs
    i = pl.program_id(0)
    f = pl.program_id(1)
    nf = pl.num_programs(1)
    tm = x_ref.shape[0]
    half = o1_ref.shape[1]

    @pl.when(f == 0)
    def _():
        x1 = (x_ref[...]
              + jnp.dot(o1_ref[...], wo_ref[0:half, :], preferred_element_type=F32)
              + jnp.dot(o2_ref[...], wo_ref[half:2 * half, :], preferred_element_type=F32))
        x1_ref[...] = x1
        h_ref[...] = _rms(x1, g_ref[...]).astype(BF16)
        acc_ref[...] = jnp.zeros_like(acc_ref)

    h = h_ref[...]
    g = jnp.dot(h, wg_ref[...], preferred_element_type=F32)
    u = jnp.dot(h, wu_ref[...], preferred_element_type=F32)

    prev = jnp.where(i % tiles_per_seq == 0, 0.0, tail_ref[f])
    halo = SUBLANES
    gbuf_ref[0:halo, :] = prev
    gbuf_ref[halo:halo + tm, :] = g
    tail_ref[f] = g[tm - halo:tm, :]
    gm1 = gbuf_ref[halo - 1:halo - 1 + tm, :]
    gm2 = gbuf_ref[halo - 2:halo - 2 + tm, :]
    cw = cw_ref[...]
    gc = cw[0:1, :] * gm2 + cw[1:2, :] * gm1 + cw[2:3, :] * g + cb_ref[...]
    y = (gc / (1.0 + jnp.exp(-gc)) * u).astype(BF16)
    acc_ref[...] += jnp.dot(y, wd_ref[...], preferred_element_type=F32)

    @pl.when(f == nf - 1)
    def _():
        res = x1_ref[...] + acc_ref[...]
        if final_norm:
            res = _rms(res, fg_ref[...])
        out_ref[...] = res


def _out_ffn(x2d, o1, o2, wo, g, wg, wu, cw, cb, wd, final_g, seq_len):
    n, d = x2d.shape
    half = o1.shape[1]
    dff = wg.shape[1]
    tm, tf = TM_FFN, TF_FFN
    nf = dff // tf
    final_norm = final_g is not None
    once = dict(pipeline_mode=pl.Buffered(1))
    per_f = once if nf == 1 else {}
    in_specs = [
        pl.BlockSpec((tm, d), lambda i, f: (i, 0)),
        pl.BlockSpec((tm, half), lambda i, f: (i, 0)),
        pl.BlockSpec((tm, half), lambda i, f: (i, 0)),
        pl.BlockSpec((d, d), lambda i, f: (0, 0), **once),
        pl.BlockSpec((1, d), lambda i, f: (0, 0)),
        pl.BlockSpec((d, tf), lambda i, f: (0, f), **per_f),
        pl.BlockSpec((d, tf), lambda i, f: (0, f), **per_f),
        pl.BlockSpec((3, tf), lambda i, f: (0, f)),
        pl.BlockSpec((1, tf), lambda i, f: (0, f)),
        pl.BlockSpec((tf, d), lambda i, f: (f, 0), **per_f),
    ]
    args = [x2d, o1, o2, wo, g, wg, wu, cw, cb, wd]
    if final_norm:
        in_specs.append(pl.BlockSpec((1, d), lambda i, f: (0, 0)))
        args.append(final_g)
    return pl.pallas_call(
        functools.partial(_out_ffn_kernel, tiles_per_seq=seq_len // tm, final_norm=final_norm),
        grid=(n // tm, nf),
        in_specs=in_specs,
        out_specs=pl.BlockSpec((tm, d), lambda i, f: (i, 0)),
        out_shape=jax.ShapeDtypeStruct((n, d), F32),
        scratch_shapes=[
            pltpu.VMEM((tm, d), F32),
            pltpu.VMEM((tm, d), BF16),
            pltpu.VMEM((tm, d), F32),
            pltpu.VMEM((tm + SUBLANES, tf), F32),
            pltpu.VMEM((nf, SUBLANES, tf), F32),
        ],
        compiler_params=pltpu.CompilerParams(
            dimension_semantics=("arbitrary", "arbitrary"), vmem_limit_bytes=VMEM_LIMIT),
        name="out_ffn_final" if final_norm else "out_ffn",
    )(*args)


def _rope_tables(seq_len):
    half = QK_ROPE // 2
    inv = ROPE_THETA ** (-jnp.arange(half, dtype=F32) / half)
    ang = jnp.arange(seq_len, dtype=F32)[:, None] * inv[None, :]
    cos, sin = jnp.cos(ang), jnp.sin(ang)
    ones = jnp.ones((seq_len, QK_NOPE), F32)
    zeros = jnp.zeros((seq_len, QK_NOPE), F32)
    pad1 = jnp.ones((seq_len, LANES - QK_NOPE - QK_ROPE), F32)
    pad0 = jnp.zeros((seq_len, LANES - QK_NOPE - QK_ROPE), F32)
    return (jnp.concatenate([ones, cos, cos, pad1], axis=1),
            jnp.concatenate([zeros, -sin, sin, pad0], axis=1))


def _pad_heads(w, width):
    rows = w.shape[0]
    return jnp.zeros((rows, N_HEADS, LANES), w.dtype).at[:, :, :width].set(
        w.reshape(rows, N_HEADS, width)).reshape(rows, N_HEADS * LANES)


def _cd_weights(w_in, w_uq, w_ukv, b_f):
    d = w_in.shape[0]
    nd = N_HEADS * HEAD_DIM
    o = Q_RANK + KV_RANK + QK_ROPE
    c_q, c_kv, k_rope = w_in[:, :Q_RANK], w_in[:, Q_RANK:Q_RANK + KV_RANK], w_in[:, Q_RANK + KV_RANK:o]
    q_d, k_d, v_d, f_logit = (w_in[:, o:o + nd], w_in[:, o + nd:o + 2 * nd], w_in[:, o + 2 * nd:o + 3 * nd],
                              w_in[:, o + 3 * nd:])
    kr_blk = jnp.zeros((d, LANES), w_in.dtype).at[:, QK_NOPE:QK_NOPE + QK_ROPE].set(k_rope)
    f_blk = jnp.zeros((d, LANES), w_in.dtype).at[:, :N_HEADS].set(f_logit)
    w1 = jnp.concatenate([c_q, c_kv, kr_blk, q_d, k_d, v_d, f_blk], axis=1).astype(BF16)
    wuq = _pad_heads(w_uq, QK_NOPE + QK_ROPE).astype(BF16)
    ukv = w_ukv.reshape(KV_RANK, N_HEADS, QK_NOPE + HEAD_DIM)
    wk = _pad_heads(ukv[:, :, :QK_NOPE].reshape(KV_RANK, N_HEADS * QK_NOPE), QK_NOPE).astype(BF16)
    wv = ukv[:, :, QK_NOPE:].reshape(KV_RANK, N_HEADS * HEAD_DIM).astype(BF16)
    bf = jnp.zeros((1, LANES), F32).at[0, :N_HEADS].set(b_f.astype(F32))
    return w1, wuq, wk, wv, bf


def _forget_selectors():
    selq = np.zeros((LANES, _NPAD), np.float32)
    selk = np.zeros((LANES, _NPAD), np.float32)
    oneq = np.zeros((1, _NPAD), np.float32)
    onek = np.zeros((1, _NPAD), np.float32)
    hsel = np.zeros((_NPAD, LANES), np.float32)
    for h in range(N_HEADS):
        hsel[h * LANES:(h + 1) * LANES, h] = 1.0
        for j in range(N_PIECES):
            selq[j * N_HEADS + h, h * LANES + F_LANE0 + j] = 1.0
            onek[0, h * LANES + F_LANE0 + j] = 1.0
            selk[j * N_HEADS + h, h * LANES + ONE_LANE0 + j] = -1.0
            oneq[0, h * LANES + ONE_LANE0 + j] = 1.0
    return (jnp.asarray(selq, BF16), jnp.asarray(selk, BF16), jnp.asarray(oneq), jnp.asarray(onek),
            jnp.asarray(hsel, BF16))


def kernel(x, ab_norm, ab_w_in, ab_rel_bias, ab_w_o, cd_norm, cd_w_in, cd_q_norm, cd_w_uq, cd_kv_norm, cd_w_ukv,
           cd_b_f, cd_w_o, ffn_norm, ffn_w_gate, ffn_w_up, ffn_conv_w, ffn_conv_b, ffn_w_down, final_norm):
    b, s, d = x.shape
    n = b * s

    qscale = jnp.full((N_HEADS * HEAD_DIM,), HEAD_DIM ** -0.5 * LOG2E, F32)
    one = jnp.ones((2 * N_HEADS * HEAD_DIM,), F32)
    colscale = jnp.concatenate([qscale, one, qscale, one])[None, :]
    per_head = lambda tab: tab[:, :, 0, :N_HEADS]
    hsel_a = jnp.asarray(np.repeat(np.eye(N_HEADS, LANES, dtype=np.float32), HEAD_DIM, axis=0), BF16)
    proj, vat, vbt, kanorm = _ab_proj(x, ab_norm[0][None, :], ab_w_in[0].astype(BF16), colscale, hsel_a)
    bias_max = jnp.max(ab_rel_bias[0].astype(F32), axis=1) * LOG2E + 1.0
    oa = _chunk_attn(proj, vat, _chunk_bias_tiles(ab_rel_bias[0]), per_head(kanorm), bias_max)
    tri = jnp.asarray(np.triu(np.ones((TK, TK), np.float32), k=1), BF16)
    ob = _stick_attn(proj, vbt, tri)
    x2d = _out_ffn(x.reshape(n, d), oa.reshape(n, -1), ob.reshape(n, -1), ab_w_o[0].astype(BF16),
                   ffn_norm[0][None, :], ffn_w_gate[0].astype(BF16), ffn_w_up[0].astype(BF16), ffn_conv_w[0],
                   ffn_conv_b[0][None, :], ffn_w_down[0].astype(BF16), None, s)

    w1, wuq, wk, wv, bf = _cd_weights(cd_w_in[0], cd_w_uq[0], cd_w_ukv[0], cd_b_f[0])
    cos_t, sin_t = _rope_tables(s)
    qc, kc, vct, qd, kd, vdt, kcnorm, kdnorm, fend = _cd_proj(
        x2d.reshape(b, s, d), cd_norm[0][None, :], w1, cd_q_norm[0][None, :], wuq, cd_kv_norm[0][None, :], wk, wv, bf,
        cos_t, sin_t, *_forget_selectors())
    oc = _softmax_attn(qc, kc, vct, per_head(kcnorm), True, "mla_attn")
    od = _softmax_attn(qd, kd, vdt, per_head(kdnorm), False, "fox_attn", per_head(fend))
    out = _out_ffn(x2d, oc.reshape(n, -1), od.reshape(n, -1), cd_w_o[0].astype(BF16), ffn_norm[1][None, :],
                   ffn_w_gate[1].astype(BF16), ffn_w_up[1].astype(BF16), ffn_conv_w[1], ffn_conv_b[1][None, :],
                   ffn_w_down[1].astype(BF16), final_norm[None, :], s)
    return out.reshape(b, s, d)
```

```python
import functools
import math

import numpy as np
import jax
import jax.numpy as jnp
from jax import lax
from jax.experimental import pallas as pl
from jax.experimental.pallas import tpu as pltpu

F32 = jnp.float32
BF16 = jnp.bfloat16

D_MODEL = 1024
HEAD_DIM = 64
CHUNK = 64
LEFT_CHUNKS = 8
BAND = (LEFT_CHUNKS + 1) * CHUNK
MAX_REL = 128
N_HEADS = 8
N_PAIRS = N_HEADS // 2
QK_NOPE = 64
QK_ROPE = 32
Q_RANK = 384
KV_RANK = 256
ROPE_THETA = 10000.0
D_FF = 2816
RMS_EPS = 1e-6

LANES = 128
SUBLANES = 8
LOG2E = math.log2(math.e)
NEG_BIG = -1e30
STICK_UNDERFLOW_LOG2 = 200.0
SOFTMAX_UNDERFLOW_LOG2 = 160.0
NORM_MARGIN = 1.01
SAFE_DENOM_LOG2 = 60.0
VMEM_LIMIT = 52 * 1024 * 1024

TM_PROJ = 512
TM_FFN = 256
TF_FFN = 2816
TK = 256
TQ = 256
N_WIN_A = LEFT_CHUNKS * CHUNK // TK + 1
HEADS_PER_STEP = 8
TILES_PER_STEP = 4
SUM_ROWS = 16

N_PIECES = 3
F_LANE0 = HEAD_DIM
ONE_LANE0 = HEAD_DIM + N_PIECES

NT_DIMS = (((1,), (1,)), ((), ()))


def _rms(x, g):
    ms = jnp.mean(x * x, axis=-1, keepdims=True)
    return x * lax.rsqrt(ms + RMS_EPS) * g


def _store_pairs_transposed(vt_ref, v):
    for p in range(N_PAIRS):
        vt_ref[0, p] = v[:, p * LANES:(p + 1) * LANES].T.astype(vt_ref.dtype)


def _running_max_norm(k, hsel_ref, carry_ref, out_ref):
    n2 = jnp.dot((k * k).astype(BF16), hsel_ref[...], preferred_element_type=F32)
    for i in range(k.shape[0] // TK):
        tile_max = jnp.sqrt(jnp.max(n2[i * TK:(i + 1) * TK], axis=0, keepdims=True)) * NORM_MARGIN
        kmax = jnp.maximum(carry_ref[0:1, :], tile_max)
        carry_ref[...] = jnp.broadcast_to(kmax, carry_ref.shape)
        out_ref[0, i] = jnp.broadcast_to(kmax, out_ref.shape[2:])


def _ab_proj_kernel(x_ref, g_ref, w_ref, cs_ref, hsel_ref, o_ref, vat_ref, vbt_ref, kanorm_ref, kcarry_ref):
    h = _rms(x_ref[0], g_ref[...]).astype(BF16)
    p = jnp.dot(h, w_ref[...], preferred_element_type=F32) * cs_ref[...]
    o_ref[0] = p.astype(o_ref.dtype)
    nv = N_HEADS * HEAD_DIM
    _store_pairs_transposed(vat_ref, p[:, 2 * nv:3 * nv])
    _store_pairs_transposed(vbt_ref, p[:, 5 * nv:6 * nv])

    @pl.when(pl.program_id(1) == 0)
    def _():
        kcarry_ref[...] = jnp.zeros_like(kcarry_ref)

    _running_max_norm(p[:, nv:2 * nv], hsel_ref, kcarry_ref, kanorm_ref)


def _ab_proj(x, g, w, colscale, hsel):
    b, s, d = x.shape
    nc = w.shape[1]
    tm = TM_PROJ
    vt_spec = pl.BlockSpec((1, N_PAIRS, LANES, tm), lambda bi, ti: (bi, 0, 0, ti))
    vt_shape = jax.ShapeDtypeStruct((b, N_PAIRS, LANES, s), BF16)
    return pl.pallas_call(
        _ab_proj_kernel,
        grid=(b, s // tm),
        in_specs=[
            pl.BlockSpec((1, tm, d), lambda bi, ti: (bi, ti, 0)),
            pl.BlockSpec((1, d), lambda bi, ti: (0, 0)),
            pl.BlockSpec((d, nc), lambda bi, ti: (0, 0)),
            pl.BlockSpec((1, nc), lambda bi, ti: (0, 0)),
            pl.BlockSpec(hsel.shape, lambda bi, ti: (0, 0)),
        ],
        out_specs=[pl.BlockSpec((1, tm, nc), lambda bi, ti: (bi, ti, 0)), vt_spec, vt_spec,
                   pl.BlockSpec((1, tm // TK, SUBLANES, LANES), lambda bi, ti: (bi, ti, 0, 0))],
        out_shape=[jax.ShapeDtypeStruct((b, s, nc), BF16), vt_shape, vt_shape,
                   jax.ShapeDtypeStruct((b, s // TK, SUBLANES, LANES), F32)],
        scratch_shapes=[pltpu.VMEM((SUBLANES, LANES), F32)],
        compiler_params=pltpu.CompilerParams(
            dimension_semantics=("arbitrary", "arbitrary"), vmem_limit_bytes=VMEM_LIMIT),
        name="ab_norm_proj",
    )(x, g, w, colscale, hsel)


_NPAD = N_HEADS * LANES
_C_Q0 = 0
_C_KV0 = _C_Q0 + Q_RANK
_C_KR0 = _C_KV0 + KV_RANK
_C_QD0 = _C_KR0 + LANES
_C_KD0 = _C_QD0 + N_HEADS * HEAD_DIM
_C_VD0 = _C_KD0 + N_HEADS * HEAD_DIM
_C_F0 = _C_VD0 + N_HEADS * HEAD_DIM
_C_END = _C_F0 + LANES


def _cd_proj_kernel(x_ref, g_ref, w1_ref, qn_ref, wuq_ref, kvn_ref, wk_ref, wv_ref, bf_ref, cos_ref, sin_ref,
                    selq_ref, selk_ref, oneq_ref, onek_ref, hsel_ref,
                    qc_ref, kc_ref, vct_ref, qd_ref, kd_ref, vdt_ref, kcnorm_ref, knorm_ref, fend_ref,
                    carry_ref, kcarry_ref, kccarry_ref):
    t = pl.program_id(1)
    tm = x_ref.shape[1]
    h = _rms(x_ref[0], g_ref[...]).astype(BF16)
    p = jnp.dot(h, w1_ref[...], preferred_element_type=F32)

    cq = _rms(p[:, _C_Q0:_C_KV0], qn_ref[...]).astype(BF16)
    ckv = _rms(p[:, _C_KV0:_C_KR0], kvn_ref[...]).astype(BF16)

    cosb = cos_ref[...]
    sinb = sin_ref[...]
    lane = lax.broadcasted_iota(jnp.int32, (tm, LANES), 1)

    def rope(xb):
        partner = jnp.where(lane < QK_NOPE + QK_ROPE // 2, pltpu.roll(xb, LANES - QK_ROPE // 2, 1),
                            pltpu.roll(xb, QK_ROPE // 2, 1))
        return xb * cosb + partner * sinb

    qc = jnp.dot(cq, wuq_ref[...], preferred_element_type=F32) * ((QK_NOPE + QK_ROPE) ** -0.5 * LOG2E)
    kc = jnp.dot(ckv, wk_ref[...], preferred_element_type=F32)
    kr = rope(p[:, _C_KR0:_C_QD0])
    kc_heads = []
    for hh in range(N_HEADS):
        sl = slice(hh * LANES, (hh + 1) * LANES)
        qc_ref[0, :, sl] = rope(qc[:, sl]).astype(BF16)
        kc_heads.append(kc[:, sl] + kr)
        kc_ref[0, :, sl] = kc_heads[hh].astype(BF16)
    kc = jnp.concatenate(kc_heads, axis=1)
    _store_pairs_transposed(vct_ref, jnp.dot(ckv, wv_ref[...], preferred_element_type=F32))
    _store_pairs_transposed(vdt_ref, p[:, _C_VD0:_C_F0])

    fl = p[:, _C_F0:_C_END] + bf_ref[...]
    y = jnp.minimum(fl, 0.0) - jnp.log(1.0 + jnp.exp(-jnp.abs(fl)))
    row = lax.broadcasted_iota(jnp.int32, (tm, LANES), 0)
    sh = 1
    while sh < tm:
        y = y + jnp.where(row >= sh, pltpu.roll(y, sh, 0), 0.0)
        sh *= 2

    @pl.when(t == 0)
    def _():
        carry_ref[...] = jnp.zeros_like(carry_ref)
        kcarry_ref[...] = jnp.zeros_like(kcarry_ref)
        kccarry_ref[...] = jnp.zeros_like(kccarry_ref)

    y = y + carry_ref[0:1, :]
    carry_ref[...] = jnp.broadcast_to(y[tm - 1:tm, :], carry_ref.shape)
    f2 = y * LOG2E

    hi = f2.astype(BF16).astype(F32)
    r1 = f2 - hi
    mid = r1.astype(BF16).astype(F32)
    lo = r1 - mid
    fp = jnp.where(lane < N_HEADS, hi, jnp.where(lane < 2 * N_HEADS, pltpu.roll(mid, N_HEADS, 1),
                                                 pltpu.roll(lo, 2 * N_HEADS, 1))).astype(BF16)
    def one_head_per_block(x2):
        blocks = []
        for hh in range(N_HEADS):
            src = x2[:, (hh // 2) * LANES:(hh // 2 + 1) * LANES]
            if hh % 2:
                src = pltpu.roll(src, HEAD_DIM, 1)
            blocks.append(jnp.where(lane < HEAD_DIM, src, 0.0))
        return jnp.concatenate(blocks, axis=1)

    qd = one_head_per_block(p[:, _C_QD0:_C_KD0]) * (HEAD_DIM ** -0.5 * LOG2E)
    kd = one_head_per_block(p[:, _C_KD0:_C_VD0])
    qd_ref[0] = (qd +jnp.dot(fp, selq_ref[...], preferred_element_type=F32) + oneq_ref[...]).astype(BF16)
    kd_ref[0] = (kd + jnp.dot(fp, selk_ref[...], preferred_element_type=F32) + onek_ref[...]).astype(BF16)

    _running_max_norm(kc, hsel_ref, kccarry_ref, kcnorm_ref)
    _running_max_norm(kd, hsel_ref, kcarry_ref, knorm_ref)
    for i in range(tm // TK):
        fend_ref[0, i] = jnp.broadcast_to(f2[(i + 1) * TK - 1:(i + 1) * TK, :], fend_ref.shape[2:])


def _cd_proj(x, g, w1, qn, wuq, kvn, wk, wv, bf, cos_t, sin_t, selq, selk, oneq, onek, hsel):
    b, s, d = x.shape
    tm = TM_PROJ
    const = lambda a: pl.BlockSpec(a.shape, lambda bi, ti: (0,) * a.ndim)
    tok = lambda nc: pl.BlockSpec((1, tm, nc), lambda bi, ti: (bi, ti, 0))
    vt_spec = pl.BlockSpec((1, N_PAIRS, LANES, tm), lambda bi, ti: (bi, 0, 0, ti))
    tab_spec = pl.BlockSpec((1, tm // TK, SUBLANES, LANES), lambda bi, ti: (bi, ti, 0, 0))
    act = jax.ShapeDtypeStruct((b, s, _NPAD), BF16)
    vt = jax.ShapeDtypeStruct((b, N_PAIRS, LANES, s), BF16)
    tab = jax.ShapeDtypeStruct((b, s // TK, SUBLANES, LANES), F32)
    return pl.pallas_call(
        _cd_proj_kernel,
        grid=(b, s // tm),
        in_specs=[
            tok(d), const(g), const(w1), const(qn), const(wuq), const(kvn), const(wk), const(wv), const(bf),
            pl.BlockSpec((tm, LANES), lambda bi, ti: (ti, 0)),
            pl.BlockSpec((tm, LANES), lambda bi, ti: (ti, 0)),
            const(selq), const(selk), const(oneq), const(onek), const(hsel),
        ],
        out_specs=[tok(_NPAD), tok(_NPAD), vt_spec, tok(_NPAD), tok(_NPAD), vt_spec, tab_spec, tab_spec, tab_spec],
        out_shape=[act, act, vt, act, act, vt, tab, tab, tab],
        scratch_shapes=[pltpu.VMEM((SUBLANES, LANES), F32)] * 3,
        compiler_params=pltpu.CompilerParams(
            dimension_semantics=("arbitrary", "arbitrary"), vmem_limit_bytes=VMEM_LIMIT),
        name="cd_norm_proj",
    )(x, g, w1, qn, wuq, kvn, wk, wv, bf, cos_t, sin_t, selq, selk, oneq, onek, hsel)


def _mask_pair_heads(q_ref, q2_ref):
    tq = q_ref.shape[1]
    lane = lax.broadcasted_iota(jnp.int32, (tq, LANES), 1)
    for h in range(q2_ref.shape[0]):
        q = q_ref[0, :, (h // 2) * LANES:(h // 2 + 1) * LANES]
        q2_ref[h] = jnp.where((lane >= HEAD_DIM) == bool(h % 2), q, jnp.zeros_like(q))


def _store_pair_heads(o_ref, outs):
    tq = outs[0].shape[1]
    lane = lax.broadcasted_iota(jnp.int32, (tq, LANES), 1)
    for pr in range(len(outs) // 2):
        o_ref[0, :, pr * LANES:(pr + 1) * LANES] = jnp.where(
            lane < HEAD_DIM, outs[2 * pr].T, outs[2 * pr + 1].T).astype(o_ref.dtype)


def _softmax_tile_update(ss, vts, m_ref, acc_ref):
    heads = range(len(ss))
    m_old = [m_ref[h] for h in heads]
    acc_old = [acc_ref[h] for h in heads]
    m_new = [functools.reduce(jnp.maximum, [jnp.max(s, axis=0, keepdims=True) for s in ss[h]], m_old[h])
             for h in heads]
    alpha = [jnp.exp2(m_old[h] - m_new[h]) for h in heads]
    ps = [[jnp.exp2(s - m_new[h]).astype(BF16) for s in ss[h]] for h in heads]
    ones = jnp.ones((SUM_ROWS, ss[0][0].shape[0]), BF16)
    pvs = [sum(jnp.dot(jnp.concatenate([vt, ones], axis=0), p, preferred_element_type=F32)
               for vt, p in zip(vts[h], ps[h])) for h in heads]
    for h in heads:
        m_ref[h] = m_new[h]
        acc_ref[h] = alpha[h] * acc_old[h] + pvs[h]


def _softmax_tile_update_fixed(ss, vts, m_ref, acc_ref):
    heads = range(len(ss))
    acc_old = [acc_ref[h] for h in heads]
    ps = [[jnp.exp2(s - m_ref[h]).astype(BF16) for s in ss[h]] for h in heads]
    ones = jnp.ones((SUM_ROWS, ss[0][0].shape[0]), BF16)
    pvs = [sum(jnp.dot(jnp.concatenate([vt, ones], axis=0), p, preferred_element_type=F32)
               for vt, p in zip(vts[h], ps[h])) for h in heads]
    for h in heads:
        acc_ref[h] = acc_old[h] + pvs[h]


def _normalised(acc_ref, h):
    return acc_ref[h, 0:LANES, :] / acc_ref[h, LANES:LANES + 1, :]


def _chunk_attn_kernel(q_ref, k_ref, vt_ref, bias_ref, knorm_ref, bmax_ref, o_ref, q2_ref, m_ref, acc_ref):
    bi = pl.program_id(0)
    qi = pl.program_id(2)
    tq = q_ref.shape[1]
    heads = range(acc_ref.shape[0])
    _mask_pair_heads(q_ref, q2_ref)
    ones = jnp.ones((SUBLANES, LANES), BF16)
    qnorm = []
    for h in heads:
        q32 = q2_ref[h].astype(F32)
        n2 = lax.dot_general(ones, (q32 * q32).astype(BF16), NT_DIMS, preferred_element_type=F32)[0:1]
        qnorm.append(jnp.sqrt(n2) * NORM_MARGIN)

    def run(fixed, js):
        update = _softmax_tile_update_fixed if fixed else _softmax_tile_update
        acc_ref[...] = jnp.zeros_like(acc_ref)
        for h in heads:
            m_ref[h] = ((qnorm[h] * knorm_ref[bi, qi, h] + bmax_ref[h]) if fixed
                        else jnp.full((1, tq), NEG_BIG, F32))
        kstarts = [pl.multiple_of((qi - (N_WIN_A - 1) + j) * tq, tq) for j in js]
        ss = [[lax.dot_general(k_ref[0, pl.ds(ks, tq), (h // 2) * LANES:(h // 2 + 1) * LANES], q2_ref[h],
                               NT_DIMS, preferred_element_type=F32) + bias_ref[h, j] for j, ks in zip(js, kstarts)]
              for h in heads]
        vts = [[vt_ref[0, h // 2, :, pl.ds(ks, tq)] for ks in kstarts] for h in heads]
        update(ss, vts, m_ref, acc_ref)
        _store_pair_heads(o_ref, [_normalised(acc_ref, h) for h in heads])

    def run_window(fixed):
        first = jnp.maximum(N_WIN_A - 1 - qi, 0)
        for f in range(N_WIN_A):
            pl.when(first == f)(functools.partial(run, fixed, list(range(f, N_WIN_A))))

    run_window(True)
    smallest = jnp.min(functools.reduce(jnp.minimum, [acc_ref[h, LANES:LANES + 1, :] for h in heads]))
    pl.when(jnp.logical_not(smallest >= 2.0 ** -SAFE_DENOM_LOG2))(lambda: run_window(False))


def _chunk_attn(proj, vt, bias, knorm, bmax):
    b, s, _ = proj.shape
    nh = HEADS_PER_STEP
    assert nh == N_HEADS
    width = nh * HEAD_DIM
    k0 = N_HEADS * HEAD_DIM // width
    resident = dict(pipeline_mode=pl.Buffered(1))
    return pl.pallas_call(
        _chunk_attn_kernel,
        grid=(b, N_HEADS // nh, s // TK),
        in_specs=[
            pl.BlockSpec((1, TK, width), lambda bi, hg, qi: (bi, qi, hg)),
            pl.BlockSpec((1, s, width), lambda bi, hg, qi: (bi, 0, k0 + hg), **resident),
            pl.BlockSpec((1, nh // 2, LANES, s), lambda bi, hg, qi: (bi, hg, 0, 0), **resident),
            pl.BlockSpec((nh, N_WIN_A, TK, TK), lambda bi, hg, qi: (hg, 0, 0, 0), **resident),
            pl.BlockSpec(memory_space=pltpu.SMEM),
            pl.BlockSpec(memory_space=pltpu.SMEM),
        ],
        out_specs=pl.BlockSpec((1, TK, width), lambda bi, hg, qi: (bi, qi, hg)),
        out_shape=jax.ShapeDtypeStruct((b, s, N_HEADS * HEAD_DIM), BF16),
        scratch_shapes=[pltpu.VMEM((nh, TK, LANES), BF16), pltpu.VMEM((nh, 1, TK), F32),
                        pltpu.VMEM((nh, LANES + SUM_ROWS, TK), F32)],
        compiler_params=pltpu.CompilerParams(
            dimension_semantics=("arbitrary", "arbitrary", "arbitrary"), vmem_limit_bytes=VMEM_LIMIT),
        name="chunk_attn",
    )(proj, proj, vt, bias, knorm, bmax)


def _chunk_bias_tiles(rel_bias):
    h = rel_bias.shape[0]
    nq = TK
    nk = N_WIN_A * TK
    period = 1 << (nq + nk - 1).bit_length()
    u = np.arange(period)
    signed = np.where(u < nk, u, u - period)
    idx = np.clip(LEFT_CHUNKS * CHUNK - signed, -MAX_REL, MAX_REL) + MAX_REL
    v = rel_bias.astype(F32)[:, idx] * LOG2E
    toep = jnp.tile(v, (1, nq))[:, :nq * (period - 1)].reshape(h, nq, period - 1)[:, :, :nk]
    r = np.arange(nq)[:, None]
    off = np.arange(nk)[None, :] - CHUNK * (r // CHUNK)
    in_band = (off >= 0) & (off < BAND)
    bias = jnp.where(in_band[None], toep, NEG_BIG)
    return bias.reshape(h, nq, N_WIN_A, TK).transpose(0, 2, 3, 1)


def _stick_kernel(q_ref, k_ref, vt_ref, tri_ref, o_ref, q2_ref, c_ref, acc_ref):
    qi = pl.program_id(2)
    tq = q_ref.shape[1]
    tk = tri_ref.shape[0]
    heads = range(acc_ref.shape[0])
    pair = lambda h: slice((h // 2) * LANES, (h // 2 + 1) * LANES)
    _mask_pair_heads(q_ref, q2_ref)
    c_ref[...] = jnp.zeros_like(c_ref)
    acc_ref[...] = jnp.zeros_like(acc_ref)
    sign_bit = jnp.uint32(0x80000000)

    def tiles(js, masked):
        nt = range(len(js))
        kstarts = [pl.multiple_of(j * tk, tk) for j in js]
        zs = [[lax.dot_general(k_ref[0, pl.ds(kstarts[t], tk), pair(h)], q2_ref[h], NT_DIMS,
                               preferred_element_type=F32) for t in nt] for h in heads]
        c_old = [c_ref[h] for h in heads]
        acc_old = [acc_ref[h] for h in heads]

        def neg_log_keep(z, t):
            neg_abs = lax.bitcast_convert_type(lax.bitcast_convert_type(z, jnp.uint32) | sign_bit, F32)
            nlk = jnp.maximum(z, 0.0) + jnp.log(1.0 + jnp.exp2(neg_abs)) * LOG2E
            return jnp.where(masks[t], nlk, 0.0) if masked[t] else nlk

        masks = [None] * len(js)
        for t in nt:
            if masked[t]:
                key = kstarts[t] + lax.broadcasted_iota(jnp.int32, (tk, tq), 0)
                qry = qi * tq + lax.broadcasted_iota(jnp.int32, (tk, tq), 1)
                masks[t] = key < qry
        nlk = [[neg_log_keep(zs[h][t], t) for t in nt] for h in heads]
        rs = [[jnp.dot(tri_ref[...], nlk[h][t].astype(BF16), preferred_element_type=F32) for t in nt] for h in heads]
        pvs = []
        c_new = []
        for h in heads:
            c = c_old[h]
            pv = None
            for t in nt:
                w = jnp.exp2((zs[h][t] - nlk[h][t]) - rs[h][t] - c)
                if masked[t]:
                    w = jnp.where(masks[t], w, 0.0)
                d = jnp.dot(vt_ref[0, h // 2, :, pl.ds(kstarts[t], tk)], w.astype(BF16),
                            preferred_element_type=F32)
                pv = d if pv is None else pv + d
                c = c + jnp.sum(nlk[h][t], axis=0, keepdims=True)
            pvs.append(pv)
            c_new.append(c)
        for h in heads:
            c_ref[h] = c_new[h]
            acc_ref[h] = acc_old[h] + pvs[h]

    assert tq == tk
    pl.when(qi > 0)(lambda: tiles([qi, qi - 1], [True, False]))
    pl.when(qi == 0)(lambda: tiles([qi], [True]))

    def more(j):
        return jnp.logical_and(j >= 0, jnp.min(c_ref[...]) < STICK_UNDERFLOW_LOG2)

    def body(j):
        tiles([j], [False])
        return j - 1

    lax.while_loop(more, body, qi - 2)
    _store_pair_heads(o_ref, [acc_ref[h] for h in heads])


def _stick_attn(proj, vt, tri):
    b, s, _ = proj.shape
    nh = HEADS_PER_STEP
    width = nh * HEAD_DIM
    q0 = 3 * N_HEADS * HEAD_DIM // width
    k0 = 4 * N_HEADS * HEAD_DIM // width
    resident = dict(pipeline_mode=pl.Buffered(1))
    return pl.pallas_call(
        _stick_kernel,
        grid=(b, N_HEADS // nh, s // TQ),
        in_specs=[
            pl.BlockSpec((1, TQ, width), lambda bi, hg, qi: (bi, qi, q0 + hg)),
            pl.BlockSpec((1, s, width), lambda bi, hg, qi: (bi, 0, k0 + hg), **resident),
            pl.BlockSpec((1, nh // 2, LANES, s), lambda bi, hg, qi: (bi, hg, 0, 0), **resident),
            pl.BlockSpec((TK, TK), lambda bi, hg, qi: (0, 0)),
        ],
        out_specs=pl.BlockSpec((1, TQ, width), lambda bi, hg, qi: (bi, qi, hg)),
        out_shape=jax.ShapeDtypeStruct((b, s, N_HEADS * HEAD_DIM), BF16),
        scratch_shapes=[pltpu.VMEM((nh, TQ, LANES), BF16), pltpu.VMEM((nh, 1, TQ), F32),
                        pltpu.VMEM((nh, LANES, TQ), F32)],
        compiler_params=pltpu.CompilerParams(
            dimension_semantics=("arbitrary", "arbitrary", "arbitrary"), vmem_limit_bytes=VMEM_LIMIT),
        name="stick_attn",
    )(proj, proj, vt, tri)


def _softmax_attn_kernel(*refs, chunk_mask, decay_skip):
    if decay_skip:
        q_ref, k_ref, vt_ref, knorm_ref, fend_ref, o_ref, m_ref, acc_ref = refs
    else:
        q_ref, k_ref, vt_ref, knorm_ref, o_ref, m_ref, acc_ref = refs
    bi = pl.program_id(0)
    qi = pl.program_id(2)
    tq = q_ref.shape[1]
    tk = TK
    heads = range(q_ref.shape[2] // LANES)
    sl = lambda h: slice(h * LANES, (h + 1) * LANES)
    assert tq == tk
    g = TILES_PER_STEP
    rem = qi % g

    lane = lax.broadcasted_iota(jnp.int32, (SUBLANES, LANES), 1)
    dims = jnp.where(lane < (HEAD_DIM if decay_skip else LANES), 1.0, 0.0).astype(BF16)
    pieces = jnp.where((lane >= F_LANE0) & (lane < F_LANE0 + N_PIECES), 1.0, 0.0).astype(BF16)
    qnorm, fq = [], []
    for h in heads:
        q = q_ref[0, :, sl(h)]
        q32 = q.astype(F32)
        n2 = lax.dot_general(dims, (q32 * q32).astype(BF16), NT_DIMS, preferred_element_type=F32)[0:1]
        qnorm.append(jnp.sqrt(n2) * NORM_MARGIN)
        if decay_skip:
            fq.append(lax.dot_general(pieces, q, NT_DIMS, preferred_element_type=F32)[0:1])

    def run(fixed):
        update = _softmax_tile_update_fixed if fixed else _softmax_tile_update
        acc_ref[...] = jnp.zeros_like(acc_ref)
        for h in heads:
            m_ref[h] = (qnorm[h] * knorm_ref[bi, qi, h] + 1.0) if fixed else jnp.full((1, tq), NEG_BIG, F32)

        def tiles(js, masked):
            kstarts = [pl.multiple_of(j * tk, tk) for j in js]

            def scores(h, t):
                s = lax.dot_general(k_ref[0, pl.ds(kstarts[t], tk), sl(h)], q_ref[0, :, sl(h)], NT_DIMS,
                                    preferred_element_type=F32)
                if masked[t]:
                    key = kstarts[t] + lax.broadcasted_iota(jnp.int32, (tk, tq), 0)
                    qry = qi * tq + lax.broadcasted_iota(jnp.int32, (tk, tq), 1)
                    s = jnp.where((key // CHUNK <= qry // CHUNK) if chunk_mask else (key <= qry), s, NEG_BIG)
                return s

            ss = [[scores(h, t) for t in range(len(js))] for h in heads]
            vts = [[vt_ref[0, h // 2, :, pl.ds(ks, tk)] for ks in kstarts] for h in heads]
            update(ss, vts, m_ref, acc_ref)

        def diagonal_step():
            for r in range(g):
                pl.when(rem == r)(functools.partial(tiles, [qi - r + t for t in range(r + 1)], [False] * r + [True]))

        if not decay_skip:
            def body(i, carry):
                tiles([g * i + t for t in range(g)], [False] * g)
                return carry

            lax.fori_loop(0, qi // g, body, 0)
            diagonal_step()
        else:
            diagonal_step()
            top = qi - 1 - rem

            def more(it):
                j = jnp.maximum(top - g * it, 0)
                if fixed:
                    level = [m_ref[h] + jnp.log2(acc_ref[h, LANES:LANES + 1, :]) for h in heads]
                else:
                    level = [m_ref[h] for h in heads]
                gap = [qnorm[h] * knorm_ref[bi, j, h] + fq[h] - fend_ref[bi, j, h] - level[h] for h in heads]
                reach = jnp.max(functools.reduce(jnp.maximum, gap))
                return jnp.logical_and(it < qi // g, reach > -SOFTMAX_UNDERFLOW_LOG2)

            def body(it):
                tiles([top - g * it - t for t in range(g)], [False] * g)
                return it + 1

            lax.while_loop(more, body, 0)
        _store_pair_heads(o_ref, [_normalised(acc_ref, h) for h in heads])

    run(True)
    smallest = jnp.min(functools.reduce(jnp.minimum, [acc_ref[h, LANES:LANES + 1, :] for h in heads]))
    pl.when(jnp.logical_not(smallest >= 2.0 ** -SAFE_DENOM_LOG2))(lambda: run(False))


def _softmax_attn(q, k, vt, knorm, chunk_mask, name, fend=None):
    b, s, _ = q.shape
    nh = HEADS_PER_STEP
    assert nh == N_HEADS
    resident = dict(pipeline_mode=pl.Buffered(1))
    tables = [knorm] + ([] if fend is None else [fend])
    return pl.pallas_call(
        functools.partial(_softmax_attn_kernel, chunk_mask=chunk_mask, decay_skip=fend is not None),
        grid=(b, N_HEADS // nh, s // TQ),
        in_specs=[
            pl.BlockSpec((1, TQ, nh * LANES), lambda bi, hg, qi: (bi, qi, hg)),
            pl.BlockSpec((1, s, nh * LANES), lambda bi, hg, qi: (bi, 0, hg), **resident),
            pl.BlockSpec((1, nh // 2, LANES, s), lambda bi, hg, qi: (bi, hg, 0, 0), **resident),
        ] + [pl.BlockSpec(memory_space=pltpu.SMEM)] * len(tables),
        out_specs=pl.BlockSpec((1, TQ, nh * HEAD_DIM), lambda bi, hg, qi: (bi, qi, hg)),
        out_shape=jax.ShapeDtypeStruct((b, s, N_HEADS * HEAD_DIM), BF16),
        scratch_shapes=[pltpu.VMEM((nh, 1, TQ), F32), pltpu.VMEM((nh, LANES + SUM_ROWS, TQ), F32)],
        compiler_params=pltpu.CompilerParams(
            dimension_semantics=("arbitrary", "arbitrary", "arbitrary"), vmem_limit_bytes=VMEM_LIMIT),
        name=name,
    )(q, k, vt, *tables)


def _out_ffn_kernel(*refs, tiles_per_seq, final_norm):
    if final_norm:
        (x_ref, o1_ref, o2_ref, wo_ref, g_ref, wg_ref, wu_ref, cw_ref, cb_ref, wd_ref, fg_ref,
         out_ref, x1_ref, h_ref, acc_ref, gbuf_ref, tail_ref) = refs
    else:
        (x_ref, o1_ref, o2_ref, wo_ref, g_ref, wg_ref, wu_ref, cw_ref, cb_ref, wd_ref,
         out_ref, x1_ref, h_ref, acc_ref, gbuf_ref, tail_ref) = refs
    i = pl.program_id(0)
    f = pl.program_id(1)
    nf = pl.num_programs(1)
    tm = x_ref.shape[0]
    half = o1_ref.shape[1]

    @pl.when(f == 0)
    def _():
        x1 = (x_ref[...]
              + jnp.dot(o1_ref[...], wo_ref[0:half, :], preferred_element_type=F32)
              + jnp.dot(o2_ref[...], wo_ref[half:2 * half, :], preferred_element_type=F32))
        x1_ref[...] = x1
        h_ref[...] = _rms(x1, g_ref[...]).astype(BF16)
        acc_ref[...] = jnp.zeros_like(acc_ref)

    h = h_ref[...]
    g = jnp.dot(h, wg_ref[...], preferred_element_type=F32)
    u = jnp.dot(h, wu_ref[...], preferred_element_type=F32)

    prev = jnp.where(i % tiles_per_seq == 0, 0.0, tail_ref[f])
    halo = SUBLANES
    gbuf_ref[0:halo, :] = prev
    gbuf_ref[halo:halo + tm, :] = g
    tail_ref[f] = g[tm - halo:tm, :]
    gm1 = gbuf_ref[halo - 1:halo - 1 + tm, :]
    gm2 = gbuf_ref[halo - 2:halo - 2 + tm, :]
    cw = cw_ref[...]
    gc = cw[0:1, :] * gm2 + cw[1:2, :] * gm1 + cw[2:3, :] * g + cb_ref[...]
    y = (gc / (1.0 + jnp.exp(-gc)) * u).astype(BF16)
    acc_ref[...] += jnp.dot(y, wd_ref[...], preferred_element_type=F32)

    @pl.when(f == nf - 1)
    def _():
        res = x1_ref[...] + acc_ref[...]
        if final_norm:
            res = _rms(res, fg_ref[...])
        out_ref[...] = res


def _out_ffn(x2d, o1, o2, wo, g, wg, wu, cw, cb, wd, final_g, seq_len):
    n, d = x2d.shape
    half = o1.shape[1]
    dff = wg.shape[1]
    tm, tf = TM_FFN, TF_FFN
    nf = dff // tf
    final_norm = final_g is not None
    once = dict(pipeline_mode=pl.Buffered(1))
    per_f = once if nf == 1 else {}
    in_specs = [
        pl.BlockSpec((tm, d), lambda i, f: (i, 0)),
        pl.BlockSpec((tm, half), lambda i, f: (i, 0)),
        pl.BlockSpec((tm, half), lambda i, f: (i, 0)),
        pl.BlockSpec((d, d), lambda i, f: (0, 0), **once),
        pl.BlockSpec((1, d), lambda i, f: (0, 0)),
        pl.BlockSpec((d, tf), lambda i, f: (0, f), **per_f),
        pl.BlockSpec((d, tf), lambda i, f: (0, f), **per_f),
        pl.BlockSpec((3, tf), lambda i, f: (0, f)),
        pl.BlockSpec((1, tf), lambda i, f: (0, f)),
        pl.BlockSpec((tf, d), lambda i, f: (f, 0), **per_f),
    ]
    args = [x2d, o1, o2, wo, g, wg, wu, cw, cb, wd]
    if final_norm:
        in_specs.append(pl.BlockSpec((1, d), lambda i, f: (0, 0)))
        args.append(final_g)
    return pl.pallas_call(
        functools.partial(_out_ffn_kernel, tiles_per_seq=seq_len // tm, final_norm=final_norm),
        grid=(n // tm, nf),
        in_specs=in_specs,
        out_specs=pl.BlockSpec((tm, d), lambda i, f: (i, 0)),
        out_shape=jax.ShapeDtypeStruct((n, d), F32),
        scratch_shapes=[
            pltpu.VMEM((tm, d), F32),
            pltpu.VMEM((tm, d), BF16),
            pltpu.VMEM((tm, d), F32),
            pltpu.VMEM((tm + SUBLANES, tf), F32),
            pltpu.VMEM((nf, SUBLANES, tf), F32),
        ],
        compiler_params=pltpu.CompilerParams(
            dimension_semantics=("arbitrary", "arbitrary"), vmem_limit_bytes=VMEM_LIMIT),
        name="out_ffn_final" if final_norm else "out_ffn",
    )(*args)


def _rope_tables(seq_len):
    half = QK_ROPE // 2
    inv = ROPE_THETA ** (-jnp.arange(half, dtype=F32) / half)
    ang = jnp.arange(seq_len, dtype=F32)[:, None] * inv[None, :]
    cos, sin = jnp.cos(ang), jnp.sin(ang)
    ones = jnp.ones((seq_len, QK_NOPE), F32)
    zeros = jnp.zeros((seq_len, QK_NOPE), F32)
    pad1 = jnp.ones((seq_len, LANES - QK_NOPE - QK_ROPE), F32)
    pad0 = jnp.zeros((seq_len, LANES - QK_NOPE - QK_ROPE), F32)
    return (jnp.concatenate([ones, cos, cos, pad1], axis=1),
            jnp.concatenate([zeros, -sin, sin, pad0], axis=1))


def _pad_heads(w, width):
    rows = w.shape[0]
    return jnp.zeros((rows, N_HEADS, LANES), w.dtype).at[:, :, :width].set(
        w.reshape(rows, N_HEADS, width)).reshape(rows, N_HEADS * LANES)


def _cd_weights(w_in, w_uq, w_ukv, b_f):
    d = w_in.shape[0]
    nd = N_HEADS * HEAD_DIM
    o = Q_RANK + KV_RANK + QK_ROPE
    c_q, c_kv, k_rope = w_in[:, :Q_RANK], w_in[:, Q_RANK:Q_RANK + KV_RANK], w_in[:, Q_RANK + KV_RANK:o]
    q_d, k_d, v_d, f_logit = (w_in[:, o:o + nd], w_in[:, o + nd:o + 2 * nd], w_in[:, o + 2 * nd:o + 3 * nd],
                              w_in[:, o + 3 * nd:])
    kr_blk = jnp.zeros((d, LANES), w_in.dtype).at[:, QK_NOPE:QK_NOPE + QK_ROPE].set(k_rope)
    f_blk = jnp.zeros((d, LANES), w_in.dtype).at[:, :N_HEADS].set(f_logit)
    w1 = jnp.concatenate([c_q, c_kv, kr_blk, q_d, k_d, v_d, f_blk], axis=1).astype(BF16)
    wuq = _pad_heads(w_uq, QK_NOPE + QK_ROPE).astype(BF16)
    ukv = w_ukv.reshape(KV_RANK, N_HEADS, QK_NOPE + HEAD_DIM)
    wk = _pad_heads(ukv[:, :, :QK_NOPE].reshape(KV_RANK, N_HEADS * QK_NOPE), QK_NOPE).astype(BF16)
    wv = ukv[:, :, QK_NOPE:].reshape(KV_RANK, N_HEADS * HEAD_DIM).astype(BF16)
    bf = jnp.zeros((1, LANES), F32).at[0, :N_HEADS].set(b_f.astype(F32))
    return w1, wuq, wk, wv, bf


def _forget_selectors():
    selq = np.zeros((LANES, _NPAD), np.float32)
    selk = np.zeros((LANES, _NPAD), np.float32)
    oneq = np.zeros((1, _NPAD), np.float32)
    onek = np.zeros((1, _NPAD), np.float32)
    hsel = np.zeros((_NPAD, LANES), np.float32)
    for h in range(N_HEADS):
        hsel[h * LANES:(h + 1) * LANES, h] = 1.0
        for j in range(N_PIECES):
            selq[j * N_HEADS + h, h * LANES + F_LANE0 + j] = 1.0
            onek[0, h * LANES + F_LANE0 + j] = 1.0
            selk[j * N_HEADS + h, h * LANES + ONE_LANE0 + j] = -1.0
            oneq[0, h * LANES + ONE_LANE0 + j] = 1.0
    return (jnp.asarray(selq, BF16), jnp.asarray(selk, BF16), jnp.asarray(oneq), jnp.asarray(onek),
            jnp.asarray(hsel, BF16))


def kernel(x, ab_norm, ab_w_in, ab_rel_bias, ab_w_o, cd_norm, cd_w_in, cd_q_norm, cd_w_uq, cd_kv_norm, cd_w_ukv,
           cd_b_f, cd_w_o, ffn_norm, ffn_w_gate, ffn_w_up, ffn_conv_w, ffn_conv_b, ffn_w_down, final_norm):
    b, s, d = x.shape
    n = b * s

    qscale = jnp.full((N_HEADS * HEAD_DIM,), HEAD_DIM ** -0.5 * LOG2E, F32)
    one = jnp.ones((2 * N_HEADS * HEAD_DIM,), F32)
    colscale = jnp.concatenate([qscale, one, qscale, one])[None, :]
    per_head = lambda tab: tab[:, :, 0, :N_HEADS]
    hsel_a = jnp.asarray(np.repeat(np.eye(N_HEADS, LANES, dtype=np.float32), HEAD_DIM, axis=0), BF16)
    proj, vat, vbt, kanorm = _ab_proj(x, ab_norm[0][None, :], ab_w_in[0].astype(BF16), colscale, hsel_a)
    bias_max = jnp.max(ab_rel_bias[0].astype(F32), axis=1) * LOG2E + 1.0
    oa = _chunk_attn(proj, vat, _chunk_bias_tiles(ab_rel_bias[0]), per_head(kanorm), bias_max)
    tri = jnp.asarray(np.triu(np.ones((TK, TK), np.float32), k=1), BF16)
    ob = _stick_attn(proj, vbt, tri)
    x2d = _out_ffn(x.reshape(n, d), oa.reshape(n, -1), ob.reshape(n, -1), ab_w_o[0].astype(BF16),
                   ffn_norm[0][None, :], ffn_w_gate[0].astype(BF16), ffn_w_up[0].astype(BF16), ffn_conv_w[0],
                   ffn_conv_b[0][None, :], ffn_w_down[0].astype(BF16), None, s)

    w1, wuq, wk, wv, bf = _cd_weights(cd_w_in[0], cd_w_uq[0], cd_w_ukv[0], cd_b_f[0])
    cos_t, sin_t = _rope_tables(s)
    qc, kc, vct, qd, kd, vdt, kcnorm, kdnorm, fend = _cd_proj(
        x2d.reshape(b, s, d), cd_norm[0][None, :], w1, cd_q_norm[0][None, :], wuq, cd_kv_norm[0][None, :], wk, wv, bf,
        cos_t, sin_t, *_forget_selectors())
    oc = _softmax_attn(qc, kc, vct, per_head(kcnorm), True, "mla_attn")
    od = _softmax_attn(qd, kd, vdt, per_head(kdnorm), False, "fox_attn", per_head(fend))
    out = _out_ffn(x2d, oc.reshape(n, -1), od.reshape(n, -1), cd_w_o[0].astype(BF16), ffn_norm[1][None, :],
                   ffn_w_gate[1].astype(BF16), ffn_w_up[1].astype(BF16), ffn_conv_w[1], ffn_conv_b[1][None, :],
                   ffn_w_down[1].astype(BF16), final_norm[None, :], s)
    return out.reshape(b, s, d)
```

```python
import functools
import math

import numpy as np
import jax
import jax.numpy as jnp
from jax import lax
from jax.experimental import pallas as pl
from jax.experimental.pallas import tpu as pltpu

F32 = jnp.float32
BF16 = jnp.bfloat16

D_MODEL = 1024
HEAD_DIM = 64
CHUNK = 64
LEFT_CHUNKS = 8
BAND = (LEFT_CHUNKS + 1) * CHUNK
MAX_REL = 128
N_HEADS = 8
N_PAIRS = N_HEADS // 2
QK_NOPE = 64
QK_ROPE = 32
Q_RANK = 384
KV_RANK = 256
ROPE_THETA = 10000.0
D_FF = 2816
RMS_EPS = 1e-6

LANES = 128
SUBLANES = 8
LOG2E = math.log2(math.e)
NEG_BIG = -1e30
STICK_UNDERFLOW_LOG2 = 200.0
SOFTMAX_UNDERFLOW_LOG2 = 160.0
NORM_MARGIN = 1.01
SAFE_DENOM_LOG2 = 60.0
VMEM_LIMIT = 52 * 1024 * 1024

TM_PROJ = 512
TM_FFN = 256
TF_FFN = 2816
TK = 256
TQ = 256
N_WIN_A = LEFT_CHUNKS * CHUNK // TK + 1
HEADS_PER_STEP = 8
TILES_PER_STEP = 4
SUM_ROWS = 16

N_PIECES = 3
F_LANE0 = HEAD_DIM
ONE_LANE0 = HEAD_DIM + N_PIECES

NT_DIMS = (((1,), (1,)), ((), ()))


def _rms(x, g):
    ms = jnp.mean(x * x, axis=-1, keepdims=True)
    return x * lax.rsqrt(ms + RMS_EPS) * g


def _store_pairs_transposed(vt_ref, v):
    for p in range(N_PAIRS):
        vt_ref[0, p] = v[:, p * LANES:(p + 1) * LANES].T.astype(vt_ref.dtype)


def _running_max_norm(k, hsel_ref, carry_ref, out_ref):
    n2 = jnp.dot((k * k).astype(BF16), hsel_ref[...], preferred_element_type=F32)
    for i in range(k.shape[0] // TK):
        tile_max = jnp.sqrt(jnp.max(n2[i * TK:(i + 1) * TK], axis=0, keepdims=True)) * NORM_MARGIN
        kmax = jnp.maximum(carry_ref[0:1, :], tile_max)
        carry_ref[...] = jnp.broadcast_to(kmax, carry_ref.shape)
        out_ref[0, i] = jnp.broadcast_to(kmax, out_ref.shape[2:])


def _ab_proj_kernel(x_ref, g_ref, w_ref, cs_ref, hsel_ref, o_ref, vat_ref, vbt_ref, kanorm_ref, kcarry_ref):
    h = _rms(x_ref[0], g_ref[...]).astype(BF16)
    p = jnp.dot(h, w_ref[...], preferred_element_type=F32) * cs_ref[...]
    o_ref[0] = p.astype(o_ref.dtype)
    nv = N_HEADS * HEAD_DIM
    _store_pairs_transposed(vat_ref, p[:, 2 * nv:3 * nv])
    _store_pairs_transposed(vbt_ref, p[:, 5 * nv:6 * nv])

    @pl.when(pl.program_id(1) == 0)
    def _():
        kcarry_ref[...] = jnp.zeros_like(kcarry_ref)

    _running_max_norm(p[:, nv:2 * nv], hsel_ref, kcarry_ref, kanorm_ref)


def _ab_proj(x, g, w, colscale, hsel):
    b, s, d = x.shape
    nc = w.shape[1]
    tm = TM_PROJ
    vt_spec = pl.BlockSpec((1, N_PAIRS, LANES, tm), lambda bi, ti: (bi, 0, 0, ti))
    vt_shape = jax.ShapeDtypeStruct((b, N_PAIRS, LANES, s), BF16)
    return pl.pallas_call(
        _ab_proj_kernel,
        grid=(b, s // tm),
        in_specs=[
            pl.BlockSpec((1, tm, d), lambda bi, ti: (bi, ti, 0)),
            pl.BlockSpec((1, d), lambda bi, ti: (0, 0)),
            pl.BlockSpec((d, nc), lambda bi, ti: (0, 0)),
            pl.BlockSpec((1, nc), lambda bi, ti: (0, 0)),
            pl.BlockSpec(hsel.shape, lambda bi, ti: (0, 0)),
        ],
        out_specs=[pl.BlockSpec((1, tm, nc), lambda bi, ti: (bi, ti, 0)), vt_spec, vt_spec,
                   pl.BlockSpec((1, tm // TK, SUBLANES, LANES), lambda bi, ti: (bi, ti, 0, 0))],
        out_shape=[jax.ShapeDtypeStruct((b, s, nc), BF16), vt_shape, vt_shape,
                   jax.ShapeDtypeStruct((b, s // TK, SUBLANES, LANES), F32)],
        scratch_shapes=[pltpu.VMEM((SUBLANES, LANES), F32)],
        compiler_params=pltpu.CompilerParams(
            dimension_semantics=("arbitrary", "arbitrary"), vmem_limit_bytes=VMEM_LIMIT),
        name="ab_norm_proj",
    )(x, g, w, colscale, hsel)


_NPAD = N_HEADS * LANES
_C_Q0 = 0
_C_KV0 = _C_Q0 + Q_RANK
_C_KR0 = _C_KV0 + KV_RANK
_C_QD0 = _C_KR0 + LANES
_C_KD0 = _C_QD0 + N_HEADS * HEAD_DIM
_C_VD0 = _C_KD0 + N_HEADS * HEAD_DIM
_C_F0 = _C_VD0 + N_HEADS * HEAD_DIM
_C_END = _C_F0 + LANES


def _cd_proj_kernel(x_ref, g_ref, w1_ref, qn_ref, wuq_ref, kvn_ref, wk_ref, wv_ref, bf_ref, cos_ref, sin_ref,
                    selq_ref, selk_ref, oneq_ref, onek_ref, hsel_ref,
                    qc_ref, kc_ref, vct_ref, qd_ref, kd_ref, vdt_ref, kcnorm_ref, knorm_ref, fend_ref,
                    carry_ref, kcarry_ref, kccarry_ref):
    t = pl.program_id(1)
    tm = x_ref.shape[1]
    h = _rms(x_ref[0], g_ref[...]).astype(BF16)
    p = jnp.dot(h, w1_ref[...], preferred_element_type=F32)

    cq = _rms(p[:, _C_Q0:_C_KV0], qn_ref[...]).astype(BF16)
    ckv = _rms(p[:, _C_KV0:_C_KR0], kvn_ref[...]).astype(BF16)

    cosb = cos_ref[...]
    sinb = sin_ref[...]
    lane = lax.broadcasted_iota(jnp.int32, (tm, LANES), 1)

    def rope(xb):
        partner = jnp.where(lane < QK_NOPE + QK_ROPE // 2, pltpu.roll(xb, LANES - QK_ROPE // 2, 1),
                            pltpu.roll(xb, QK_ROPE // 2, 1))
        return xb * cosb + partner * sinb

    qc = jnp.dot(cq, wuq_ref[...], preferred_element_type=F32) * ((QK_NOPE + QK_ROPE) ** -0.5 * LOG2E)
    kc = jnp.dot(ckv, wk_ref[...], preferred_element_type=F32)
    kr = rope(p[:, _C_KR0:_C_QD0])
    kc_heads = []
    for hh in range(N_HEADS):
        sl = slice(hh * LANES, (hh + 1) * LANES)
        qc_ref[0, :, sl] = rope(qc[:, sl]).astype(BF16)
        kc_heads.append(kc[:, sl] + kr)
        kc_ref[0, :, sl] = kc_heads[hh].astype(BF16)
    kc = jnp.concatenate(kc_heads, axis=1)
    _store_pairs_transposed(vct_ref, jnp.dot(ckv, wv_ref[...], preferred_element_type=F32))
    _store_pairs_transposed(vdt_ref, p[:, _C_VD0:_C_F0])

    fl = p[:, _C_F0:_C_END] + bf_ref[...]
    y = jnp.minimum(fl, 0.0) - jnp.log(1.0 + jnp.exp(-jnp.abs(fl)))
    row = lax.broadcasted_iota(jnp.int32, (tm, LANES), 0)
    sh = 1
    while sh < tm:
        y = y + jnp.where(row >= sh, pltpu.roll(y, sh, 0), 0.0)
        sh *= 2

    @pl.when(t == 0)
    def _():
        carry_ref[...] = jnp.zeros_like(carry_ref)
        kcarry_ref[...] = jnp.zeros_like(kcarry_ref)
        kccarry_ref[...] = jnp.zeros_like(kccarry_ref)

    y = y + carry_ref[0:1, :]
    carry_ref[...] = jnp.broadcast_to(y[tm - 1:tm, :], carry_ref.shape)
    f2 = y * LOG2E

    hi = f2.astype(BF16).astype(F32)
    r1 = f2 - hi
    mid = r1.astype(BF16).astype(F32)
    lo = r1 - mid
    fp = jnp.where(lane < N_HEADS, hi, jnp.where(lane < 2 * N_HEADS, pltpu.roll(mid, N_HEADS, 1),
                                                 pltpu.roll(lo, 2 * N_HEADS, 1))).astype(BF16)
    def one_head_per_block(x2):
        blocks = []
        for hh in range(N_HEADS):
            src = x2[:, (hh // 2) * LANES:(hh // 2 + 1) * LANES]
            if hh % 2:
                src = pltpu.roll(src, HEAD_DIM, 1)
            blocks.append(jnp.where(lane < HEAD_DIM, src, 0.0))
        return jnp.concatenate(blocks, axis=1)

    qd = one_head_per_block(p[:, _C_QD0:_C_KD0]) * (HEAD_DIM ** -0.5 * LOG2E)
    kd = one_head_per_block(p[:, _C_KD0:_C_VD0])
    qd_ref[0] = (qd +jnp.dot(fp, selq_ref[...], preferred_element_type=F32) + oneq_ref[...]).astype(BF16)
    kd_ref[0] = (kd + jnp.dot(fp, selk_ref[...], preferred_element_type=F32) + onek_ref[...]).astype(BF16)

    _running_max_norm(kc, hsel_ref, kccarry_ref, kcnorm_ref)
    _running_max_norm(kd, hsel_ref, kcarry_ref, knorm_ref)
    for i in range(tm // TK):
        fend_ref[0, i] = jnp.broadcast_to(f2[(i + 1) * TK - 1:(i + 1) * TK, :], fend_ref.shape[2:])


def _cd_proj(x, g, w1, qn, wuq, kvn, wk, wv, bf, cos_t, sin_t, selq, selk, oneq, onek, hsel):
    b, s, d = x.shape
    tm = TM_PROJ
    const = lambda a: pl.BlockSpec(a.shape, lambda bi, ti: (0,) * a.ndim)
    tok = lambda nc: pl.BlockSpec((1, tm, nc), lambda bi, ti: (bi, ti, 0))
    vt_spec = pl.BlockSpec((1, N_PAIRS, LANES, tm), lambda bi, ti: (bi, 0, 0, ti))
    tab_spec = pl.BlockSpec((1, tm // TK, SUBLANES, LANES), lambda bi, ti: (bi, ti, 0, 0))
    act = jax.ShapeDtypeStruct((b, s, _NPAD), BF16)
    vt = jax.ShapeDtypeStruct((b, N_PAIRS, LANES, s), BF16)
    tab = jax.ShapeDtypeStruct((b, s // TK, SUBLANES, LANES), F32)
    return pl.pallas_call(
        _cd_proj_kernel,
        grid=(b, s // tm),
        in_specs=[
            tok(d), const(g), const(w1), const(qn), const(wuq), const(kvn), const(wk), const(wv), const(bf),
            pl.BlockSpec((tm, LANES), lambda bi, ti: (ti, 0)),
            pl.BlockSpec((tm, LANES), lambda bi, ti: (ti, 0)),
            const(selq), const(selk), const(oneq), const(onek), const(hsel),
        ],
        out_specs=[tok(_NPAD), tok(_NPAD), vt_spec, tok(_NPAD), tok(_NPAD), vt_spec, tab_spec, tab_spec, tab_spec],
        out_shape=[act, act, vt, act, act, vt, tab, tab, tab],
        scratch_shapes=[pltpu.VMEM((SUBLANES, LANES), F32)] * 3,
        compiler_params=pltpu.CompilerParams(
            dimension_semantics=("arbitrary", "arbitrary"), vmem_limit_bytes=VMEM_LIMIT),
        name="cd_norm_proj",
    )(x, g, w1, qn, wuq, kvn, wk, wv, bf, cos_t, sin_t, selq, selk, oneq, onek, hsel)


def _mask_pair_heads(q_ref, q2_ref):
    tq = q_ref.shape[1]
    lane = lax.broadcasted_iota(jnp.int32, (tq, LANES), 1)
    for h in range(q2_ref.shape[0]):
        q = q_ref[0, :, (h // 2) * LANES:(h // 2 + 1) * LANES]
        q2_ref[h] = jnp.where((lane >= HEAD_DIM) == bool(h % 2), q, jnp.zeros_like(q))


def _store_pair_heads(o_ref, outs):
    tq = outs[0].shape[1]
    lane = lax.broadcasted_iota(jnp.int32, (tq, LANES), 1)
    for pr in range(len(outs) // 2):
        o_ref[0, :, pr * LANES:(pr + 1) * LANES] = jnp.where(
            lane < HEAD_DIM, outs[2 * pr].T, outs[2 * pr + 1].T).astype(o_ref.dtype)


def _softmax_tile_update(ss, vts, m_ref, acc_ref):
    heads = range(len(ss))
    m_old = [m_ref[h] for h in heads]
    acc_old = [acc_ref[h] for h in heads]
    m_new = [functools.reduce(jnp.maximum, [jnp.max(s, axis=0, keepdims=True) for s in ss[h]], m_old[h])
             for h in heads]
    alpha = [jnp.exp2(m_old[h] - m_new[h]) for h in heads]
    ps = [[jnp.exp2(s - m_new[h]).astype(BF16) for s in ss[h]] for h in heads]
    ones = jnp.ones((SUM_ROWS, ss[0][0].shape[0]), BF16)
    pvs = [sum(jnp.dot(jnp.concatenate([vt, ones], axis=0), p, preferred_element_type=F32)
               for vt, p in zip(vts[h], ps[h])) for h in heads]
    for h in heads:
        m_ref[h] = m_new[h]
        acc_ref[h] = alpha[h] * acc_old[h] + pvs[h]


def _softmax_tile_update_fixed(ss, vts, m_ref, acc_ref):
    heads = range(len(ss))
    acc_old = [acc_ref[h] for h in heads]
    ps = [[jnp.exp2(s - m_ref[h]).astype(BF16) for s in ss[h]] for h in heads]
    ones = jnp.ones((SUM_ROWS, ss[0][0].shape[0]), BF16)
    pvs = [sum(jnp.dot(jnp.concatenate([vt, ones], axis=0), p, preferred_element_type=F32)
               for vt, p in zip(vts[h], ps[h])) for h in heads]
    for h in heads:
        acc_ref[h] = acc_old[h] + pvs[h]


def _normalised(acc_ref, h):
    return acc_ref[h, 0:LANES, :] / acc_ref[h, LANES:LANES + 1, :]


def _chunk_attn_kernel(q_ref, k_ref, vt_ref, bias_ref, knorm_ref, bmax_ref, o_ref, q2_ref, m_ref, acc_ref):
    bi = pl.program_id(0)
    qi = pl.program_id(2)
    tq = q_ref.shape[1]
    heads = range(acc_ref.shape[0])
    _mask_pair_heads(q_ref, q2_ref)
    ones = jnp.ones((SUBLANES, LANES), BF16)
    qnorm = []
    for h in heads:
        q32 = q2_ref[h].astype(F32)
        n2 = lax.dot_general(ones, (q32 * q32).astype(BF16), NT_DIMS, preferred_element_type=F32)[0:1]
        qnorm.append(jnp.sqrt(n2) * NORM_MARGIN)

    def run(fixed, js):
        update = _softmax_tile_update_fixed if fixed else _softmax_tile_update
        acc_ref[...] = jnp.zeros_like(acc_ref)
        for h in heads:
            m_ref[h] = ((qnorm[h] * knorm_ref[bi, qi, h] + bmax_ref[h]) if fixed
                        else jnp.full((1, tq), NEG_BIG, F32))
        kstarts = [pl.multiple_of((qi - (N_WIN_A - 1) + j) * tq, tq) for j in js]
        ss = [[lax.dot_general(k_ref[0, pl.ds(ks, tq), (h // 2) * LANES:(h // 2 + 1) * LANES], q2_ref[h],
                               NT_DIMS, preferred_element_type=F32) + bias_ref[h, j] for j, ks in zip(js, kstarts)]
              for h in heads]
        vts = [[vt_ref[0, h // 2, :, pl.ds(ks, tq)] for ks in kstarts] for h in heads]
        update(ss, vts, m_ref, acc_ref)
        _store_pair_heads(o_ref, [_normalised(acc_ref, h) for h in heads])

    def run_window(fixed):
        first = jnp.maximum(N_WIN_A - 1 - qi, 0)
        for f in range(N_WIN_A):
            pl.when(first == f)(functools.partial(run, fixed, list(range(f, N_WIN_A))))

    run_window(True)
    smallest = jnp.min(functools.reduce(jnp.minimum, [acc_ref[h, LANES:LANES + 1, :] for h in heads]))
    pl.when(jnp.logical_not(smallest >= 2.0 ** -SAFE_DENOM_LOG2))(lambda: run_window(False))


def _chunk_attn(proj, vt, bias, knorm, bmax):
    b, s, _ = proj.shape
    nh = HEADS_PER_STEP
    assert nh == N_HEADS
    width = nh * HEAD_DIM
    k0 = N_HEADS * HEAD_DIM // width
    resident = dict(pipeline_mode=pl.Buffered(1))
    return pl.pallas_call(
        _chunk_attn_kernel,
        grid=(b, N_HEADS // nh, s // TK),
        in_specs=[
            pl.BlockSpec((1, TK, width), lambda bi, hg, qi: (bi, qi, hg)),
            pl.BlockSpec((1, s, width), lambda bi, hg, qi: (bi, 0, k0 + hg), **resident),
            pl.BlockSpec((1, nh // 2, LANES, s), lambda bi, hg, qi: (bi, hg, 0, 0), **resident),
            pl.BlockSpec((nh, N_WIN_A, TK, TK), lambda bi, hg, qi: (hg, 0, 0, 0), **resident),
            pl.BlockSpec(memory_space=pltpu.SMEM),
            pl.BlockSpec(memory_space=pltpu.SMEM),
        ],
        out_specs=pl.BlockSpec((1, TK, width), lambda bi, hg, qi: (bi, qi, hg)),
        out_shape=jax.ShapeDtypeStruct((b, s, N_HEADS * HEAD_DIM), BF16),
        scratch_shapes=[pltpu.VMEM((nh, TK, LANES), BF16), pltpu.VMEM((nh, 1, TK), F32),
                        pltpu.VMEM((nh, LANES + SUM_ROWS, TK), F32)],
        compiler_params=pltpu.CompilerParams(
            dimension_semantics=("arbitrary", "arbitrary", "arbitrary"), vmem_limit_bytes=VMEM_LIMIT),
        name="chunk_attn",
    )(proj, proj, vt, bias, knorm, bmax)


def _chunk_bias_tiles(rel_bias):
    h = rel_bias.shape[0]
    nq = TK
    nk = N_WIN_A * TK
    period = 1 << (nq + nk - 1).bit_length()
    u = np.arange(period)
    signed = np.where(u < nk, u, u - period)
    idx = np.clip(LEFT_CHUNKS * CHUNK - signed, -MAX_REL, MAX_REL) + MAX_REL
    v = rel_bias.astype(F32)[:, idx] * LOG2E
    toep = jnp.tile(v, (1, nq))[:, :nq * (period - 1)].reshape(h, nq, period - 1)[:, :, :nk]
    r = np.arange(nq)[:, None]
    off = np.arange(nk)[None, :] - CHUNK * (r // CHUNK)
    in_band = (off >= 0) & (off < BAND)
    bias = jnp.where(in_band[None], toep, NEG_BIG)
    return bias.reshape(h, nq, N_WIN_A, TK).transpose(0, 2, 3, 1)


def _stick_kernel(q_ref, k_ref, vt_ref, tri_ref, o_ref, q2_ref, c_ref, acc_ref):
    qi = pl.program_id(2)
    tq = q_ref.shape[1]
    tk = tri_ref.shape[1]
    heads = range(acc_ref.shape[0])
    pair = lambda h: slice((h // 2) * LANES, (h // 2 + 1) * LANES)
    _mask_pair_heads(q_ref, q2_ref)
    c_ref[...] = jnp.zeros_like(c_ref)
    acc_ref[...] = jnp.zeros_like(acc_ref)
    sign_bit = jnp.uint32(0x80000000)

    def tiles(js, masked):
        nt = range(len(js))
        kstarts = [pl.multiple_of(j * tk, tk) for j in js]
        zs = [[lax.dot_general(k_ref[0, pl.ds(kstarts[t], tk), pair(h)], q2_ref[h], NT_DIMS,
                               preferred_element_type=F32) for t in nt] for h in heads]
        c_old = [c_ref[h] for h in heads]
        acc_old = [acc_ref[h] for h in heads]

        def neg_log_keep(z, t):
            neg_abs = lax.bitcast_convert_type(lax.bitcast_convert_type(z, jnp.uint32) | sign_bit, F32)
            nlk = jnp.maximum(z, 0.0) + jnp.log(1.0 + jnp.exp2(neg_abs)) * LOG2E
            return jnp.where(masks[t], nlk, 0.0) if masked[t] else nlk

        masks = [None] * len(js)
        for t in nt:
            if masked[t]:
                key = kstarts[t] + lax.broadcasted_iota(jnp.int32, (tk, tq), 0)
                qry = qi * tq + lax.broadcasted_iota(jnp.int32, (tk, tq), 1)
                masks[t] = key < qry
        nlk = [[neg_log_keep(zs[h][t], t) for t in nt] for h in heads]
        rs = [[jnp.dot(tri_ref[...], nlk[h][t].astype(BF16), preferred_element_type=F32) for t in nt] for h in heads]
        pvs = []
        c_new = []
        for h in heads:
            c = c_old[h]
            pv = None
            for t in nt:
                w = jnp.exp2((zs[h][t] - nlk[h][t]) - rs[h][t][0:tk] - c)
                if masked[t]:
                    w = jnp.where(masks[t], w, 0.0)
                d = jnp.dot(vt_ref[0, h // 2, :, pl.ds(kstarts[t], tk)], w.astype(BF16),
                            preferred_element_type=F32)
                pv = d if pv is None else pv + d
                c = c + rs[h][t][tk:tk + 1]
            pvs.append(pv)
            c_new.append(c)
        for h in heads:
            c_ref[h] = c_new[h]
            acc_ref[h] = acc_old[h] + pvs[h]

    assert tq == tk
    pl.when(qi > 0)(lambda: tiles([qi, qi - 1], [True, False]))
    pl.when(qi == 0)(lambda: tiles([qi], [True]))

    def more(j):
        return jnp.logical_and(j >= 0, jnp.min(c_ref[...]) < STICK_UNDERFLOW_LOG2)

    def body(j):
        tiles([j], [False])
        return j - 1

    lax.while_loop(more, body, qi - 2)
    _store_pair_heads(o_ref, [acc_ref[h] for h in heads])


def _stick_attn(proj, vt, tri):
    b, s, _ = proj.shape
    nh = HEADS_PER_STEP
    width = nh * HEAD_DIM
    q0 = 3 * N_HEADS * HEAD_DIM // width
    k0 = 4 * N_HEADS * HEAD_DIM // width
    resident = dict(pipeline_mode=pl.Buffered(1))
    return pl.pallas_call(
        _stick_kernel,
        grid=(b, N_HEADS // nh, s // TQ),
        in_specs=[
            pl.BlockSpec((1, TQ, width), lambda bi, hg, qi: (bi, qi, q0 + hg)),
            pl.BlockSpec((1, s, width), lambda bi, hg, qi: (bi, 0, k0 + hg), **resident),
            pl.BlockSpec((1, nh // 2, LANES, s), lambda bi, hg, qi: (bi, hg, 0, 0), **resident),
            pl.BlockSpec((TK + SUM_ROWS, TK), lambda bi, hg, qi: (0, 0)),
        ],
        out_specs=pl.BlockSpec((1, TQ, width), lambda bi, hg, qi: (bi, qi, hg)),
        out_shape=jax.ShapeDtypeStruct((b, s, N_HEADS * HEAD_DIM), BF16),
        scratch_shapes=[pltpu.VMEM((nh, TQ, LANES), BF16), pltpu.VMEM((nh, 1, TQ), F32),
                        pltpu.VMEM((nh, LANES, TQ), F32)],
        compiler_params=pltpu.CompilerParams(
            dimension_semantics=("arbitrary", "arbitrary", "arbitrary"), vmem_limit_bytes=VMEM_LIMIT),
        name="stick_attn",
    )(proj, proj, vt, tri)


def _softmax_attn_kernel(*refs, chunk_mask, decay_skip):
    if decay_skip:
        q_ref, k_ref, vt_ref, knorm_ref, fend_ref, o_ref, m_ref, acc_ref = refs
    else:
        q_ref, k_ref, vt_ref, knorm_ref, o_ref, m_ref, acc_ref = refs
    bi = pl.program_id(0)
    qi = pl.program_id(2)
    tq = q_ref.shape[1]
    tk = TK
    heads = range(q_ref.shape[2] // LANES)
    sl = lambda h: slice(h * LANES, (h + 1) * LANES)
    assert tq == tk
    g = TILES_PER_STEP
    rem = qi % g

    lane = lax.broadcasted_iota(jnp.int32, (SUBLANES, LANES), 1)
    dims = jnp.where(lane < (HEAD_DIM if decay_skip else LANES), 1.0, 0.0).astype(BF16)
    pieces = jnp.where((lane >= F_LANE0) & (lane < F_LANE0 + N_PIECES), 1.0, 0.0).astype(BF16)
    qnorm, fq = [], []
    for h in heads:
        q = q_ref[0, :, sl(h)]
        q32 = q.astype(F32)
        n2 = lax.dot_general(dims, (q32 * q32).astype(BF16), NT_DIMS, preferred_element_type=F32)[0:1]
        qnorm.append(jnp.sqrt(n2) * NORM_MARGIN)
        if decay_skip:
            fq.append(lax.dot_general(pieces, q, NT_DIMS, preferred_element_type=F32)[0:1])

    def run(fixed):
        update = _softmax_tile_update_fixed if fixed else _softmax_tile_update
        acc_ref[...] = jnp.zeros_like(acc_ref)
        for h in heads:
            m_ref[h] = (qnorm[h] * knorm_ref[bi, qi, h] + 1.0) if fixed else jnp.full((1, tq), NEG_BIG, F32)

        def tiles(js, masked):
            kstarts = [pl.multiple_of(j * tk, tk) for j in js]

            def scores(h, t):
                s = lax.dot_general(k_ref[0, pl.ds(kstarts[t], tk), sl(h)], q_ref[0, :, sl(h)], NT_DIMS,
                                    preferred_element_type=F32)
                if masked[t]:
                    key = kstarts[t] + lax.broadcasted_iota(jnp.int32, (tk, tq), 0)
                    qry = qi * tq + lax.broadcasted_iota(jnp.int32, (tk, tq), 1)
                    s = jnp.where((key // CHUNK <= qry // CHUNK) if chunk_mask else (key <= qry), s, NEG_BIG)
                return s

            ss = [[scores(h, t) for t in range(len(js))] for h in heads]
            vts = [[vt_ref[0, h // 2, :, pl.ds(ks, tk)] for ks in kstarts] for h in heads]
            update(ss, vts, m_ref, acc_ref)

        def diagonal_step():
            for r in range(g):
                pl.when(rem == r)(functools.partial(tiles, [qi - r + t for t in range(r + 1)], [False] * r + [True]))

        if not decay_skip:
            def body(i, carry):
                tiles([g * i + t for t in range(g)], [False] * g)
                return carry

            lax.fori_loop(0, qi // g, body, 0)
            diagonal_step()
        else:
            diagonal_step()
            top = qi - 1 - rem

            def more(it):
                j = jnp.maximum(top - g * it, 0)
                if fixed:
                    level = [m_ref[h] + jnp.log2(acc_ref[h, LANES:LANES + 1, :]) for h in heads]
                else:
                    level = [m_ref[h] for h in heads]
                gap = [qnorm[h] * knorm_ref[bi, j, h] + fq[h] - fend_ref[bi, j, h] - level[h] for h in heads]
                reach = jnp.max(functools.reduce(jnp.maximum, gap))
                return jnp.logical_and(it < qi // g, reach > -SOFTMAX_UNDERFLOW_LOG2)

            def body(it):
                tiles([top - g * it - t for t in range(g)], [False] * g)
                return it + 1

            lax.while_loop(more, body, 0)
        _store_pair_heads(o_ref, [_normalised(acc_ref, h) for h in heads])

    run(True)
    smallest = jnp.min(functools.reduce(jnp.minimum, [acc_ref[h, LANES:LANES + 1, :] for h in heads]))
    pl.when(jnp.logical_not(smallest >= 2.0 ** -SAFE_DENOM_LOG2))(lambda: run(False))


def _softmax_attn(q, k, vt, knorm, chunk_mask, name, fend=None):
    b, s, _ = q.shape
    nh = HEADS_PER_STEP
    assert nh == N_HEADS
    resident = dict(pipeline_mode=pl.Buffered(1))
    tables = [knorm] + ([] if fend is None else [fend])
    return pl.pallas_call(
        functools.partial(_softmax_attn_kernel, chunk_mask=chunk_mask, decay_skip=fend is not None),
        grid=(b, N_HEADS // nh, s // TQ),
        in_specs=[
            pl.BlockSpec((1, TQ, nh * LANES), lambda bi, hg, qi: (bi, qi, hg)),
            pl.BlockSpec((1, s, nh * LANES), lambda bi, hg, qi: (bi, 0, hg), **resident),
            pl.BlockSpec((1, nh // 2, LANES, s), lambda bi, hg, qi: (bi, hg, 0, 0), **resident),
        ] + [pl.BlockSpec(memory_space=pltpu.SMEM)] * len(tables),
        out_specs=pl.BlockSpec((1, TQ, nh * HEAD_DIM), lambda bi, hg, qi: (bi, qi, hg)),
        out_shape=jax.ShapeDtypeStruct((b, s, N_HEADS * HEAD_DIM), BF16),
        scratch_shapes=[pltpu.VMEM((nh, 1, TQ), F32), pltpu.VMEM((nh, LANES + SUM_ROWS, TQ), F32)],
        compiler_params=pltpu.CompilerParams(
            dimension_semantics=("arbitrary", "arbitrary", "arbitrary"), vmem_limit_bytes=VMEM_LIMIT),
        name=name,
    )(q, k, vt, *tables)


def _out_ffn_kernel(*refs, tiles_per_seq, final_norm):
    if final_norm:
        (x_ref, o1_ref, o2_ref, wo_ref, g_ref, wg_ref, wu_ref, cw_ref, cb_ref, wd_ref, fg_ref,
         out_ref, x1_ref, h_ref, acc_ref, gbuf_ref, tail_ref) = refs
    else:
        (x_ref, o1_ref, o2_ref, wo_ref, g_ref, wg_ref, wu_ref, cw_ref, cb_ref, wd_ref,
         out_ref, x1_ref, h_ref, acc_ref, gbuf_ref, tail_ref) = refs
    i = pl.program_id(0)
    f = pl.program_id(1)
    nf = pl.num_programs(1)
    tm = x_ref.shape[0]
    half = o1_ref.shape[1]

    @pl.when(f == 0)
    def _():
        x1 = (x_ref[...]
              + jnp.dot(o1_ref[...], wo_ref[0:half, :], preferred_element_type=F32)
              + jnp.dot(o2_ref[...], wo_ref[half:2 * half, :], preferred_element_type=F32))
        x1_ref[...] = x1
        h_ref[...] = _rms(x1, g_ref[...]).astype(BF16)
        acc_ref[...] = jnp.zeros_like(acc_ref)

    h = h_ref[...]
    g = jnp.dot(h, wg_ref[...], preferred_element_type=F32)
    u = jnp.dot(h, wu_ref[...], preferred_element_type=F32)

    prev = jnp.where(i % tiles_per_seq == 0, 0.0, tail_ref[f])
    halo = SUBLANES
    gbuf_ref[0:halo, :] = prev
    gbuf_ref[halo:halo + tm, :] = g
    tail_ref[f] = g[tm - halo:tm, :]
    gm1 = gbuf_ref[halo - 1:halo - 1 + tm, :]
    gm2 = gbuf_ref[halo - 2:halo - 2 + tm, :]
    cw = cw_ref[...]
    gc = cw[0:1, :] * gm2 + cw[1:2, :] * gm1 + cw[2:3, :] * g + cb_ref[...]
    y = (gc / (1.0 + jnp.exp(-gc)) * u).astype(BF16)
    acc_ref[...] += jnp.dot(y, wd_ref[...], preferred_element_type=F32)

    @pl.when(f == nf - 1)
    def _():
        res = x1_ref[...] + acc_ref[...]
        if final_norm:
            res = _rms(res, fg_ref[...])
        out_ref[...] = res


def _out_ffn(x2d, o1, o2, wo, g, wg, wu, cw, cb, wd, final_g, seq_len):
    n, d = x2d.shape
    half = o1.shape[1]
    dff = wg.shape[1]
    tm, tf = TM_FFN, TF_FFN
    nf = dff // tf
    final_norm = final_g is not None
    once = dict(pipeline_mode=pl.Buffered(1))
    per_f = once if nf == 1 else {}
    in_specs = [
        pl.BlockSpec((tm, d), lambda i, f: (i, 0)),
        pl.BlockSpec((tm, half), lambda i, f: (i, 0)),
        pl.BlockSpec((tm, half), lambda i, f: (i, 0)),
        pl.BlockSpec((d, d), lambda i, f: (0, 0), **once),
        pl.BlockSpec((1, d), lambda i, f: (0, 0)),
        pl.BlockSpec((d, tf), lambda i, f: (0, f), **per_f),
        pl.BlockSpec((d, tf), lambda i, f: (0, f), **per_f),
        pl.BlockSpec((3, tf), lambda i, f: (0, f)),
        pl.BlockSpec((1, tf), lambda i, f: (0, f)),
        pl.BlockSpec((tf, d), lambda i, f: (f, 0), **per_f),
    ]
    args = [x2d, o1, o2, wo, g, wg, wu, cw, cb, wd]
    if final_norm:
        in_specs.append(pl.BlockSpec((1, d), lambda i, f: (0, 0)))
        args.append(final_g)
    return pl.pallas_call(
        functools.partial(_out_ffn_kernel, tiles_per_seq=seq_len // tm, final_norm=final_norm),
        grid=(n // tm, nf),
        in_specs=in_specs,
        out_specs=pl.BlockSpec((tm, d), lambda i, f: (i, 0)),
        out_shape=jax.ShapeDtypeStruct((n, d), F32),
        scratch_shapes=[
            pltpu.VMEM((tm, d), F32),
            pltpu.VMEM((tm, d), BF16),
            pltpu.VMEM((tm, d), F32),
            pltpu.VMEM((tm + SUBLANES, tf), F32),
            pltpu.VMEM((nf, SUBLANES, tf), F32),
        ],
        compiler_params=pltpu.CompilerParams(
            dimension_semantics=("arbitrary", "arbitrary"), vmem_limit_bytes=VMEM_LIMIT),
        name="out_ffn_final" if final_norm else "out_ffn",
    )(*args)


def _rope_tables(seq_len):
    half = QK_ROPE // 2
    inv = ROPE_THETA ** (-jnp.arange(half, dtype=F32) / half)
    ang = jnp.arange(seq_len, dtype=F32)[:, None] * inv[None, :]
    cos, sin = jnp.cos(ang), jnp.sin(ang)
    ones = jnp.ones((seq_len, QK_NOPE), F32)
    zeros = jnp.zeros((seq_len, QK_NOPE), F32)
    pad1 = jnp.ones((seq_len, LANES - QK_NOPE - QK_ROPE), F32)
    pad0 = jnp.zeros((seq_len, LANES - QK_NOPE - QK_ROPE), F32)
    return (jnp.concatenate([ones, cos, cos, pad1], axis=1),
            jnp.concatenate([zeros, -sin, sin, pad0], axis=1))


def _pad_heads(w, width):
    rows = w.shape[0]
    return jnp.zeros((rows, N_HEADS, LANES), w.dtype).at[:, :, :width].set(
        w.reshape(rows, N_HEADS, width)).reshape(rows, N_HEADS * LANES)


def _cd_weights(w_in, w_uq, w_ukv, b_f):
    d = w_in.shape[0]
    nd = N_HEADS * HEAD_DIM
    o = Q_RANK + KV_RANK + QK_ROPE
    c_q, c_kv, k_rope = w_in[:, :Q_RANK], w_in[:, Q_RANK:Q_RANK + KV_RANK], w_in[:, Q_RANK + KV_RANK:o]
    q_d, k_d, v_d, f_logit = (w_in[:, o:o + nd], w_in[:, o + nd:o + 2 * nd], w_in[:, o + 2 * nd:o + 3 * nd],
                              w_in[:, o + 3 * nd:])
    kr_blk = jnp.zeros((d, LANES), w_in.dtype).at[:, QK_NOPE:QK_NOPE + QK_ROPE].set(k_rope)
    f_blk = jnp.zeros((d, LANES), w_in.dtype).at[:, :N_HEADS].set(f_logit)
    w1 = jnp.concatenate([c_q, c_kv, kr_blk, q_d, k_d, v_d, f_blk], axis=1).astype(BF16)
    wuq = _pad_heads(w_uq, QK_NOPE + QK_ROPE).astype(BF16)
    ukv = w_ukv.reshape(KV_RANK, N_HEADS, QK_NOPE + HEAD_DIM)
    wk = _pad_heads(ukv[:, :, :QK_NOPE].reshape(KV_RANK, N_HEADS * QK_NOPE), QK_NOPE).astype(BF16)
    wv = ukv[:, :, QK_NOPE:].reshape(KV_RANK, N_HEADS * HEAD_DIM).astype(BF16)
    bf = jnp.zeros((1, LANES), F32).at[0, :N_HEADS].set(b_f.astype(F32))
    return w1, wuq, wk, wv, bf


def _forget_selectors():
    selq = np.zeros((LANES, _NPAD), np.float32)
    selk = np.zeros((LANES, _NPAD), np.float32)
    oneq = np.zeros((1, _NPAD), np.float32)
    onek = np.zeros((1, _NPAD), np.float32)
    hsel = np.zeros((_NPAD, LANES), np.float32)
    for h in range(N_HEADS):
        hsel[h * LANES:(h + 1) * LANES, h] = 1.0
        for j in range(N_PIECES):
            selq[j * N_HEADS + h, h * LANES + F_LANE0 + j] = 1.0
            onek[0, h * LANES + F_LANE0 + j] = 1.0
            selk[j * N_HEADS + h, h * LANES + ONE_LANE0 + j] = -1.0
            oneq[0, h * LANES + ONE_LANE0 + j] = 1.0
    return (jnp.asarray(selq, BF16), jnp.asarray(selk, BF16), jnp.asarray(oneq), jnp.asarray(onek),
            jnp.asarray(hsel, BF16))


def kernel(x, ab_norm, ab_w_in, ab_rel_bias, ab_w_o, cd_norm, cd_w_in, cd_q_norm, cd_w_uq, cd_kv_norm, cd_w_ukv,
           cd_b_f, cd_w_o, ffn_norm, ffn_w_gate, ffn_w_up, ffn_conv_w, ffn_conv_b, ffn_w_down, final_norm):
    b, s, d = x.shape
    n = b * s

    qscale = jnp.full((N_HEADS * HEAD_DIM,), HEAD_DIM ** -0.5 * LOG2E, F32)
    one = jnp.ones((2 * N_HEADS * HEAD_DIM,), F32)
    colscale = jnp.concatenate([qscale, one, qscale, one])[None, :]
    per_head = lambda tab: tab[:, :, 0, :N_HEADS]
    hsel_a = jnp.asarray(np.repeat(np.eye(N_HEADS, LANES, dtype=np.float32), HEAD_DIM, axis=0), BF16)
    proj, vat, vbt, kanorm = _ab_proj(x, ab_norm[0][None, :], ab_w_in[0].astype(BF16), colscale, hsel_a)
    bias_max = jnp.max(ab_rel_bias[0].astype(F32), axis=1) * LOG2E + 1.0
    oa = _chunk_attn(proj, vat, _chunk_bias_tiles(ab_rel_bias[0]), per_head(kanorm), bias_max)
    tri = jnp.asarray(np.concatenate([np.triu(np.ones((TK, TK), np.float32), k=1),
                                      np.ones((SUM_ROWS, TK), np.float32)]), BF16)
    ob = _stick_attn(proj, vbt, tri)
    x2d = _out_ffn(x.reshape(n, d), oa.reshape(n, -1), ob.reshape(n, -1), ab_w_o[0].astype(BF16),
                   ffn_norm[0][None, :], ffn_w_gate[0].astype(BF16), ffn_w_up[0].astype(BF16), ffn_conv_w[0],
                   ffn_conv_b[0][None, :], ffn_w_down[0].astype(BF16), None, s)

    w1, wuq, wk, wv, bf = _cd_weights(cd_w_in[0], cd_w_uq[0], cd_w_ukv[0], cd_b_f[0])
    cos_t, sin_t = _rope_tables(s)
    qc, kc, vct, qd, kd, vdt, kcnorm, kdnorm, fend = _cd_proj(
        x2d.reshape(b, s, d), cd_norm[0][None, :], w1, cd_q_norm[0][None, :], wuq, cd_kv_norm[0][None, :], wk, wv, bf,
        cos_t, sin_t, *_forget_selectors())
    oc = _softmax_attn(qc, kc, vct, per_head(kcnorm), True, "mla_attn")
    od = _softmax_attn(qd, kd, vdt, per_head(kdnorm), False, "fox_attn", per_head(fend))
    out = _out_ffn(x2d, oc.reshape(n, -1), od.reshape(n, -1), cd_w_o[0].astype(BF16), ffn_norm[1][None, :],
                   ffn_w_gate[1].astype(BF16), ffn_w_up[1].astype(BF16), ffn_conv_w[1], ffn_conv_b[1][None, :],
                   ffn_w_down[1].astype(BF16), final_norm[None, :], s)
    return out.reshape(b, s, d)
```
